```python
import jax, jax.numpy as jnp
from jax import lax
import numpy as np

D_MODEL = 1024
BATCH = 8
SEQ = 4096
DEPTH = 2
DEC_BATCH = 32
DEC_SEQ = 4
PAST_LEN = 16384
PAGE_SIZE = 128

N_EVEN = (DEPTH + 1) // 2
N_ODD = DEPTH // 2
EPS = 1e-6
D_FF = ((8 * D_MODEL // 3 + 255) // 256) * 256

POOL_WIDTH = D_MODEL // 4
POOL_WINDOWS = (2, 4, 8, 16)
POOL_GROUPS = len(POOL_WINDOWS)
POOL_GDIM = POOL_WIDTH // POOL_GROUPS
POOL_BUF = max(POOL_WINDOWS) - 1
ML_HEADS = 4
ML_WIDTH = D_MODEL - POOL_WIDTH
ML_DH = ML_WIDTH // ML_HEADS
ML_CHUNK = 64
FORGET_BIAS = 3.0
EVEN_IN = POOL_WIDTH + 4 * ML_WIDTH + 2 * ML_HEADS

NSA_HEADS = 12
HEAD_DIM = 64
NSA_WIDTH = NSA_HEADS * HEAD_DIM
KV_HEADS = 2
GQ = NSA_HEADS // KV_HEADS
KV_W = KV_HEADS * HEAD_DIM
CMP_STRIDE = 16
CMP_BLOCK = 2 * CMP_STRIDE
CMP_LEAD = CMP_BLOCK // CMP_STRIDE - 1
SEL_BLOCK = 64
RATIO = SEL_BLOCK // CMP_STRIDE
SEL_TOPN = 16
WINDOW = 512
Q_BLOCK = 64
CONV_CH = D_MODEL - NSA_WIDTH
CONV_K = 3
ODD_IN = NSA_WIDTH + 6 * KV_W + 3 * NSA_HEADS + 3 * CONV_CH

kernel_name = 'hybrid_pool_mlstm_nsa_conv_step'


def _rmsnorm(x, g):
    xf = x.astype(jnp.float32)
    y = xf * lax.rsqrt(jnp.mean(xf * xf, axis=-1, keepdims=True) + EPS)
    return (y * g.astype(jnp.float32)).astype(x.dtype)


def _swiglu(x, w_in, w_out):
    a, b = jnp.split(x @ w_in, 2, axis=-1)
    return (jax.nn.silu(a) * b) @ w_out


def _masked_softmax(s, mask):
    s = jnp.where(mask, s, -jnp.inf)
    m = jnp.max(s, axis=-1, keepdims=True)
    m = jnp.where(jnp.isfinite(m), m, 0.0)
    e = jnp.exp(s - m)
    return e / jnp.maximum(jnp.sum(e, axis=-1, keepdims=True), 1e-30)


def _alibi_slopes():
    sl = 2.0 ** (-8.0 * np.arange(1, NSA_HEADS + 1) / NSA_HEADS)
    return jnp.asarray(sl, jnp.float32).reshape(KV_HEADS, GQ)


def _pool_mix(u, prefix, pos0, w_pool, scale):
    B, T, _ = u.shape
    full = jnp.concatenate([prefix, u], axis=1)
    cs = jnp.cumsum(full.astype(jnp.float32), axis=1)
    cs = jnp.concatenate([jnp.zeros((B, 1, POOL_WIDTH), jnp.float32), cs], axis=1)
    pos = pos0 + jnp.arange(T)
    end = cs[:, POOL_BUF + 1:]
    means = []
    for g, w in enumerate(POOL_WINDOWS):
        lo, hi = g * POOL_GDIM, (g + 1) * POOL_GDIM
        win_sum = end[..., lo:hi] - cs[:, POOL_BUF + 1 - w:POOL_BUF + 1 - w + T, lo:hi]
        cnt = jnp.minimum(pos + 1, w).astype(jnp.float32)
        means.append(win_sum / cnt[None, :, None])
    mixed = (jnp.concatenate(means, axis=-1) - u.astype(jnp.float32)).astype(u.dtype)
    y = jnp.einsum('btgc,gcd->btgd', mixed.reshape(B, T, POOL_GROUPS, POOL_GDIM), w_pool)
    return y.reshape(B, T, POOL_WIDTH) * scale, full[:, -POOL_BUF:]


def _mlstm(q, k, v, ig, fg, c0, n0, m0):
    f32 = jnp.float32
    B, T, H, D = q.shape
    L = ML_CHUNK if T % ML_CHUNK == 0 else T
    nc = T // L

    def chunks(a):
        a = a.astype(f32).reshape((B, nc, L) + a.shape[2:])
        return jnp.swapaxes(jnp.moveaxis(a, 1, 0), 2, 3)

    xs = (chunks(q), chunks(k.astype(f32) * (D ** -0.5)), chunks(v), chunks(ig),
          chunks(jax.nn.log_sigmoid(fg.astype(f32))))
    causal = jnp.tril(jnp.ones((L, L), dtype=bool))

    def step(carry, inp):
        c, n, m = carry
        qc, kc, vc, ic, fc = inp
        b = jnp.cumsum(fc, axis=-1)
        dmat = jnp.where(causal, b[..., :, None] - b[..., None, :] + ic[..., None, :], -jnp.inf)
        inter = b + m[..., None]
        mt = jnp.maximum(jnp.max(dmat, axis=-1), inter)
        w = jnp.exp(dmat - mt[..., None])
        a = jnp.exp(inter - mt)
        s = jnp.einsum('bhtd,bhsd->bhts', qc, kc) * w
        num = jnp.einsum('bhts,bhse->bhte', s, vc) + a[..., None] * jnp.einsum('bhed,bhtd->bhte', c, qc)
        den = jnp.sum(s, axis=-1) + a * jnp.einsum('bhd,bhtd->bht', n, qc)
        h = num / jnp.maximum(jnp.abs(den), jnp.exp(-mt))[..., None]
        g_end = b[..., -1:] - b + ic
        m_new = jnp.maximum(b[..., -1] + m, jnp.max(g_end, axis=-1))
        wk = jnp.exp(g_end - m_new[..., None])
        decay = jnp.exp(b[..., -1] + m - m_new)
        c_new = decay[..., None, None] * c + jnp.einsum('bhs,bhse,bhsd->bhed', wk, vc, kc)
        n_new = decay[..., None] * n + jnp.einsum('bhs,bhsd->bhd', wk, kc)
        return (c_new, n_new, m_new), h

    (c, n, m), h = lax.scan(step, (c0.astype(f32), n0.astype(f32), m0.astype(f32)), xs)
    h = jnp.moveaxis(jnp.swapaxes(h, 2, 3), 0, 1).reshape(B, T, H, D)
    return h, c, n, m


def _even_mixer(h, pos0, pool_buf, c0, n0, m0, w_in, b_gate, w_pool, pool_scale, w_out):
    B, T, _ = h.shape
    z = h @ w_in
    s1 = POOL_WIDTH
    s2 = s1 + ML_WIDTH
    s3 = s2 + ML_WIDTH
    s4 = s3 + ML_WIDTH
    s5 = s4 + ML_WIDTH
    u, q, k, v, og, gt = jnp.split(z, [s1, s2, s3, s4, s5], axis=-1)
    y_pool, new_pool = _pool_mix(u, pool_buf, pos0, w_pool, pool_scale)
    gt = gt + b_gate
    hd = lambda a: a.reshape(B, T, ML_HEADS, ML_DH)
    hm, c, n, m = _mlstm(hd(q), hd(k), hd(v), gt[..., :ML_HEADS], gt[..., ML_HEADS:], c0, n0, m0)
    y_ml = jax.nn.sigmoid(og) * hm.reshape(B, T, ML_WIDTH).astype(og.dtype)
    y = jnp.concatenate([y_pool, y_ml], axis=-1) @ w_out
    return y, new_pool, c, n, m


def _compress(kv, pos_w, w, b):
    B, L = kv.shape[:2]
    nch = L // CMP_STRIDE
    ch = kv[:, :nch * CMP_STRIDE].reshape(B, nch, CMP_STRIDE, 2, KV_HEADS, HEAD_DIM)
    pre = (jnp.einsum('bnjkgd,kj->bnkgd', ch[:, :-1], pos_w[:, :CMP_STRIDE]) +
           jnp.einsum('bnjkgd,kj->bnkgd', ch[:, 1:], pos_w[:, CMP_STRIDE:]))
    return jnp.einsum('bnkgd,kde->bnkge', pre, w) + b[None, None, :, None, :]


def _sel_blocks(kv):
    B, L = kv.shape[:2]
    ns = -(-L // SEL_BLOCK)
    kv = jnp.pad(kv, ((0, 0), (0, ns * SEL_BLOCK - L), (0, 0), (0, 0), (0, 0)))
    kv = kv.reshape(B, ns, SEL_BLOCK, 2, KV_HEADS, HEAD_DIM)
    return jnp.transpose(kv, (0, 4, 1, 2, 3, 5))


def _nsa_query_block(q, gates, p0, kv_cmp, kv_sel, band, band_p0):
    f32 = jnp.float32
    B, Q = q.shape[:2]
    qg = q.reshape(B, Q, KV_HEADS, GQ, HEAD_DIM) * (HEAD_DIM ** -0.5)
    slopes = _alibi_slopes()[None, None, :, :, None]
    tpos = p0 + jnp.arange(Q)

    nc = kv_cmp.shape[1]
    cpos = CMP_STRIDE * jnp.arange(nc) + (CMP_BLOCK - 1)
    dist = tpos[:, None] - cpos[None, :]
    s = jnp.einsum('bqgid,bngd->bqgin', qg, kv_cmp[:, :, 0], preferred_element_type=f32)
    s = s - slopes * dist[None, :, None, None, :].astype(f32)
    p_cmp = _masked_softmax(s, (dist >= 0)[None, :, None, None, :])
    o_cmp = jnp.einsum('bqgin,bngd->bqgid', p_cmp, kv_cmp[:, :, 1])

    ns = kv_sel.shape[2]
    imp = jnp.sum(p_cmp, axis=3)
    pad_r = RATIO * (ns + 1) - nc - CMP_LEAD
    imp = jnp.pad(imp, ((0, 0), (0, 0), (0, 0), (CMP_LEAD, pad_r))).reshape(B, Q, KV_HEADS, ns + 1, RATIO)
    score = imp[..., :-1, :].sum(-1) + imp[..., 1:, :CMP_LEAD].sum(-1)
    blk = jnp.arange(ns)[None, :]
    cur = (tpos // SEL_BLOCK)[:, None]
    visible = blk <= cur
    forced = (blk == 0) | (blk == cur) | (blk == cur - 1)
    score = jnp.where(forced[None, :, None, :], jnp.inf,
                      jnp.where(visible[None, :, None, :], score, -jnp.inf))
    n_sel = min(SEL_TOPN, ns)
    _, idx = lax.top_k(score, n_sel)
    bi = jnp.arange(B)[:, None, None, None]
    gi = jnp.arange(KV_HEADS)[None, None, :, None]
    sel = kv_sel[bi, gi, idx]
    spos = idx[..., None] * SEL_BLOCK + jnp.arange(SEL_BLOCK)
    sdist = tpos[None, :, None, None, None] - spos
    s = jnp.einsum('bqgid,bqgnrd->bqginr', qg, sel[..., 0, :], preferred_element_type=f32)
    s = s - slopes[..., None] * sdist[:, :, :, None].astype(f32)
    s = s.reshape(B, Q, KV_HEADS, GQ, n_sel * SEL_BLOCK)
    smask = (sdist >= 0).reshape(B, Q, KV_HEADS, 1, n_sel * SEL_BLOCK)
    p = _masked_softmax(s, smask).reshape(B, Q, KV_HEADS, GQ, n_sel, SEL_BLOCK)
    o_sel = jnp.einsum('bqginr,bqgnrd->bqgid', p, sel[..., 1, :])

    wb = band.shape[1]
    bpos = band_p0 + jnp.arange(wb)
    wdist = tpos[:, None] - bpos[None, :]
    wmask = (wdist >= 0) & (wdist < WINDOW) & (bpos[None, :] >= 0)
    s = jnp.einsum('bqgid,bkgd->bqgik', qg, band[:, :, 0], preferred_element_type=f32)
    s = s - slopes * wdist[None, :, None, None, :].astype(f32)
    p = _masked_softmax(s, wmask[None, :, None, None, :])
    o_win = jnp.einsum('bqgik,bkgd->bqgid', p, band[:, :, 1])

    g = gates.reshape(B, Q, KV_HEADS, GQ, 3).astype(f32)
    o = g[..., 0:1] * o_cmp + g[..., 1:2] * o_sel + g[..., 2:3] * o_win
    return o.reshape(B, Q, NSA_WIDTH).astype(q.dtype)


def _odd_mixer(h, pos0, past_kv, win_buf, conv_buf, w_in, cmp_pos_w, cmp_w, cmp_b, conv_w, w_out):
    B, T, _ = h.shape
    z = h @ w_in
    o1 = NSA_WIDTH
    o2 = o1 + 6 * KV_W
    o3 = o2 + 3 * NSA_HEADS
    o4 = o3 + CONV_CH
    o5 = o4 + CONV_CH
    q, kv, gt, bg, cg, xin = jnp.split(z, [o1, o2, o3, o4, o5], axis=-1)
    q = q.reshape(B, T, NSA_HEADS, HEAD_DIM)
    kv = kv.reshape(B, T, 3, 2, KV_HEADS, HEAD_DIM)
    gates = jax.nn.sigmoid(gt.astype(jnp.float32)).reshape(B, T, NSA_HEADS, 3)
    kv_pg, kv_win = kv[:, :, :2], kv[:, :, 2]
    kv_all = kv_pg if past_kv is None else jnp.concatenate([past_kv, kv_pg], axis=1)
    kv_cmp = _compress(kv_all[:, :, 0], cmp_pos_w, cmp_w, cmp_b)
    kv_sel = _sel_blocks(kv_all[:, :, 1])
    if past_kv is None:
        win_src = jnp.concatenate([jnp.zeros((B, WINDOW - 1) + kv_win.shape[2:], kv_win.dtype), kv_win], axis=1)
        nqb = T // Q_BLOCK
        qb = jnp.moveaxis(q.reshape(B, nqb, Q_BLOCK, NSA_HEADS, HEAD_DIM), 1, 0)
        gb = jnp.moveaxis(gates.reshape(B, nqb, Q_BLOCK, NSA_HEADS, 3), 1, 0)

        def one_block(args):
            bi, qq, gg = args
            p0 = pos0 + bi * Q_BLOCK
            band = lax.dynamic_slice_in_dim(win_src, bi * Q_BLOCK, WINDOW - 1 + Q_BLOCK, axis=1)
            return _nsa_query_block(qq, gg, p0, kv_cmp, kv_sel, band, p0 - (WINDOW - 1))

        o = lax.map(one_block, (jnp.arange(nqb), qb, gb))
        o = jnp.moveaxis(o, 0, 1).reshape(B, T, NSA_WIDTH)
        new_win = kv_win[:, -min(WINDOW, T):]
    else:
        wlen = win_buf.shape[1]
        win_src = jnp.concatenate([win_buf, kv_win], axis=1)
        o = _nsa_query_block(q, gates, pos0, kv_cmp, kv_sel, win_src, pos0 - wlen)
        new_win = win_src[:, -wlen:]
    u = cg * xin
    full = jnp.concatenate([conv_buf, u], axis=1)
    conv = full[:, 0:T] * conv_w[0]
    for j in range(1, CONV_K):
        conv = conv + full[:, j:j + T] * conv_w[j]
    y = jnp.concatenate([o, bg * conv], axis=-1) @ w_out
    new_rows = kv_pg.reshape(B, T, 4, KV_HEADS, HEAD_DIM)
    return y, new_rows, new_win, full[:, -(CONV_K - 1):]


def _gather_past(pool, page_table):
    bd, npg = page_table.shape
    pages = pool[page_table]
    return pages.reshape(bd, npg * PAGE_SIZE, 2, 2, KV_HEADS, HEAD_DIM)


def _trunk(x, pos0, pool_buf, ml_c, ml_n, ml_m, kv_cache, page_table, win_buf, conv_buf,
           norm_g, final_g, w_ffn_in, w_ffn_out, w_in_even, b_gate_even, w_pool, pool_scale,
           w_out_even, w_in_odd, cmp_pos_w, cmp_w, cmp_b, conv_w, w_out_odd):
    pools, cs, ns, ms, rows, wins, convs = [], [], [], [], [], [], []
    for l in range(DEPTH):
        x = x + 0.5 * _swiglu(_rmsnorm(x, norm_g[l, 0]), w_ffn_in[l, 0], w_ffn_out[l, 0])
        h = _rmsnorm(x, norm_g[l, 1])
        j = l // 2
        if l % 2 == 0:
            y, pb, c, n, m = _even_mixer(h, pos0, pool_buf[j], ml_c[j], ml_n[j], ml_m[j], w_in_even[j],
                                         b_gate_even[j], w_pool[j], pool_scale[j], w_out_even[j])
            pools.append(pb)
            cs.append(c)
            ns.append(n)
            ms.append(m)
        else:
            past = None if kv_cache is None else _gather_past(kv_cache[j], page_table)
            wb = None if win_buf is None else win_buf[j]
            y, r, wn, cn = _odd_mixer(h, pos0, past, wb, conv_buf[j], w_in_odd[j], cmp_pos_w[j],
                                      cmp_w[j], cmp_b[j], conv_w[j], w_out_odd[j])
            rows.append(r)
            wins.append(wn)
            convs.append(cn)
        x = x + y
        x = x + 0.5 * _swiglu(_rmsnorm(x, norm_g[l, 2]), w_ffn_in[l, 1], w_ffn_out[l, 1])
    states = (jnp.stack(pools), jnp.stack(cs), jnp.stack(ns), jnp.stack(ms),
              jnp.stack(rows), jnp.stack(wins), jnp.stack(convs))
    return _rmsnorm(x, final_g), states


def setup_inputs(seed: int = 0) -> dict:
    key = jax.random.key(seed)
    ks = jax.random.split(key, 26)
    f32 = jnp.float32

    def nrm(k, shape, scale=1.0):
        return scale * jax.random.normal(k, shape, f32)

    n_pages = PAST_LEN // PAGE_SIZE
    n_used = DEC_BATCH * n_pages
    n_pool = n_used + (n_used + 3) // 4
    win_len = min(WINDOW, PAST_LEN)
    page_table = jax.random.permutation(ks[9], n_pool)[:n_used].reshape(DEC_BATCH, n_pages).astype(jnp.int32)
    b_gate_even = jnp.concatenate([nrm(ks[15], (N_EVEN, ML_HEADS), 0.1),
                                   FORGET_BIAS + nrm(ks[16], (N_EVEN, ML_HEADS), 0.1)], axis=-1)
    return {
        'x_prompt': nrm(ks[0], (BATCH, SEQ, D_MODEL)),
        'x_sample': nrm(ks[1], (DEC_BATCH, DEC_SEQ, D_MODEL)),
        'state_pool': nrm(ks[2], (N_EVEN, DEC_BATCH, POOL_BUF, POOL_WIDTH)),
        'state_mlstm_c': nrm(ks[3], (N_EVEN, DEC_BATCH, ML_HEADS, ML_DH, ML_DH), 0.3),
        'state_mlstm_n': nrm(ks[4], (N_EVEN, DEC_BATCH, ML_HEADS, ML_DH), 0.3),
        'state_mlstm_m': nrm(ks[5], (N_EVEN, DEC_BATCH, ML_HEADS)),
        'cache_nsa_kv': nrm(ks[6], (N_ODD, n_pool, PAGE_SIZE, 4, KV_HEADS, HEAD_DIM)),
        'state_win_kv': nrm(ks[7], (N_ODD, DEC_BATCH, win_len, 2, KV_HEADS, HEAD_DIM)),
        'state_conv': nrm(ks[8], (N_ODD, DEC_BATCH, CONV_K - 1, CONV_CH)),
        'page_table': page_table,
        'norm_g': 1.0 + nrm(ks[10], (DEPTH, 3, D_MODEL), 0.02),
        'final_g': 1.0 + nrm(ks[11], (D_MODEL,), 0.02),
        'w_ffn_in': nrm(ks[12], (DEPTH, 2, D_MODEL, 2 * D_FF), D_MODEL ** -0.5),
        'w_ffn_out': nrm(ks[13], (DEPTH, 2, D_FF, D_MODEL), D_FF ** -0.5),
        'w_in_even': nrm(ks[14], (N_EVEN, D_MODEL, EVEN_IN), D_MODEL ** -0.5),
        'b_gate_even': b_gate_even,
        'w_pool': nrm(ks[17], (N_EVEN, POOL_GROUPS, POOL_GDIM, POOL_GDIM), POOL_GDIM ** -0.5),
        'pool_scale': 1.0 + nrm(ks[18], (N_EVEN, POOL_WIDTH), 0.02),
        'w_out_even': nrm(ks[19], (N_EVEN, POOL_WIDTH + ML_WIDTH, D_MODEL), D_MODEL ** -0.5),
        'w_in_odd': nrm(ks[20], (N_ODD, D_MODEL, ODD_IN), D_MODEL ** -0.5),
        'cmp_pos_w': (1.0 + nrm(ks[21], (N_ODD, 2, CMP_BLOCK), 0.1)) * (CMP_BLOCK ** -0.5),
        'cmp_w': nrm(ks[22], (N_ODD, 2, HEAD_DIM, HEAD_DIM), HEAD_DIM ** -0.5),
        'cmp_b': nrm(ks[23], (N_ODD, 2, HEAD_DIM), 0.02),
        'conv_w': nrm(ks[24], (N_ODD, CONV_K, CONV_CH), CONV_K ** -0.5),
        'w_out_odd': nrm(ks[25], (N_ODD, NSA_WIDTH + CONV_CH, D_MODEL), D_MODEL ** -0.5),
    }


def reference(x_prompt, x_sample, state_pool, state_mlstm_c, state_mlstm_n, state_mlstm_m,
              cache_nsa_kv, state_win_kv, state_conv, page_table,
              norm_g, final_g, w_ffn_in, w_ffn_out, w_in_even, b_gate_even, w_pool, pool_scale,
              w_out_even, w_in_odd, cmp_pos_w, cmp_w, cmp_b, conv_w, w_out_odd):
    bp = x_prompt.shape[0]
    dt = x_prompt.dtype
    f32 = jnp.float32
    pool0 = jnp.zeros((N_EVEN, bp, POOL_BUF, POOL_WIDTH), dt)
    c0 = jnp.zeros((N_EVEN, bp, ML_HEADS, ML_DH, ML_DH), f32)
    n0 = jnp.zeros((N_EVEN, bp, ML_HEADS, ML_DH), f32)
    m0 = jnp.zeros((N_EVEN, bp, ML_HEADS), f32)
    conv0 = jnp.zeros((N_ODD, bp, CONV_K - 1, CONV_CH), dt)
    y_prompt, (pool_p, c_p, n_p, m_p, kv_p, win_p, conv_p) = _trunk(
        x_prompt, 0, pool0, c0, n0, m0, None, None, None, conv0,
        norm_g, final_g, w_ffn_in, w_ffn_out, w_in_even, b_gate_even, w_pool, pool_scale,
        w_out_even, w_in_odd, cmp_pos_w, cmp_w, cmp_b, conv_w, w_out_odd)
    y_sample, (pool_s, c_s, n_s, m_s, kv_s, win_s, conv_s) = _trunk(
        x_sample, PAST_LEN, state_pool, state_mlstm_c, state_mlstm_n, state_mlstm_m,
        cache_nsa_kv, page_table, state_win_kv, state_conv,
        norm_g, final_g, w_ffn_in, w_ffn_out, w_in_even, b_gate_even, w_pool, pool_scale,
        w_out_even, w_in_odd, cmp_pos_w, cmp_w, cmp_b, conv_w, w_out_odd)
    return (y_prompt, y_sample, pool_p, pool_s, c_p, c_s, n_p, n_s, m_p, m_s,
            kv_p, kv_s, win_p, win_s, conv_p, conv_s)
```

```python
import functools

import numpy as np
import jax
import jax.numpy as jnp
from jax import lax
from jax.experimental import pallas as pl
from jax.experimental.pallas import tpu as pltpu

F32 = jnp.float32
BF16 = jnp.bfloat16
HIGHEST = lax.Precision.HIGHEST

EPS = 1e-6
POOL_WINDOWS = (2, 4, 8, 16)
POOL_GDIM = 64
POOL_WIDTH = 256
POOL_HALO = 16
ML_HEADS = 4
ML_DH = 192
ML_WIDTH = ML_HEADS * ML_DH
NSA_HEADS = 12
HEAD_DIM = 64
NSA_WIDTH = NSA_HEADS * HEAD_DIM
KV_HEADS = 2
GQ = NSA_HEADS // KV_HEADS
CMP_STRIDE = 16
SEL_BLOCK = 64
SEL_TOPN = 16
WINDOW = 512
Q_BLOCK = 64
PAGE_SIZE = 128
CONV_CH = 256
CONV_K = 3
CONV_HALO = 8

VMEM_LIMIT = 56 * 1024 * 1024
NEG = -1e30


def _params(sem, vmem=VMEM_LIMIT):
    return pltpu.CompilerParams(dimension_semantics=sem, vmem_limit_bytes=vmem)


def _const_spec(shape):
    nd = len(shape)
    return pl.BlockSpec(shape, lambda *_: (0,) * nd, pipeline_mode=pl.Buffered(1))


def _rms(x, g):
    return x * lax.rsqrt(jnp.mean(x * x, axis=-1, keepdims=True) + EPS) * g


def _dot(a, b):
    return jnp.dot(a, b, preferred_element_type=F32)


def _dot_nt(a, b):
    return lax.dot_general(a, b, (((1,), (1,)), ((), ())), preferred_element_type=F32)


def _ffn_body(x_ref, g_ref, win_ref, wout_ref, *rest, d_ff, chunks, has_final):
    o_ref = rest[-1]
    x = x_ref[...]
    hn = _rms(x, g_ref[...]).astype(BF16)
    acc = jnp.zeros(x.shape, F32)
    off = 0
    for fc in chunks:
        a = _dot(hn, win_ref[:, off:off + fc])
        b = _dot(hn, win_ref[:, d_ff + off:d_ff + off + fc])
        act = (a * jax.nn.sigmoid(a) * b).astype(BF16)
        acc = acc + _dot(act, wout_ref[off:off + fc, :])
        off += fc
    y = x + 0.5 * acc
    if has_final:
        y = _rms(y, rest[0][...])
    o_ref[...] = y


def _ffn(x, g, w_in, w_out, final_g=None):
    n, d = x.shape
    d_ff = w_out.shape[0]
    tm = 512 if n % 512 == 0 else n
    chunks, left = [], d_ff
    while left:
        chunks.append(min(1024, left))
        left -= chunks[-1]
    row = pl.BlockSpec((tm, d), lambda i: (i, 0))
    in_specs = [row, _const_spec((1, d)), _const_spec(w_in.shape), _const_spec(w_out.shape)]
    args = [x, g.reshape(1, d), w_in, w_out]
    if final_g is not None:
        in_specs.append(_const_spec((1, d)))
        args.append(final_g.reshape(1, d))
    return pl.pallas_call(
        functools.partial(_ffn_body, d_ff=d_ff, chunks=tuple(chunks), has_final=final_g is not None),
        grid=(n // tm,), in_specs=in_specs, out_specs=row,
        out_shape=jax.ShapeDtypeStruct((n, d), F32),
        compiler_params=_params(("parallel",)), name="ffn")(*args)


def _even_in_body(x_ref, g_ref, wu_ref, wh_ref, wvt_ref, wg_ref, wgt_ref, bg_ref, bgt_ref,
                  u_ref, q_ref, k_ref, v_ref, og_ref, vt_ref, gc_ref, gr_ref):
    hn = _rms(x_ref[...], g_ref[...]).astype(BF16)
    u_ref[...] = _dot(hn, wu_ref[...])
    for h in range(ML_HEADS):
        q_ref[h] = _dot(hn, wh_ref[h])
        k_ref[h] = _dot(hn, wh_ref[ML_HEADS + h])
        v_ref[h] = _dot(hn, wh_ref[2 * ML_HEADS + h])
        og_ref[h] = _dot(hn, wh_ref[3 * ML_HEADS + h])
        vt_ref[h] = _dot_nt(wvt_ref[h], hn)
    gc_ref[...] = _dot(hn, wg_ref[...]) + bg_ref[...]
    gr_ref[...] = _dot_nt(wgt_ref[...], hn) + bgt_ref[...]


def _even_in(x, g, w_in, b_gate):
    n, d = x.shape
    tm = 512 if n % 512 == 0 else n
    h4 = ML_HEADS
    wu = w_in[:, :POOL_WIDTH].astype(BF16)
    wh = w_in[:, POOL_WIDTH:POOL_WIDTH + 4 * ML_WIDTH].reshape(d, 4 * h4, ML_DH).transpose(1, 0, 2).astype(BF16)
    wvt = wh[2 * h4:3 * h4].transpose(0, 2, 1)
    wg = w_in[:, POOL_WIDTH + 4 * ML_WIDTH:].astype(BF16)
    row = lambda w: pl.BlockSpec((tm, w), lambda i: (i, 0))
    hrow = pl.BlockSpec((h4, tm, ML_DH), lambda i: (0, i, 0))
    hsh = jax.ShapeDtypeStruct((h4, n, ML_DH), F32)
    return pl.pallas_call(
        _even_in_body, grid=(n // tm,),
        in_specs=[row(d), _const_spec((1, d)), _const_spec(wu.shape), _const_spec(wh.shape),
                  _const_spec(wvt.shape), _const_spec(wg.shape), _const_spec((2 * h4, d)),
                  _const_spec((1, 2 * h4)), _const_spec((2 * h4, 1))],
        out_specs=[row(POOL_WIDTH), hrow, hrow, hrow, hrow,
                   pl.BlockSpec((h4, ML_DH, tm), lambda i: (0, 0, i)),
                   row(2 * h4), pl.BlockSpec((2 * h4, tm), lambda i: (0, i))],
        out_shape=[jax.ShapeDtypeStruct((n, POOL_WIDTH), F32), hsh, hsh, hsh, hsh,
                   jax.ShapeDtypeStruct((h4, ML_DH, n), F32),
                   jax.ShapeDtypeStruct((n, 2 * h4), F32), jax.ShapeDtypeStruct((2 * h4, n), F32)],
        compiler_params=_params(("parallel",)), name="even_in")(
            x, g.reshape(1, d), wu, wh, wvt, wg, wg.T, b_gate.reshape(1, -1), b_gate.reshape(-1, 1))


def _pool_body(u_ref, pre_ref, w_ref, sc_ref, y_ref, carry, full, *, tb, pos0):
    t = pl.program_id(1)

    @pl.when(t == 0)
    def _():
        carry[...] = pre_ref[...]

    u = u_ref[...]
    full[0:POOL_HALO] = carry[...]
    full[POOL_HALO:] = u
    acc = full[...]
    sums = []
    for sh in (1, 2, 4, 8):
        acc = acc + pltpu.roll(acc, sh, 0)
        sums.append(acc[POOL_HALO:])
    lane = lax.broadcasted_iota(jnp.int32, (tb, POOL_WIDTH), 1)
    grp = lane // POOL_GDIM
    win = jnp.where(grp == 0, sums[0], jnp.where(grp == 1, sums[1], jnp.where(grp == 2, sums[2], sums[3])))
    width = jnp.where(grp == 0, 2, jnp.where(grp == 1, 4, jnp.where(grp == 2, 8, 16)))
    pos = pos0 + t * tb + lax.broadcasted_iota(jnp.int32, (tb, POOL_WIDTH), 0)
    cnt = jnp.minimum(pos + 1, width).astype(F32)
    mixed = (win / cnt - u).astype(BF16)
    y_ref[...] = _dot(mixed, w_ref[...]) * sc_ref[...]
    carry[...] = full[tb:tb + POOL_HALO]


def _pool(u, prefix, w_pool, scale, pos0):
    b, t, c = u.shape
    tb = 512 if t % 512 == 0 else t
    pre = jnp.pad(prefix, ((0, 0), (POOL_HALO - prefix.shape[1], 0), (0, 0)))
    wbd = jax.scipy.linalg.block_diag(*[w_pool[i] for i in range(w_pool.shape[0])]).astype(BF16)
    return pl.pallas_call(
        functools.partial(_pool_body, tb=tb, pos0=pos0), grid=(b, t // tb),
        in_specs=[pl.BlockSpec((None, tb, c), lambda i, j: (i, j, 0)),
                  pl.BlockSpec((None, POOL_HALO, c), lambda i, j: (i, 0, 0)),
                  _const_spec((c, c)), _const_spec((1, c))],
        out_specs=pl.BlockSpec((None, tb, c), lambda i, j: (i, j, 0)),
        out_shape=jax.ShapeDtypeStruct((b, t, c), F32),
        scratch_shapes=[pltpu.VMEM((POOL_HALO, c), F32), pltpu.VMEM((tb + POOL_HALO, c), F32)],
        compiler_params=_params(("parallel", "arbitrary")), name="pool")(u, pre, wbd, scale.reshape(1, c))


def _log_sigmoid(x):
    return jnp.minimum(x, 0.0) - jnp.log(1.0 + jnp.exp(-jnp.abs(x)))


def _mlstm_body(q_ref, k_ref, v_ref, vt_ref, gc_ref, gr_ref, c0_ref, n0_ref, m0_ref,
                h_ref, c_ref, n_ref, m_ref, *, L, t_valid):
    hd = pl.program_id(1)

    @pl.when(pl.program_id(2) == 0)
    def _():
        c_ref[...] = c0_ref[...]
        n_ref[...] = n0_ref[...]
        m_ref[...] = m0_ref[...]

    gc = gc_ref[...]
    gr = gr_ref[...]
    lane8 = lax.broadcasted_iota(jnp.int32, gc.shape, 1)
    sub8 = lax.broadcasted_iota(jnp.int32, gr.shape, 0)
    lf_c = _log_sigmoid(gc)
    lf_r = _log_sigmoid(gr)
    i_c = jnp.sum(jnp.where(lane8 == hd, gc, 0.0), axis=1, keepdims=True)
    i_r = jnp.sum(jnp.where(sub8 == hd, gr, 0.0), axis=0, keepdims=True)
    row = lax.broadcasted_iota(jnp.int32, (L, L), 0)
    col = lax.broadcasted_iota(jnp.int32, (L, L), 1)
    if t_valid < L:
        tok_c = lax.broadcasted_iota(jnp.int32, (L, 1), 0) < t_valid
        tok_r = lax.broadcasted_iota(jnp.int32, (1, L), 1) < t_valid
        lf_c = jnp.where(tok_c, lf_c, 0.0)
        lf_r = jnp.where(tok_r, lf_r, 0.0)
        i_c = jnp.where(tok_c, i_c, -jnp.inf)
        i_r = jnp.where(tok_r, i_r, -jnp.inf)
    causal = row >= col
    tri = causal.astype(F32)
    cs_c = jnp.dot(tri, lf_c, precision=HIGHEST, preferred_element_type=F32)
    cs_r = lax.dot_general(lf_r, tri, (((1,), (1,)), ((), ())), precision=HIGHEST,
                           preferred_element_type=F32)
    b_c = jnp.sum(jnp.where(lane8 == hd + ML_HEADS, cs_c, 0.0), axis=1, keepdims=True)
    b_r = jnp.sum(jnp.where(sub8 == hd + ML_HEADS, cs_r, 0.0), axis=0, keepdims=True)
    m_prev = m_ref[...]
    dmat = jnp.where(causal, b_c - b_r + i_r, -jnp.inf)
    inter = b_c + m_prev
    mt = jnp.maximum(jnp.max(dmat, axis=1, keepdims=True), inter)
    w = jnp.exp(dmat - mt)
    a = jnp.exp(inter - mt)
    q = q_ref[...]
    kf = k_ref[...] * (ML_DH ** -0.5)
    qb = q.astype(BF16)
    kb = kf.astype(BF16)
    s = _dot_nt(qb, kb) * w
    c_old = c_ref[...]
    n_old = n_ref[...]
    num = _dot(s.astype(BF16), v_ref[...].astype(BF16)) + a * _dot_nt(qb, c_old.astype(BF16))
    den = jnp.sum(s, axis=1, keepdims=True) + a * jnp.sum(q * n_old, axis=1, keepdims=True)
    h_ref[...] = num / jnp.maximum(jnp.abs(den), jnp.exp(-mt))
    b_last = b_c[L - 1:L, :]
    ge_r = b_last - b_r + i_r
    ge_c = b_last - b_c + i_c
    m_new = jnp.maximum(b_last + m_prev, jnp.max(ge_r, axis=1, keepdims=True))
    wk_r = jnp.exp(ge_r - m_new)
    wk_c = jnp.exp(ge_c - m_new)
    decay = jnp.exp(b_last + m_prev - m_new)
    c_ref[...] = decay * c_old + _dot((vt_ref[...] * wk_r).astype(BF16), kb)
    n_ref[...] = decay * n_old + jnp.sum(kf * wk_c, axis=0, keepdims=True)
    m_ref[...] = m_new


def _mlstm(q, k, v, vt, gc, gr, c0, n0, m0, *, L, t_valid):
    h4, b, t, dh = q.shape
    nc = t // L
    tok = pl.BlockSpec((None, None, L, dh), lambda i, j, c: (j, i, c, 0))
    if vt.ndim == 4:
        vt_spec = pl.BlockSpec((None, None, dh, L), lambda i, j, c: (j, i, 0, c))
        gr_spec = pl.BlockSpec((None, 2 * h4, L), lambda i, j, c: (i, 0, c))
    else:
        vt_spec = pl.BlockSpec((None, dh, L), lambda i, j, c: (j, 0, i * nc + c))
        gr_spec = pl.BlockSpec((2 * h4, L), lambda i, j, c: (0, i * nc + c))
    st = lambda r, w: pl.BlockSpec((None, None, r, w), lambda i, j, c: (i, j, 0, 0))
    return pl.pallas_call(
        functools.partial(_mlstm_body, L=L, t_valid=t_valid), grid=(b, h4, nc),
        in_specs=[tok, tok, tok, vt_spec, pl.BlockSpec((None, L, 2 * h4), lambda i, j, c: (i, c, 0)), gr_spec,
                  st(dh, dh), st(1, dh), st(1, 1)],
        out_specs=[tok, st(dh, dh), st(1, dh), st(1, 1)],
        out_shape=[jax.ShapeDtypeStruct((h4, b, t, dh), F32), jax.ShapeDtypeStruct((b, h4, dh, dh), F32),
                   jax.ShapeDtypeStruct((b, h4, 1, dh), F32), jax.ShapeDtypeStruct((b, h4, 1, 1), F32)],
        compiler_params=_params(("parallel", "parallel", "arbitrary")), name="mlstm")(
            q, k, v, vt, gc, gr, c0, n0.reshape(b, h4, 1, dh), m0.reshape(b, h4, 1, 1))


def _even_out_body(x_ref, yp_ref, og_ref, hm_ref, w0_ref, w1_ref, o_ref):
    acc = _dot(yp_ref[...].astype(BF16), w0_ref[...])
    for h in range(ML_HEADS):
        acc = acc + _dot((jax.nn.sigmoid(og_ref[h]) * hm_ref[h]).astype(BF16), w1_ref[h])
    o_ref[...] = x_ref[...] + acc


def _even_out(x, ypool, og, hm, w_out):
    n, d = x.shape
    tm = 512 if n % 512 == 0 else n
    w0 = w_out[:POOL_WIDTH].astype(BF16)
    w1 = w_out[POOL_WIDTH:].reshape(ML_HEADS, ML_DH, d).astype(BF16)
    row = lambda w: pl.BlockSpec((tm, w), lambda i: (i, 0))
    hrow = pl.BlockSpec((ML_HEADS, tm, ML_DH), lambda i: (0, i, 0))
    return pl.pallas_call(
        _even_out_body, grid=(n // tm,),
        in_specs=[row(d), row(POOL_WIDTH), hrow, hrow, _const_spec(w0.shape), _const_spec(w1.shape)],
        out_specs=row(d), out_shape=jax.ShapeDtypeStruct((n, d), F32),
        compiler_params=_params(("parallel",)), name="even_out")(x, ypool, og, hm, w0, w1)


def _odd_in_body(x_ref, g_ref, wq_ref, wkv_ref, wgt_ref, wc_ref,
                 q_ref, rows_ref, win_ref, gates_ref, ucv_ref, bg_ref):
    hn = _rms(x_ref[...], g_ref[...]).astype(BF16)
    for h in range(NSA_HEADS):
        q_ref[h] = _dot(hn, wq_ref[h]) * (HEAD_DIM ** -0.5)
    nrow = rows_ref.shape[1]
    rows_ref[...] = _dot(hn, wkv_ref[:, :nrow])
    win_ref[...] = _dot(hn, wkv_ref[:, nrow:])
    gates_ref[...] = jax.nn.sigmoid(_dot(hn, wgt_ref[...]))
    bg_ref[...] = _dot(hn, wc_ref[:, :CONV_CH])
    ucv_ref[...] = _dot(hn, wc_ref[:, CONV_CH:2 * CONV_CH]) * _dot(hn, wc_ref[:, 2 * CONV_CH:])


def _odd_in(x, g, w_in):
    n, d = x.shape
    tm = 512 if n % 512 == 0 else n
    kvw = 6 * KV_HEADS * HEAD_DIM
    ngt = 3 * NSA_HEADS
    wq = w_in[:, :NSA_WIDTH].reshape(d, NSA_HEADS, HEAD_DIM).transpose(1, 0, 2).astype(BF16)
    wkv = w_in[:, NSA_WIDTH:NSA_WIDTH + kvw].astype(BF16)
    wgt = w_in[:, NSA_WIDTH + kvw:NSA_WIDTH + kvw + ngt].astype(BF16)
    wc = w_in[:, NSA_WIDTH + kvw + ngt:].astype(BF16)
    nrow = 4 * KV_HEADS * HEAD_DIM
    row = lambda w: pl.BlockSpec((tm, w), lambda i: (i, 0))
    sh = lambda w: jax.ShapeDtypeStruct((n, w), F32)
    return pl.pallas_call(
        _odd_in_body, grid=(n // tm,),
        in_specs=[row(d), _const_spec((1, d)), _const_spec(wq.shape), _const_spec(wkv.shape),
                  _const_spec(wgt.shape), _const_spec(wc.shape)],
        out_specs=[pl.BlockSpec((NSA_HEADS, tm, HEAD_DIM), lambda i: (0, i, 0)),
                   row(nrow), row(kvw - nrow), row(ngt), row(CONV_CH), row(CONV_CH)],
        out_shape=[jax.ShapeDtypeStruct((NSA_HEADS, n, HEAD_DIM), F32),
                   sh(nrow), sh(kvw - nrow), sh(ngt), sh(CONV_CH), sh(CONV_CH)],
        compiler_params=_params(("parallel",)), name="odd_in")(x, g.reshape(1, d), wq, wkv, wgt, wc)


def _conv_body(u_ref, bg_ref, pre_ref, w_ref, y_ref, carry, full, *, tb):
    @pl.when(pl.program_id(1) == 0)
    def _():
        carry[...] = pre_ref[...]

    full[0:CONV_HALO] = carry[...]
    full[CONV_HALO:] = u_ref[...]
    f = full[...]
    w = w_ref[...]
    conv = f * w[2:3] + pltpu.roll(f, 1, 0) * w[1:2] + pltpu.roll(f, 2, 0) * w[0:1]
    y_ref[...] = bg_ref[...] * conv[CONV_HALO:]
    carry[...] = full[tb:tb + CONV_HALO]


def _conv(u, bg, prefix, conv_w):
    b, t, c = u.shape
    tb = 512 if t % 512 == 0 else t
    pre = jnp.pad(prefix, ((0, 0), (CONV_HALO - prefix.shape[1], 0), (0, 0)))
    blk = pl.BlockSpec((None, tb, c), lambda i, j: (i, j, 0))
    return pl.pallas_call(
        functools.partial(_conv_body, tb=tb), grid=(b, t // tb),
        in_specs=[blk, blk, pl.BlockSpec((None, CONV_HALO, c), lambda i, j: (i, 0, 0)), _const_spec((CONV_K, c))],
        out_specs=blk, out_shape=jax.ShapeDtypeStruct((b, t, c), F32),
        scratch_shapes=[pltpu.VMEM((CONV_HALO, c), F32), pltpu.VMEM((tb + CONV_HALO, c), F32)],
        compiler_params=_params(("parallel", "arbitrary")), name="conv")(u, bg, pre, conv_w)


def _odd_out_body(x_ref, o_ref, yc_ref, w0_ref, w1_ref, out_ref):
    out_ref[...] = (x_ref[...] + _dot(o_ref[...].astype(BF16), w0_ref[...])
                    + _dot(yc_ref[...].astype(BF16), w1_ref[...]))


def _odd_out(x, o, yconv, w_out):
    n, d = x.shape
    tm = 512 if n % 512 == 0 else n
    w0 = w_out[:NSA_WIDTH].astype(BF16)
    w1 = w_out[NSA_WIDTH:].astype(BF16)
    row = lambda w: pl.BlockSpec((tm, w), lambda i: (i, 0))
    return pl.pallas_call(
        _odd_out_body, grid=(n // tm,),
        in_specs=[row(d), row(NSA_WIDTH), row(CONV_CH), _const_spec(w0.shape), _const_spec(w1.shape)],
        out_specs=row(d), out_shape=jax.ShapeDtypeStruct((n, d), F32),
        compiler_params=_params(("parallel",)), name="odd_out")(x, o, yconv, w0, w1)


def _cmp_weights(cmp_pos_w, cmp_w, cmp_b):
    eye = jnp.eye(PAGE_SIZE // CMP_STRIDE, dtype=F32)
    mats = []
    for kv in range(2):
        ea = jnp.kron(eye, cmp_pos_w[kv, :CMP_STRIDE][None, :])
        eb = jnp.kron(eye, cmp_pos_w[kv, CMP_STRIDE:][None, :])
        top = eb + jnp.concatenate([jnp.zeros_like(ea[:1]), ea[:-1]], axis=0)
        mats.append(jnp.concatenate([top, ea[-1:], jnp.zeros((7, PAGE_SIZE), F32)], axis=0))
    wmix = jnp.stack(mats).astype(BF16)
    wbd = jax.scipy.linalg.block_diag(cmp_w[0], cmp_w[0], cmp_w[1], cmp_w[1]).astype(BF16)
    bias = jnp.concatenate([cmp_b[0], cmp_b[0], cmp_b[1], cmp_b[1]]).reshape(1, -1)
    return wmix, wbd, bias


def _compress_body(*refs, n_pages, n_prefetch):
    refs = refs[n_prefetch:]
    pages = refs[:n_pages]
    wmix_ref, wbd_ref, bias_ref, out_ref, carry = refs[n_pages:]

    @pl.when(pl.program_id(1) == 0)
    def _():
        carry[...] = jnp.zeros(carry.shape, F32)

    half = 2 * HEAD_DIM
    first = lax.broadcasted_iota(jnp.int32, (8, 2 * half), 0) == 0
    for p in range(n_pages):
        rows = pages[p][...].astype(BF16)
        rk = _dot(wmix_ref[0], rows[:, :half])
        rv = _dot(wmix_ref[1], rows[:, half:])
        pre = jnp.concatenate([rk[0:8], rv[0:8]], axis=1) + jnp.where(first, carry[...], 0.0)
        carry[...] = jnp.broadcast_to(jnp.concatenate([rk[8:9], rv[8:9]], axis=1), carry.shape)
        out_ref[8 * p:8 * p + 8, :] = _dot(pre.astype(BF16), wbd_ref[...]) + bias_ref[...]


def _compress_prompt(rows, wmix, wbd, bias):
    b, t, _ = rows.shape
    pg = 4
    ncol = 4 * HEAD_DIM
    specs = [pl.BlockSpec((None, PAGE_SIZE, ncol), functools.partial(lambda i, j, p: (i, j * pg + p, 0), p=p))
             for p in range(pg)]
    return pl.pallas_call(
        functools.partial(_compress_body, n_pages=pg, n_prefetch=0), grid=(b, t // (PAGE_SIZE * pg)),
        in_specs=specs + [_const_spec(wmix.shape), _const_spec(wbd.shape), _const_spec(bias.shape)],
        out_specs=pl.BlockSpec((None, 8 * pg, ncol), lambda i, j: (i, j, 0)),
        out_shape=jax.ShapeDtypeStruct((b, t // CMP_STRIDE, ncol), F32),
        scratch_shapes=[pltpu.VMEM((8, ncol), F32)],
        compiler_params=_params(("parallel", "arbitrary")), name="compress_prompt")(*([rows] * pg), wmix, wbd, bias)


def _compress_paged(cache, page_table, wmix, wbd, bias):
    b, npg = page_table.shape
    pg = 8
    ncol = 4 * HEAD_DIM
    specs = [pl.BlockSpec((None, PAGE_SIZE, ncol),
                          functools.partial(lambda i, j, pt, p: (pt[i, j * pg + p], 0, 0), p=p))
             for p in range(pg)]
    return pl.pallas_call(
        functools.partial(_compress_body, n_pages=pg, n_prefetch=1),
        grid_spec=pltpu.PrefetchScalarGridSpec(
            num_scalar_prefetch=1, grid=(b, npg // pg),
            in_specs=specs + [_const_spec(wmix.shape), _const_spec(wbd.shape), _const_spec(bias.shape)],
            out_specs=pl.BlockSpec((None, 8 * pg, ncol), lambda i, j, pt: (i, j, 0)),
            scratch_shapes=[pltpu.VMEM((8, ncol), F32)]),
        out_shape=jax.ShapeDtypeStruct((b, npg * 8, ncol), F32),
        compiler_params=_params(("parallel", "arbitrary")), name="compress_paged")(
            page_table, *([cache] * pg), wmix, wbd, bias)


def _alibi_slopes(rows_per_head):
    sl = (2.0 ** (-8.0 * np.arange(1, NSA_HEADS + 1) / NSA_HEADS)).astype(np.float32).reshape(KV_HEADS, GQ)
    return jnp.asarray(np.repeat(sl, rows_per_head, axis=1)[:, :, None])


def _score_matrix(n_cmp_rows, n_blocks, lanes):
    m = np.arange(n_cmp_rows)[:, None]
    j = np.arange(lanes)[None, :]
    return jnp.asarray(((m >= 4 * j) & (m <= 4 * j + 4) & (m >= 1) & (j < n_blocks)).astype(np.float32))


def _softmax_rows(s, mask):
    s = jnp.where(mask, s, -jnp.inf)
    m = jnp.max(s, axis=-1, keepdims=True)
    m = jnp.where(m > -jnp.inf, m, 0.0)
    e = jnp.exp(s - m)
    return e / jnp.maximum(jnp.sum(e, axis=-1, keepdims=True), 1e-30)


def _select_blocks(score, cur, n_blocks):
    r, lanes = score.shape
    blk = lax.broadcasted_iota(jnp.int32, (r, lanes), 1)
    forced = (blk == 0) | (blk == cur) | (blk == cur - 1)
    val = jnp.where(forced, jnp.inf, jnp.where(blk <= cur, score, -jnp.inf))
    rank = jnp.zeros((r, lanes), jnp.int32)
    for i in range(n_blocks):
        ci = val[:, i:i + 1]
        ahead = (ci > val) | ((ci == val) & (blk > i))
        rank = rank + ahead.astype(jnp.int32)
    return ((rank < min(SEL_TOPN, n_blocks)) & (blk < n_blocks)).astype(F32)


def _expand_blocks(selmask, first_block, n_keys, reps):
    lanes = selmask.shape[1]
    b = lax.broadcasted_iota(jnp.int32, (lanes, n_keys), 0)
    k = lax.broadcasted_iota(jnp.int32, (lanes, n_keys), 1)
    onehot = jnp.where(b == first_block + k // SEL_BLOCK, 1.0, 0.0).astype(BF16)
    m = _dot(selmask.astype(BF16), onehot)
    return jnp.concatenate([m] * reps, axis=0) > 0.5


def _nsa_prompt_body(q_ref, gates_ref, kvc_ref, sel_ref, win_ref, slope_ref, smat_ref, o_ref, *, t_len, kc):
    bi = pl.program_id(1)
    p0 = bi * Q_BLOCK
    nrow = GQ * Q_BLOCK
    n_cmp = kvc_ref.shape[0]
    n_blocks = t_len // SEL_BLOCK
    qpos = lax.broadcasted_iota(jnp.int32, (Q_BLOCK, 1), 0) + p0
    tpos = jnp.concatenate([qpos] * GQ, axis=0)
    gates = gates_ref[...]
    wb = min(WINDOW + Q_BLOCK, t_len)
    wstart = pl.multiple_of(jnp.maximum(p0 + Q_BLOCK - wb, 0), Q_BLOCK)
    for g in range(KV_HEADS):
        qg = q_ref[g * GQ:(g + 1) * GQ].reshape(nrow, HEAD_DIM).astype(BF16)
        slope = slope_ref[g]
        ks, vs = g * HEAD_DIM, (KV_HEADS + g) * HEAD_DIM

        cpos = CMP_STRIDE * lax.broadcasted_iota(jnp.int32, (1, n_cmp), 1) + (CMP_STRIDE - 1)
        dist = tpos - cpos
        s = _dot_nt(qg, kvc_ref[:, ks:ks + HEAD_DIM].astype(BF16)) - slope * dist.astype(F32)
        p_cmp = _softmax_rows(s, (dist >= 0) & (cpos >= 2 * CMP_STRIDE - 1))
        o_cmp = _dot(p_cmp.astype(BF16), kvc_ref[:, vs:vs + HEAD_DIM].astype(BF16))
        imp = jnp.sum(p_cmp.reshape(GQ, Q_BLOCK, n_cmp), axis=0)
        score = jnp.dot(imp, smat_ref[...], precision=HIGHEST, preferred_element_type=F32)
        selmask = _select_blocks(score, jnp.full((Q_BLOCK, 1), bi, jnp.int32), n_blocks)

        def sel_step(c, carry):
            m_i, l_i, acc = carry
            k0 = pl.multiple_of(c * kc, kc)
            kk = sel_ref[pl.ds(k0, kc), ks:ks + HEAD_DIM].astype(BF16)
            vv = sel_ref[pl.ds(k0, kc), vs:vs + HEAD_DIM].astype(BF16)
            kpos = k0 + lax.broadcasted_iota(jnp.int32, (1, kc), 1)
            d = tpos - kpos
            sc = _dot_nt(qg, kk) - slope * d.astype(F32)
            ok = _expand_blocks(selmask, c * (kc // SEL_BLOCK), kc, GQ) & (d >= 0)
            sc = jnp.where(ok, sc, NEG)
            m_new = jnp.maximum(m_i, jnp.max(sc, axis=1, keepdims=True))
            pr = jnp.where(ok, jnp.exp(sc - m_new), 0.0)
            alpha = jnp.exp(m_i - m_new)
            return (m_new, alpha * l_i + jnp.sum(pr, axis=1, keepdims=True),
                    alpha * acc + _dot(pr.astype(BF16), vv))

        init = (jnp.full((nrow, 1), NEG, F32), jnp.zeros((nrow, 1), F32), jnp.zeros((nrow, HEAD_DIM), F32))
        _, l_s, acc_s = lax.fori_loop(0, (p0 + Q_BLOCK + kc - 1) // kc, sel_step, init)
        o_sel = acc_s / jnp.maximum(l_s, 1e-30)

        kw = win_ref[pl.ds(wstart, wb), ks:ks + HEAD_DIM].astype(BF16)
        vw = win_ref[pl.ds(wstart, wb), vs:vs + HEAD_DIM].astype(BF16)
        wpos = wstart + lax.broadcasted_iota(jnp.int32, (1, wb), 1)
        wd = tpos - wpos
        sw = _dot_nt(qg, kw) - slope * wd.astype(F32)
        p_win = _softmax_rows(sw, (wd >= 0) & (wd < WINDOW))
        o_win = _dot(p_win.astype(BF16), vw)

        for i in range(GQ):
            hh = g * GQ + i
            r = slice(i * Q_BLOCK, (i + 1) * Q_BLOCK)
            o_ref[:, hh * HEAD_DIM:(hh + 1) * HEAD_DIM] = (
                gates[:, 3 * hh:3 * hh + 1] * o_cmp[r] + gates[:, 3 * hh + 1:3 * hh + 2] * o_sel[r]
                + gates[:, 3 * hh + 2:3 * hh + 3] * o_win[r])


def _nsa_prompt(q, gates, kvc, rows, win):
    _, b, t, _ = q.shape
    assert t % Q_BLOCK == 0
    kc = 512 if t % 512 == 0 else t
    n_cmp = kvc.shape[1]
    ncol = 4 * HEAD_DIM
    slopes = _alibi_slopes(Q_BLOCK)
    smat = _score_matrix(n_cmp, t // SEL_BLOCK, t // SEL_BLOCK)
    return pl.pallas_call(
        functools.partial(_nsa_prompt_body, t_len=t, kc=kc), grid=(b, t // Q_BLOCK),
        in_specs=[pl.BlockSpec((NSA_HEADS, None, Q_BLOCK, HEAD_DIM), lambda i, j: (0, i, j, 0)),
                  pl.BlockSpec((None, Q_BLOCK, 3 * NSA_HEADS), lambda i, j: (i, j, 0)),
                  pl.BlockSpec((None, n_cmp, ncol), lambda i, j: (i, 0, 0)),
                  pl.BlockSpec((None, t, ncol), lambda i, j: (i, 0, 1)),
                  pl.BlockSpec((None, t, ncol), lambda i, j: (i, 0, 0)),
                  _const_spec(slopes.shape), _const_spec(smat.shape)],
        out_specs=pl.BlockSpec((None, Q_BLOCK, NSA_WIDTH), lambda i, j: (i, j, 0)),
        out_shape=jax.ShapeDtypeStruct((b, t, NSA_WIDTH), F32),
        compiler_params=_params(("parallel", "arbitrary")), name="nsa_prompt")(
            q, gates, kvc, rows, win, slopes, smat)


def _nsa_dec_select_body(q_ref, kvc_ref, slope_ref, smat_ref, ocmp_ref, sel_ref, *, past_len, tq, n_blocks):
    n_cmp = kvc_ref.shape[0]
    nrow = GQ * tq
    qpos = lax.broadcasted_iota(jnp.int32, (tq, 1), 0) + past_len
    tpos = jnp.concatenate([qpos] * GQ, axis=0)
    cpos = CMP_STRIDE * lax.broadcasted_iota(jnp.int32, (1, n_cmp), 1) + (CMP_STRIDE - 1)
    dist = tpos - cpos
    for g in range(KV_HEADS):
        qg = q_ref[g * GQ:(g + 1) * GQ].reshape(nrow, HEAD_DIM).astype(BF16)
        ks, vs = g * HEAD_DIM, (KV_HEADS + g) * HEAD_DIM
        s = _dot_nt(qg, kvc_ref[:, ks:ks + HEAD_DIM].astype(BF16)) - slope_ref[g] * dist.astype(F32)
        p_cmp = _softmax_rows(s, (dist >= 0) & (cpos >= 2 * CMP_STRIDE - 1))
        ocmp_ref[g] = _dot(p_cmp.astype(BF16), kvc_ref[:, vs:vs + HEAD_DIM].astype(BF16))
        imp = jnp.sum(p_cmp.reshape(GQ, tq, n_cmp), axis=0)
        score = jnp.dot(imp, smat_ref[...], precision=HIGHEST, preferred_element_type=F32)
        sel_ref[g] = _select_blocks(score, qpos // SEL_BLOCK, n_blocks)


def _nsa_dec_select(q, kvc, past_len, n_blocks):
    _, b, tq, _ = q.shape
    n_cmp = kvc.shape[1]
    lanes = -(-n_blocks // 128) * 128
    slopes = _alibi_slopes(tq)
    smat = _score_matrix(n_cmp, n_blocks, lanes)
    return pl.pallas_call(
        functools.partial(_nsa_dec_select_body, past_len=past_len, tq=tq, n_blocks=n_blocks), grid=(b,),
        in_specs=[pl.BlockSpec((NSA_HEADS, None, tq, HEAD_DIM), lambda i: (0, i, 0, 0)),
                  pl.BlockSpec((None, n_cmp, 4 * HEAD_DIM), lambda i: (i, 0, 0)),
                  _const_spec(slopes.shape), _const_spec(smat.shape)],
        out_specs=[pl.BlockSpec((None, KV_HEADS, GQ * tq, HEAD_DIM), lambda i: (i, 0, 0, 0)),
                   pl.BlockSpec((None, KV_HEADS, tq, lanes), lambda i: (i, 0, 0, 0))],
        out_shape=[jax.ShapeDtypeStruct((b, KV_HEADS, GQ * tq, HEAD_DIM), F32),
                   jax.ShapeDtypeStruct((b, KV_HEADS, tq, lanes), F32)],
        compiler_params=_params(("parallel",)), name="nsa_dec_select")(q, kvc, slopes, smat)


def _nsa_dec_attend_body(pt_ref, q_ref, selm_ref, ocmp_ref, gates_ref, newsel_ref, winbuf_ref, newwin_ref,
                         slope_ref, *rest, past_len, tq, t_valid, n_pages):
    pages = rest[:n_pages]
    o_ref, m_sc, l_sc, acc_sc = rest[n_pages:]
    step = pl.program_id(1)
    nrow = GQ * tq
    qpos = lax.broadcasted_iota(jnp.int32, (tq, 1), 0) + past_len
    tpos = jnp.concatenate([qpos] * GQ, axis=0)

    @pl.when(step == 0)
    def _():
        m_sc[...] = jnp.full(m_sc.shape, NEG, F32)
        l_sc[...] = jnp.zeros(l_sc.shape, F32)
        acc_sc[...] = jnp.zeros(acc_sc.shape, F32)

    def update(g, sc, ok, vals):
        sc = jnp.where(ok, sc, NEG)
        m_i = m_sc[g]
        m_new = jnp.maximum(m_i, jnp.max(sc, axis=1, keepdims=True))
        pr = jnp.where(ok, jnp.exp(sc - m_new), 0.0)
        alpha = jnp.exp(m_i - m_new)
        l_sc[g] = alpha * l_sc[g] + jnp.sum(pr, axis=1, keepdims=True)
        acc_sc[g] = alpha * acc_sc[g] + _dot(pr.astype(BF16), vals)
        m_sc[g] = m_new

    nk = n_pages * PAGE_SIZE
    k0 = step * nk
    kpos = k0 + lax.broadcasted_iota(jnp.int32, (1, nk), 1)
    for g in range(KV_HEADS):
        qg = q_ref[g * GQ:(g + 1) * GQ].reshape(nrow, HEAD_DIM).astype(BF16)
        ks, vs = g * HEAD_DIM, (KV_HEADS + g) * HEAD_DIM
        kk = jnp.concatenate([pg[:, ks:ks + HEAD_DIM] for pg in pages], axis=0).astype(BF16)
        vv = jnp.concatenate([pg[:, vs:vs + HEAD_DIM] for pg in pages], axis=0).astype(BF16)
        d = tpos - kpos
        sc = _dot_nt(qg, kk) - slope_ref[g] * d.astype(F32)
        ok = _expand_blocks(selm_ref[g], step * (nk // SEL_BLOCK), nk, GQ) & (d >= 0)
        update(g, sc, ok, vv)

    @pl.when(step == pl.num_programs(1) - 1)
    def _():
        gates = gates_ref[...]
        tn = newsel_ref.shape[0]
        npos = past_len + lax.broadcasted_iota(jnp.int32, (1, tn), 1)
        nd = tpos - npos
        valid_new = (nd >= 0) & (npos < past_len + t_valid)
        wlen = winbuf_ref.shape[0]
        wpos = past_len - wlen + lax.broadcasted_iota(jnp.int32, (1, wlen), 1)
        wd = tpos - wpos
        for g in range(KV_HEADS):
            qg = q_ref[g * GQ:(g + 1) * GQ].reshape(nrow, HEAD_DIM).astype(BF16)
            ks, vs = g * HEAD_DIM, (KV_HEADS + g) * HEAD_DIM
            slope = slope_ref[g]
            sc = _dot_nt(qg, newsel_ref[:, ks:ks + HEAD_DIM].astype(BF16)) - slope * nd.astype(F32)
            last_blk = past_len // SEL_BLOCK
            picked = selm_ref[g][:, last_blk:last_blk + 1] > 0.5
            ok = jnp.concatenate([picked] * GQ, axis=0) & valid_new
            update(g, sc, ok, newsel_ref[:, vs:vs + HEAD_DIM].astype(BF16))
            o_sel = acc_sc[g] / jnp.maximum(l_sc[g], 1e-30)
            s1 = _dot_nt(qg, winbuf_ref[:, ks:ks + HEAD_DIM].astype(BF16)) - slope * wd.astype(F32)
            s2 = _dot_nt(qg, newwin_ref[:, ks:ks + HEAD_DIM].astype(BF16)) - slope * nd.astype(F32)
            ok1 = (wd >= 0) & (wd < WINDOW) & (wpos >= 0)
            ok2 = valid_new & (nd < WINDOW)
            s1 = jnp.where(ok1, s1, -jnp.inf)
            s2 = jnp.where(ok2, s2, -jnp.inf)
            mx = jnp.maximum(jnp.max(s1, axis=1, keepdims=True), jnp.max(s2, axis=1, keepdims=True))
            mx = jnp.where(mx > -jnp.inf, mx, 0.0)
            e1 = jnp.exp(s1 - mx)
            e2 = jnp.exp(s2 - mx)
            den = jnp.maximum(jnp.sum(e1, axis=1, keepdims=True) + jnp.sum(e2, axis=1, keepdims=True), 1e-30)
            o_win = (_dot(e1.astype(BF16), winbuf_ref[:, vs:vs + HEAD_DIM].astype(BF16))
                     + _dot(e2.astype(BF16), newwin_ref[:, vs:vs + HEAD_DIM].astype(BF16))) / den
            o_cmp = ocmp_ref[g]
            for i in range(GQ):
                hh = g * GQ + i
                r = slice(i * tq, (i + 1) * tq)
                o_ref[:, hh * HEAD_DIM:(hh + 1) * HEAD_DIM] = (
                    gates[:, 3 * hh:3 * hh + 1] * o_cmp[r] + gates[:, 3 * hh + 1:3 * hh + 2] * o_sel[r]
                    + gates[:, 3 * hh + 2:3 * hh + 3] * o_win[r])


def _nsa_dec_attend(q, selmask, o_cmp, gates, newsel, winbuf, newwin, cache, page_table, past_len, t_valid):
    _, b, tq, _ = q.shape
    npg = page_table.shape[1]
    pg = 8
    ncol = 4 * HEAD_DIM
    lanes = selmask.shape[-1]
    slopes = _alibi_slopes(tq)
    wlen = winbuf.shape[1]
    bspec = lambda shape: pl.BlockSpec((None,) + shape, lambda i, j, pt: (i,) + (0,) * len(shape))
    page_specs = [pl.BlockSpec((None, PAGE_SIZE, ncol),
                               functools.partial(lambda i, j, pt, p: (pt[i, j * pg + p], 0, 1), p=p))
                  for p in range(pg)]
    return pl.pallas_call(
        functools.partial(_nsa_dec_attend_body, past_len=past_len, tq=tq, t_valid=t_valid, n_pages=pg),
        grid_spec=pltpu.PrefetchScalarGridSpec(
            num_scalar_prefetch=1, grid=(b, npg // pg),
            in_specs=[pl.BlockSpec((NSA_HEADS, None, tq, HEAD_DIM), lambda i, j, pt: (0, i, 0, 0)),
                      bspec((KV_HEADS, tq, lanes)), bspec((KV_HEADS, GQ * tq, HEAD_DIM)),
                      bspec((tq, 3 * NSA_HEADS)), bspec((tq, ncol)), bspec((wlen, ncol)), bspec((tq, ncol)),
                      pl.BlockSpec(slopes.shape, lambda i, j, pt: (0, 0, 0))] + page_specs,
            out_specs=bspec((tq, NSA_WIDTH)),
            scratch_shapes=[pltpu.VMEM((KV_HEADS, GQ * tq, 1), F32), pltpu.VMEM((KV_HEADS, GQ * tq, 1), F32),
                            pltpu.VMEM((KV_HEADS, GQ * tq, HEAD_DIM), F32)]),
        out_shape=jax.ShapeDtypeStruct((b, tq, NSA_WIDTH), F32),
        compiler_params=_params(("parallel", "arbitrary")), name="nsa_dec_attend")(
            page_table, q, selmask, o_cmp, gates, newsel, winbuf, newwin, slopes, *([cache] * pg))


def _pad_t(a, axis, to):
    pad = [(0, 0)] * a.ndim
    pad[axis] = (0, to - a.shape[axis])
    return jnp.pad(a, pad)


def _even_layer(x, b, t, pos0, pool_buf, c0, n0, m0, g, w_in, b_gate, w_pool, pool_scale, w_out):
    n = b * t
    u, q, k, v, og, vt, gc, gr = _even_in(x, g, w_in, b_gate)
    u3 = u.reshape(b, t, POOL_WIDTH)
    new_pool = jnp.concatenate([pool_buf, u3], axis=1)[:, -pool_buf.shape[1]:]
    h4 = ML_HEADS
    if t % 256 == 0:
        tp, L = t, 256
        vt_in, gr_in = vt, gr
    else:
        tp = L = -(-t // 8) * 8
        vt_in = _pad_t(vt.reshape(h4, ML_DH, b, t), 3, tp).transpose(0, 2, 1, 3)
        gr_in = _pad_t(gr.reshape(2 * h4, b, t), 2, tp).transpose(1, 0, 2)
    seq = lambda a: _pad_t(a.reshape(h4, b, t, ML_DH), 2, tp)
    ypool = _pool(_pad_t(u3, 1, tp), pool_buf, w_pool, pool_scale, pos0)[:, :t].reshape(n, POOL_WIDTH)
    hm, c, nn, m = _mlstm(seq(q), seq(k), seq(v), vt_in, _pad_t(gc.reshape(b, t, 2 * h4), 1, tp), gr_in,
                          c0, n0, m0, L=L, t_valid=t)
    hm = hm[:, :, :t].reshape(h4, n, ML_DH)
    x = _even_out(x, ypool, og, hm, w_out)
    return x, new_pool, c, nn.reshape(b, h4, ML_DH), m.reshape(b, h4)


def _odd_layer(x, b, t, pos0, cache, page_table, win_buf, conv_buf, g, w_in, cmp_pos_w, cmp_w, cmp_b,
               conv_w, w_out):
    n = b * t
    q, rows, win, gates, ucv, bg = _odd_in(x, g, w_in)
    wmix, wbd, bias = _cmp_weights(cmp_pos_w, cmp_w, cmp_b)
    ncol = 4 * HEAD_DIM
    rows3 = rows.reshape(b, t, 2 * ncol)
    win3 = win.reshape(b, t, ncol)
    u3 = ucv.reshape(b, t, CONV_CH)
    new_conv = jnp.concatenate([conv_buf, u3], axis=1)[:, -(CONV_K - 1):]
    if cache is None:
        kvc = _compress_prompt(rows3, wmix, wbd, bias)
        o = _nsa_prompt(q.reshape(NSA_HEADS, b, t, HEAD_DIM), gates.reshape(b, t, 3 * NSA_HEADS), kvc, rows3, win3)
        new_win = win3[:, -min(WINDOW, t):]
        tp = t
    else:
        assert pos0 % SEL_BLOCK + t <= SEL_BLOCK
        tp = -(-t // 8) * 8
        n_pool = cache.shape[0]
        cache2 = cache.reshape(n_pool, PAGE_SIZE, 2 * ncol)
        kvc = _compress_paged(cache2, page_table, wmix, wbd, bias)
        qp = _pad_t(q.reshape(NSA_HEADS, b, t, HEAD_DIM), 2, tp)
        n_blocks = -(-(pos0 + t) // SEL_BLOCK)
        o_cmp, selmask = _nsa_dec_select(qp, kvc, pos0, n_blocks)
        wlen = win_buf.shape[1]
        winb = win_buf.reshape(b, wlen, ncol)
        o = _nsa_dec_attend(qp, selmask, o_cmp, _pad_t(gates.reshape(b, t, -1), 1, tp),
                            _pad_t(rows3[:, :, ncol:], 1, tp), winb, _pad_t(win3, 1, tp),
                            cache2, page_table, pos0, t)[:, :t]
        new_win = jnp.concatenate([winb, win3], axis=1)[:, -wlen:]
    yconv = _conv(_pad_t(u3, 1, tp), _pad_t(bg.reshape(b, t, CONV_CH), 1, tp), conv_buf, conv_w)[:, :t]
    x = _odd_out(x, o.reshape(n, NSA_WIDTH), yconv.reshape(n, CONV_CH), w_out)
    new_rows = rows3.reshape(b, t, 4, KV_HEADS, HEAD_DIM)
    new_win = new_win.reshape(b, new_win.shape[1], 2, KV_HEADS, HEAD_DIM)
    return x, new_rows, new_win, new_conv


def _trunk(x3, pos0, pool_buf, ml_c, ml_n, ml_m, kv_cache, page_table, win_buf, conv_buf, p):
    b, t, d = x3.shape
    depth = p["norm_g"].shape[0]
    x = x3.reshape(b * t, d)
    pools, cs, ns, ms, rows, wins, convs = [], [], [], [], [], [], []
    for l in range(depth):
        j = l // 2
        x = _ffn(x, p["norm_g"][l, 0], p["w_ffn_in"][l][0], p["w_ffn_out"][l][0])
        if l % 2 == 0:
            x, pb, c, n, m = _even_layer(x, b, t, pos0, pool_buf[j], ml_c[j], ml_n[j], ml_m[j], p["norm_g"][l, 1],
                                         p["w_in_even"][j], p["b_gate_even"][j], p["w_pool"][j],
                                         p["pool_scale"][j], p["w_out_even"][j])
            pools.append(pb)
            cs.append(c)
            ns.append(n)
            ms.append(m)
        else:
            cache = None if kv_cache is None else kv_cache[j]
            wb = None if win_buf is None else win_buf[j]
            x, r, wn, cn = _odd_layer(x, b, t, pos0, cache, page_table, wb, conv_buf[j], p["norm_g"][l, 1],
                                      p["w_in_odd"][j], p["cmp_pos_w"][j], p["cmp_w"][j], p["cmp_b"][j],
                                      p["conv_w"][j], p["w_out_odd"][j])
            rows.append(r)
            wins.append(wn)
            convs.append(cn)
        x = _ffn(x, p["norm_g"][l, 2], p["w_ffn_in"][l][1], p["w_ffn_out"][l][1],
                 final_g=p["final_g"] if l == depth - 1 else None)
    states = (jnp.stack(pools), jnp.stack(cs), jnp.stack(ns), jnp.stack(ms),
              jnp.stack(rows), jnp.stack(wins), jnp.stack(convs))
    return x.reshape(b, t, d), states


def kernel(x_prompt, x_sample, state_pool, state_mlstm_c, state_mlstm_n, state_mlstm_m, cache_nsa_kv, state_win_kv, state_conv, page_table, norm_g, final_g, w_ffn_in, w_ffn_out, w_in_even, b_gate_even, w_pool, pool_scale, w_out_even, w_in_odd, cmp_pos_w, cmp_w, cmp_b, conv_w, w_out_odd):
    bp = x_prompt.shape[0]
    n_even, n_odd = state_pool.shape[0], state_conv.shape[0]
    past_len = page_table.shape[1] * PAGE_SIZE
    p = dict(norm_g=norm_g, final_g=final_g, w_ffn_in=w_ffn_in.astype(BF16), w_ffn_out=w_ffn_out.astype(BF16),
             w_in_even=w_in_even, b_gate_even=b_gate_even, w_pool=w_pool, pool_scale=pool_scale,
             w_out_even=w_out_even, w_in_odd=w_in_odd, cmp_pos_w=cmp_pos_w, cmp_w=cmp_w, cmp_b=cmp_b,
             conv_w=conv_w, w_out_odd=w_out_odd)
    pool0 = jnp.zeros((n_even, bp) + state_pool.shape[2:], F32)
    c0 = jnp.zeros((n_even, bp) + state_mlstm_c.shape[2:], F32)
    n0 = jnp.zeros((n_even, bp) + state_mlstm_n.shape[2:], F32)
    m0 = jnp.zeros((n_even, bp) + state_mlstm_m.shape[2:], F32)
    conv0 = jnp.zeros((n_odd, bp) + state_conv.shape[2:], F32)
    y_p, (pool_p, c_p, n_p, m_p, kv_p, win_p, conv_p) = _trunk(
        x_prompt, 0, pool0, c0, n0, m0, None, None, None, conv0, p)
    y_s, (pool_s, c_s, n_s, m_s, kv_s, win_s, conv_s) = _trunk(
        x_sample, past_len, state_pool, state_mlstm_c, state_mlstm_n, state_mlstm_m,
        cache_nsa_kv, page_table, state_win_kv, state_conv, p)
    return (y_p, y_s, pool_p, pool_s, c_p, c_s, n_p, n_s, m_p, m_s,
            kv_p, kv_s, win_p, win_s, conv_p, conv_s)
```

```python
import functools

import numpy as np
import jax
import jax.numpy as jnp
from jax import lax
from jax.experimental import pallas as pl
from jax.experimental.pallas import tpu as pltpu

F32 = jnp.float32
BF16 = jnp.bfloat16
HIGHEST = lax.Precision.HIGHEST

EPS = 1e-6
POOL_WINDOWS = (2, 4, 8, 16)
POOL_GDIM = 64
POOL_WIDTH = 256
POOL_HALO = 16
ML_HEADS = 4
ML_DH = 192
ML_WIDTH = ML_HEADS * ML_DH
NSA_HEADS = 12
HEAD_DIM = 64
NSA_WIDTH = NSA_HEADS * HEAD_DIM
KV_HEADS = 2
GQ = NSA_HEADS // KV_HEADS
CMP_STRIDE = 16
SEL_BLOCK = 64
SEL_TOPN = 16
WINDOW = 512
Q_BLOCK = 64
PAGE_SIZE = 128
CONV_CH = 256
CONV_K = 3
CONV_HALO = 8

VMEM_LIMIT = 56 * 1024 * 1024
NEG = -1e30


def _params(sem, vmem=VMEM_LIMIT):
    return pltpu.CompilerParams(dimension_semantics=sem, vmem_limit_bytes=vmem)


def _const_spec(shape):
    nd = len(shape)
    return pl.BlockSpec(shape, lambda *_: (0,) * nd, pipeline_mode=pl.Buffered(1))


def _rms(x, g):
    return x * lax.rsqrt(jnp.mean(x * x, axis=-1, keepdims=True) + EPS) * g


def _dot(a, b):
    return jnp.dot(a, b, preferred_element_type=F32)


def _dot_nt(a, b):
    return lax.dot_general(a, b, (((1,), (1,)), ((), ())), preferred_element_type=F32)


def _ffn_body(x_ref, g_ref, win_ref, wout_ref, *rest, d_ff, chunks, has_final):
    o_ref = rest[-1]
    x = x_ref[...]
    hn = _rms(x, g_ref[...]).astype(BF16)
    acc = jnp.zeros(x.shape, F32)
    off = 0
    for fc in chunks:
        a = _dot(hn, win_ref[:, off:off + fc])
        b = _dot(hn, win_ref[:, d_ff + off:d_ff + off + fc])
        act = (a * jax.nn.sigmoid(a) * b).astype(BF16)
        acc = acc + _dot(act, wout_ref[off:off + fc, :])
        off += fc
    y = x + 0.5 * acc
    if has_final:
        y = _rms(y, rest[0][...])
    o_ref[...] = y


def _ffn(x, g, w_in, w_out, final_g=None):
    n, d = x.shape
    d_ff = w_out.shape[0]
    tm = 512 if n % 512 == 0 else n
    chunks, left = [], d_ff
    while left:
        chunks.append(min(1024, left))
        left -= chunks[-1]
    row = pl.BlockSpec((tm, d), lambda i: (i, 0))
    in_specs = [row, _const_spec((1, d)), _const_spec(w_in.shape), _const_spec(w_out.shape)]
    args = [x, g.reshape(1, d), w_in, w_out]
    if final_g is not None:
        in_specs.append(_const_spec((1, d)))
        args.append(final_g.reshape(1, d))
    return pl.pallas_call(
        functools.partial(_ffn_body, d_ff=d_ff, chunks=tuple(chunks), has_final=final_g is not None),
        grid=(n // tm,), in_specs=in_specs, out_specs=row,
        out_shape=jax.ShapeDtypeStruct((n, d), F32),
        compiler_params=_params(("parallel",)), name="ffn")(*args)


def _even_in_body(x_ref, g_ref, wu_ref, wh_ref, wvt_ref, wg_ref, wgt_ref, bg_ref, bgt_ref,
                  u_ref, q_ref, k_ref, v_ref, og_ref, vt_ref, gc_ref, gr_ref):
    hn = _rms(x_ref[...], g_ref[...]).astype(BF16)
    u_ref[...] = _dot(hn, wu_ref[...])
    for h in range(ML_HEADS):
        q_ref[h] = _dot(hn, wh_ref[h])
        k_ref[h] = _dot(hn, wh_ref[ML_HEADS + h])
        v_ref[h] = _dot(hn, wh_ref[2 * ML_HEADS + h])
        og_ref[h] = _dot(hn, wh_ref[3 * ML_HEADS + h])
        vt_ref[h] = _dot_nt(wvt_ref[h], hn)
    gc_ref[...] = _dot(hn, wg_ref[...]) + bg_ref[...]
    gr_ref[...] = _dot_nt(wgt_ref[...], hn) + bgt_ref[...]


def _even_in(x, g, w_in, b_gate):
    n, d = x.shape
    tm = 512 if n % 512 == 0 else n
    h4 = ML_HEADS
    wu = w_in[:, :POOL_WIDTH].astype(BF16)
    wh = w_in[:, POOL_WIDTH:POOL_WIDTH + 4 * ML_WIDTH].reshape(d, 4 * h4, ML_DH).transpose(1, 0, 2).astype(BF16)
    wvt = wh[2 * h4:3 * h4].transpose(0, 2, 1)
    wg = w_in[:, POOL_WIDTH + 4 * ML_WIDTH:].astype(BF16)
    row = lambda w: pl.BlockSpec((tm, w), lambda i: (i, 0))
    hrow = pl.BlockSpec((h4, tm, ML_DH), lambda i: (0, i, 0))
    hsh = jax.ShapeDtypeStruct((h4, n, ML_DH), F32)
    return pl.pallas_call(
        _even_in_body, grid=(n // tm,),
        in_specs=[row(d), _const_spec((1, d)), _const_spec(wu.shape), _const_spec(wh.shape),
                  _const_spec(wvt.shape), _const_spec(wg.shape), _const_spec((2 * h4, d)),
                  _const_spec((1, 2 * h4)), _const_spec((2 * h4, 1))],
        out_specs=[row(POOL_WIDTH), hrow, hrow, hrow, hrow,
                   pl.BlockSpec((h4, ML_DH, tm), lambda i: (0, 0, i)),
                   row(2 * h4), pl.BlockSpec((2 * h4, tm), lambda i: (0, i))],
        out_shape=[jax.ShapeDtypeStruct((n, POOL_WIDTH), F32), hsh, hsh, hsh, hsh,
                   jax.ShapeDtypeStruct((h4, ML_DH, n), F32),
                   jax.ShapeDtypeStruct((n, 2 * h4), F32), jax.ShapeDtypeStruct((2 * h4, n), F32)],
        compiler_params=_params(("parallel",)), name="even_in")(
            x, g.reshape(1, d), wu, wh, wvt, wg, wg.T, b_gate.reshape(1, -1), b_gate.reshape(-1, 1))


def _pool_body(u_ref, pre_ref, w_ref, sc_ref, y_ref, carry, full, *, tb, pos0):
    t = pl.program_id(1)

    @pl.when(t == 0)
    def _():
        carry[...] = pre_ref[...]

    u = u_ref[...]
    full[0:POOL_HALO] = carry[...]
    full[POOL_HALO:] = u
    acc = full[...]
    sums = []
    for sh in (1, 2, 4, 8):
        acc = acc + pltpu.roll(acc, sh, 0)
        sums.append(acc[POOL_HALO:])
    lane = lax.broadcasted_iota(jnp.int32, (tb, POOL_WIDTH), 1)
    grp = lane // POOL_GDIM
    win = jnp.where(grp == 0, sums[0], jnp.where(grp == 1, sums[1], jnp.where(grp == 2, sums[2], sums[3])))
    width = jnp.where(grp == 0, 2, jnp.where(grp == 1, 4, jnp.where(grp == 2, 8, 16)))
    pos = pos0 + t * tb + lax.broadcasted_iota(jnp.int32, (tb, POOL_WIDTH), 0)
    cnt = jnp.minimum(pos + 1, width).astype(F32)
    mixed = (win / cnt - u).astype(BF16)
    y_ref[...] = _dot(mixed, w_ref[...]) * sc_ref[...]
    carry[...] = full[tb:tb + POOL_HALO]


def _pool(u, prefix, w_pool, scale, pos0):
    b, t, c = u.shape
    tb = 512 if t % 512 == 0 else t
    pre = jnp.pad(prefix, ((0, 0), (POOL_HALO - prefix.shape[1], 0), (0, 0)))
    wbd = jax.scipy.linalg.block_diag(*[w_pool[i] for i in range(w_pool.shape[0])]).astype(BF16)
    return pl.pallas_call(
        functools.partial(_pool_body, tb=tb, pos0=pos0), grid=(b, t // tb),
        in_specs=[pl.BlockSpec((None, tb, c), lambda i, j: (i, j, 0)),
                  pl.BlockSpec((None, POOL_HALO, c), lambda i, j: (i, 0, 0)),
                  _const_spec((c, c)), _const_spec((1, c))],
        out_specs=pl.BlockSpec((None, tb, c), lambda i, j: (i, j, 0)),
        out_shape=jax.ShapeDtypeStruct((b, t, c), F32),
        scratch_shapes=[pltpu.VMEM((POOL_HALO, c), F32), pltpu.VMEM((tb + POOL_HALO, c), F32)],
        compiler_params=_params(("parallel", "arbitrary")), name="pool")(u, pre, wbd, scale.reshape(1, c))


def _log_sigmoid(x):
    return jnp.minimum(x, 0.0) - jnp.log(1.0 + jnp.exp(-jnp.abs(x)))


def _mlstm_body(q_ref, k_ref, v_ref, vt_ref, gc_ref, gr_ref, c0_ref, n0_ref, m0_ref,
                h_ref, c_ref, n_ref, m_ref, *, L, t_valid):
    hd = pl.program_id(1)

    @pl.when(pl.program_id(2) == 0)
    def _():
        c_ref[...] = c0_ref[...]
        n_ref[...] = n0_ref[...]
        m_ref[...] = m0_ref[...]

    gc = gc_ref[...]
    gr = gr_ref[...]
    lane8 = lax.broadcasted_iota(jnp.int32, gc.shape, 1)
    sub8 = lax.broadcasted_iota(jnp.int32, gr.shape, 0)
    lf_c = _log_sigmoid(gc)
    lf_r = _log_sigmoid(gr)
    i_c = jnp.sum(jnp.where(lane8 == hd, gc, 0.0), axis=1, keepdims=True)
    i_r = jnp.sum(jnp.where(sub8 == hd, gr, 0.0), axis=0, keepdims=True)
    row = lax.broadcasted_iota(jnp.int32, (L, L), 0)
    col = lax.broadcasted_iota(jnp.int32, (L, L), 1)
    if t_valid < L:
        tok_c = lax.broadcasted_iota(jnp.int32, (L, 1), 0) < t_valid
        tok_r = lax.broadcasted_iota(jnp.int32, (1, L), 1) < t_valid
        lf_c = jnp.where(tok_c, lf_c, 0.0)
        lf_r = jnp.where(tok_r, lf_r, 0.0)
        i_c = jnp.where(tok_c, i_c, -jnp.inf)
        i_r = jnp.where(tok_r, i_r, -jnp.inf)
    causal = row >= col
    tri = causal.astype(F32)
    cs_c = jnp.dot(tri, lf_c, precision=HIGHEST, preferred_element_type=F32)
    cs_r = lax.dot_general(lf_r, tri, (((1,), (1,)), ((), ())), precision=HIGHEST,
                           preferred_element_type=F32)
    b_c = jnp.sum(jnp.where(lane8 == hd + ML_HEADS, cs_c, 0.0), axis=1, keepdims=True)
    b_r = jnp.sum(jnp.where(sub8 == hd + ML_HEADS, cs_r, 0.0), axis=0, keepdims=True)
    m_prev = m_ref[...]
    dmat = jnp.where(causal, b_c - b_r + i_r, -jnp.inf)
    inter = b_c + m_prev
    mt = jnp.maximum(jnp.max(dmat, axis=1, keepdims=True), inter)
    w = jnp.exp(dmat - mt)
    a = jnp.exp(inter - mt)
    q = q_ref[...]
    kf = k_ref[...] * (ML_DH ** -0.5)
    qb = q.astype(BF16)
    kb = kf.astype(BF16)
    s = _dot_nt(qb, kb) * w
    c_old = c_ref[...]
    n_old = n_ref[...]
    num = _dot(s.astype(BF16), v_ref[...].astype(BF16)) + a * _dot_nt(qb, c_old.astype(BF16))
    den = jnp.sum(s, axis=1, keepdims=True) + a * jnp.sum(q * n_old, axis=1, keepdims=True)
    h_ref[...] = num / jnp.maximum(jnp.abs(den), jnp.exp(-mt))
    b_last = b_c[L - 1:L, :]
    ge_r = b_last - b_r + i_r
    ge_c = b_last - b_c + i_c
    m_new = jnp.maximum(b_last + m_prev, jnp.max(ge_r, axis=1, keepdims=True))
    wk_r = jnp.exp(ge_r - m_new)
    wk_c = jnp.exp(ge_c - m_new)
    decay = jnp.exp(b_last + m_prev - m_new)
    c_ref[...] = decay * c_old + _dot((vt_ref[...] * wk_r).astype(BF16), kb)
    n_ref[...] = decay * n_old + jnp.sum(kf * wk_c, axis=0, keepdims=True)
    m_ref[...] = m_new


def _mlstm(q, k, v, vt, gc, gr, c0, n0, m0, *, L, t_valid):
    h4, b, t, dh = q.shape
    nc = t // L
    tok = pl.BlockSpec((None, None, L, dh), lambda i, j, c: (j, i, c, 0))
    if vt.ndim == 4:
        vt_spec = pl.BlockSpec((None, None, dh, L), lambda i, j, c: (j, i, 0, c))
        gr_spec = pl.BlockSpec((None, 2 * h4, L), lambda i, j, c: (i, 0, c))
    else:
        vt_spec = pl.BlockSpec((None, dh, L), lambda i, j, c: (j, 0, i * nc + c))
        gr_spec = pl.BlockSpec((2 * h4, L), lambda i, j, c: (0, i * nc + c))
    st = lambda r, w: pl.BlockSpec((None, None, r, w), lambda i, j, c: (i, j, 0, 0))
    return pl.pallas_call(
        functools.partial(_mlstm_body, L=L, t_valid=t_valid), grid=(b, h4, nc),
        in_specs=[tok, tok, tok, vt_spec, pl.BlockSpec((None, L, 2 * h4), lambda i, j, c: (i, c, 0)), gr_spec,
                  st(dh, dh), st(1, dh), st(1, 1)],
        out_specs=[tok, st(dh, dh), st(1, dh), st(1, 1)],
        out_shape=[jax.ShapeDtypeStruct((h4, b, t, dh), F32), jax.ShapeDtypeStruct((b, h4, dh, dh), F32),
                   jax.ShapeDtypeStruct((b, h4, 1, dh), F32), jax.ShapeDtypeStruct((b, h4, 1, 1), F32)],
        compiler_params=_params(("parallel", "parallel", "arbitrary")), name="mlstm")(
            q, k, v, vt, gc, gr, c0, n0.reshape(b, h4, 1, dh), m0.reshape(b, h4, 1, 1))


def _even_out_body(x_ref, yp_ref, og_ref, hm_ref, w0_ref, w1_ref, o_ref):
    acc = _dot(yp_ref[...].astype(BF16), w0_ref[...])
    for h in range(ML_HEADS):
        acc = acc + _dot((jax.nn.sigmoid(og_ref[h]) * hm_ref[h]).astype(BF16), w1_ref[h])
    o_ref[...] = x_ref[...] + acc


def _even_out(x, ypool, og, hm, w_out):
    n, d = x.shape
    tm = 512 if n % 512 == 0 else n
    w0 = w_out[:POOL_WIDTH].astype(BF16)
    w1 = w_out[POOL_WIDTH:].reshape(ML_HEADS, ML_DH, d).astype(BF16)
    row = lambda w: pl.BlockSpec((tm, w), lambda i: (i, 0))
    hrow = pl.BlockSpec((ML_HEADS, tm, ML_DH), lambda i: (0, i, 0))
    return pl.pallas_call(
        _even_out_body, grid=(n // tm,),
        in_specs=[row(d), row(POOL_WIDTH), hrow, hrow, _const_spec(w0.shape), _const_spec(w1.shape)],
        out_specs=row(d), out_shape=jax.ShapeDtypeStruct((n, d), F32),
        compiler_params=_params(("parallel",)), name="even_out")(x, ypool, og, hm, w0, w1)


def _odd_in_body(x_ref, g_ref, wq_ref, wkv_ref, wkt_ref, wgt_ref, wc_ref,
                 q_ref, rows_ref, win_ref, kt_ref, gates_ref, ucv_ref, bg_ref):
    hn = _rms(x_ref[...], g_ref[...]).astype(BF16)
    for h in range(NSA_HEADS):
        q_ref[h] = _dot(hn, wq_ref[h]) * (HEAD_DIM ** -0.5)
    nrow = rows_ref.shape[1]
    rows_ref[...] = _dot(hn, wkv_ref[:, :nrow])
    win_ref[...] = _dot(hn, wkv_ref[:, nrow:])
    kt_ref[...] = _dot_nt(wkt_ref[...], hn).astype(BF16)
    gates_ref[...] = jax.nn.sigmoid(_dot(hn, wgt_ref[...]))
    bg_ref[...] = _dot(hn, wc_ref[:, :CONV_CH])
    ucv_ref[...] = _dot(hn, wc_ref[:, CONV_CH:2 * CONV_CH]) * _dot(hn, wc_ref[:, 2 * CONV_CH:])


def _odd_in(x, g, w_in):
    n, d = x.shape
    tm = 512 if n % 512 == 0 else n
    kvw = 6 * KV_HEADS * HEAD_DIM
    ngt = 3 * NSA_HEADS
    wq = w_in[:, :NSA_WIDTH].reshape(d, NSA_HEADS, HEAD_DIM).transpose(1, 0, 2).astype(BF16)
    wkv = w_in[:, NSA_WIDTH:NSA_WIDTH + kvw].astype(BF16)
    wgt = w_in[:, NSA_WIDTH + kvw:NSA_WIDTH + kvw + ngt].astype(BF16)
    wc = w_in[:, NSA_WIDTH + kvw + ngt:].astype(BF16)
    nrow = 4 * KV_HEADS * HEAD_DIM
    kw = KV_HEADS * HEAD_DIM
    wkt = jnp.concatenate([wkv[:, 2 * kw:3 * kw], wkv[:, 4 * kw:5 * kw]], axis=1).T
    row = lambda w: pl.BlockSpec((tm, w), lambda i: (i, 0))
    sh = lambda w: jax.ShapeDtypeStruct((n, w), F32)
    return pl.pallas_call(
        _odd_in_body, grid=(n // tm,),
        in_specs=[row(d), _const_spec((1, d)), _const_spec(wq.shape), _const_spec(wkv.shape),
                  _const_spec(wkt.shape), _const_spec(wgt.shape), _const_spec(wc.shape)],
        out_specs=[pl.BlockSpec((NSA_HEADS, tm, HEAD_DIM), lambda i: (0, i, 0)),
                   row(nrow), row(kvw - nrow), pl.BlockSpec((2 * kw, tm), lambda i: (0, i)),
                   row(ngt), row(CONV_CH), row(CONV_CH)],
        out_shape=[jax.ShapeDtypeStruct((NSA_HEADS, n, HEAD_DIM), F32),
                   sh(nrow), sh(kvw - nrow), jax.ShapeDtypeStruct((2 * kw, n), BF16),
                   sh(ngt), sh(CONV_CH), sh(CONV_CH)],
        compiler_params=_params(("parallel",)), name="odd_in")(x, g.reshape(1, d), wq, wkv, wkt, wgt, wc)


def _conv_body(u_ref, bg_ref, pre_ref, w_ref, y_ref, carry, full, *, tb):
    @pl.when(pl.program_id(1) == 0)
    def _():
        carry[...] = pre_ref[...]

    full[0:CONV_HALO] = carry[...]
    full[CONV_HALO:] = u_ref[...]
    f = full[...]
    w = w_ref[...]
    conv = f * w[2:3] + pltpu.roll(f, 1, 0) * w[1:2] + pltpu.roll(f, 2, 0) * w[0:1]
    y_ref[...] = bg_ref[...] * conv[CONV_HALO:]
    carry[...] = full[tb:tb + CONV_HALO]


def _conv(u, bg, prefix, conv_w):
    b, t, c = u.shape
    tb = 512 if t % 512 == 0 else t
    pre = jnp.pad(prefix, ((0, 0), (CONV_HALO - prefix.shape[1], 0), (0, 0)))
    blk = pl.BlockSpec((None, tb, c), lambda i, j: (i, j, 0))
    return pl.pallas_call(
        functools.partial(_conv_body, tb=tb), grid=(b, t // tb),
        in_specs=[blk, blk, pl.BlockSpec((None, CONV_HALO, c), lambda i, j: (i, 0, 0)), _const_spec((CONV_K, c))],
        out_specs=blk, out_shape=jax.ShapeDtypeStruct((b, t, c), F32),
        scratch_shapes=[pltpu.VMEM((CONV_HALO, c), F32), pltpu.VMEM((tb + CONV_HALO, c), F32)],
        compiler_params=_params(("parallel", "arbitrary")), name="conv")(u, bg, pre, conv_w)


def _odd_out_body(x_ref, o_ref, yc_ref, w0_ref, w1_ref, out_ref):
    out_ref[...] = (x_ref[...] + _dot(o_ref[...].astype(BF16), w0_ref[...])
                    + _dot(yc_ref[...].astype(BF16), w1_ref[...]))


def _odd_out(x, o, yconv, w_out):
    n, d = x.shape
    tm = 512 if n % 512 == 0 else n
    w0 = w_out[:NSA_WIDTH].astype(BF16)
    w1 = w_out[NSA_WIDTH:].astype(BF16)
    row = lambda w: pl.BlockSpec((tm, w), lambda i: (i, 0))
    return pl.pallas_call(
        _odd_out_body, grid=(n // tm,),
        in_specs=[row(d), row(NSA_WIDTH), row(CONV_CH), _const_spec(w0.shape), _const_spec(w1.shape)],
        out_specs=row(d), out_shape=jax.ShapeDtypeStruct((n, d), F32),
        compiler_params=_params(("parallel",)), name="odd_out")(x, o, yconv, w0, w1)


def _cmp_weights(cmp_pos_w, cmp_w, cmp_b):
    eye = jnp.eye(PAGE_SIZE // CMP_STRIDE, dtype=F32)
    mats = []
    for kv in range(2):
        ea = jnp.kron(eye, cmp_pos_w[kv, :CMP_STRIDE][None, :])
        eb = jnp.kron(eye, cmp_pos_w[kv, CMP_STRIDE:][None, :])
        top = eb + jnp.concatenate([jnp.zeros_like(ea[:1]), ea[:-1]], axis=0)
        mats.append(jnp.concatenate([top, ea[-1:], jnp.zeros((7, PAGE_SIZE), F32)], axis=0))
    wmix = jnp.stack(mats).astype(BF16)
    wbd = jax.scipy.linalg.block_diag(cmp_w[0], cmp_w[0], cmp_w[1], cmp_w[1]).astype(BF16)
    bias = jnp.concatenate([cmp_b[0], cmp_b[0], cmp_b[1], cmp_b[1]]).reshape(1, -1)
    return wmix, wbd, bias


def _compress_body(*refs, n_pages, n_prefetch):
    refs = refs[n_prefetch:]
    pages = refs[:n_pages]
    wmix_ref, wbd_ref, bias_ref, out_ref, carry = refs[n_pages:]

    @pl.when(pl.program_id(1) == 0)
    def _():
        carry[...] = jnp.zeros(carry.shape, F32)

    half = 2 * HEAD_DIM
    first = lax.broadcasted_iota(jnp.int32, (8, 2 * half), 0) == 0
    for p in range(n_pages):
        rows = pages[p][...].astype(BF16)
        rk = _dot(wmix_ref[0], rows[:, :half])
        rv = _dot(wmix_ref[1], rows[:, half:])
        pre = jnp.concatenate([rk[0:8], rv[0:8]], axis=1) + jnp.where(first, carry[...], 0.0)
        carry[...] = jnp.broadcast_to(jnp.concatenate([rk[8:9], rv[8:9]], axis=1), carry.shape)
        out_ref[8 * p:8 * p + 8, :] = _dot(pre.astype(BF16), wbd_ref[...]) + bias_ref[...]


def _compress_prompt(rows, wmix, wbd, bias):
    b, t, _ = rows.shape
    pg = 4
    ncol = 4 * HEAD_DIM
    specs = [pl.BlockSpec((None, PAGE_SIZE, ncol), functools.partial(lambda i, j, p: (i, j * pg + p, 0), p=p))
             for p in range(pg)]
    return pl.pallas_call(
        functools.partial(_compress_body, n_pages=pg, n_prefetch=0), grid=(b, t // (PAGE_SIZE * pg)),
        in_specs=specs + [_const_spec(wmix.shape), _const_spec(wbd.shape), _const_spec(bias.shape)],
        out_specs=pl.BlockSpec((None, 8 * pg, ncol), lambda i, j: (i, j, 0)),
        out_shape=jax.ShapeDtypeStruct((b, t // CMP_STRIDE, ncol), F32),
        scratch_shapes=[pltpu.VMEM((8, ncol), F32)],
        compiler_params=_params(("parallel", "arbitrary")), name="compress_prompt")(*([rows] * pg), wmix, wbd, bias)


def _compress_paged(cache, page_table, wmix, wbd, bias):
    b, npg = page_table.shape
    pg = 8
    ncol = 4 * HEAD_DIM
    specs = [pl.BlockSpec((None, PAGE_SIZE, ncol),
                          functools.partial(lambda i, j, pt, p: (pt[i, j * pg + p], 0, 0), p=p))
             for p in range(pg)]
    return pl.pallas_call(
        functools.partial(_compress_body, n_pages=pg, n_prefetch=1),
        grid_spec=pltpu.PrefetchScalarGridSpec(
            num_scalar_prefetch=1, grid=(b, npg // pg),
            in_specs=specs + [_const_spec(wmix.shape), _const_spec(wbd.shape), _const_spec(bias.shape)],
            out_specs=pl.BlockSpec((None, 8 * pg, ncol), lambda i, j, pt: (i, j, 0)),
            scratch_shapes=[pltpu.VMEM((8, ncol), F32)]),
        out_shape=jax.ShapeDtypeStruct((b, npg * 8, ncol), F32),
        compiler_params=_params(("parallel", "arbitrary")), name="compress_paged")(
            page_table, *([cache] * pg), wmix, wbd, bias)


def _alibi_slopes(rows_per_head):
    sl = (2.0 ** (-8.0 * np.arange(1, NSA_HEADS + 1) / NSA_HEADS)).astype(np.float32).reshape(KV_HEADS, GQ)
    return jnp.asarray(np.repeat(sl, rows_per_head, axis=1)[:, :, None])


def _score_matrix(n_cmp_rows, n_blocks, lanes):
    m = np.arange(n_cmp_rows)[:, None]
    j = np.arange(lanes)[None, :]
    return jnp.asarray(((m >= 4 * j) & (m <= 4 * j + 4) & (m >= 1) & (j < n_blocks)).astype(np.float32))


def _softmax_rows(s, mask):
    s = jnp.where(mask, s, -jnp.inf)
    m = jnp.max(s, axis=-1, keepdims=True)
    m = jnp.where(m > -jnp.inf, m, 0.0)
    e = jnp.exp(s - m)
    return e / jnp.maximum(jnp.sum(e, axis=-1, keepdims=True), 1e-30)


def _select_blocks(score, cur, n_blocks):
    r, lanes = score.shape
    blk = lax.broadcasted_iota(jnp.int32, (r, lanes), 1)
    forced = (blk == 0) | (blk == cur) | (blk == cur - 1)
    val = jnp.where(forced, jnp.inf, jnp.where(blk <= cur, score, -jnp.inf))
    rank = jnp.zeros((r, lanes), jnp.int32)
    for i in range(n_blocks):
        ci = val[:, i:i + 1]
        ahead = (ci > val) | ((ci == val) & (blk > i))
        rank = rank + ahead.astype(jnp.int32)
    return ((rank < min(SEL_TOPN, n_blocks)) & (blk < n_blocks)).astype(F32)


def _expand_blocks(selmask, first_block, n_keys, reps):
    lanes = selmask.shape[1]
    b = lax.broadcasted_iota(jnp.int32, (lanes, n_keys), 0)
    k = lax.broadcasted_iota(jnp.int32, (lanes, n_keys), 1)
    onehot = jnp.where(b == first_block + k // SEL_BLOCK, 1.0, 0.0).astype(BF16)
    m = _dot(selmask.astype(BF16), onehot)
    return jnp.concatenate([m] * reps, axis=0) > 0.5


POS_ROWS = 16


def _slope_pieces(rows_per_head):
    sl = _alibi_slopes(rows_per_head)
    hi = sl.astype(BF16)
    mid = (sl - hi.astype(F32)).astype(BF16)
    lo = (sl - hi.astype(F32) - mid.astype(F32)).astype(BF16)
    pad = jnp.zeros(sl.shape[:2] + (POS_ROWS - 6,), BF16)
    return jnp.concatenate([hi, mid, lo, hi, mid, lo, pad], axis=2)


def _key_constants(t_len):
    k = np.arange(t_len)
    hi = (SEL_BLOCK * (k // SEL_BLOCK)).astype(np.float32)
    lo = (k % SEL_BLOCK).astype(np.float32)
    pos = np.stack([hi, hi, hi, lo, lo, lo] + [np.zeros(t_len, np.float32)] * (POS_ROWS - 6))
    onehot = (np.arange(t_len // SEL_BLOCK)[:, None] == (k // SEL_BLOCK)[None, :]).astype(np.float32)
    return jnp.asarray(np.concatenate([pos, onehot], axis=0)).astype(BF16)


def _select_bias_pair(scores, cur, n_blocks):
    r = scores[0].shape[0]
    lanes = KV_HEADS * n_blocks
    lane = lax.broadcasted_iota(jnp.int32, (r, lanes), 1)
    blk = lane % n_blocks
    forced = (blk == 0) | (blk == cur) | (blk == cur - 1)
    val = jnp.where(forced, jnp.inf, jnp.where(blk <= cur, jnp.concatenate(scores, axis=1), -jnp.inf))
    rank = jnp.zeros((r, lanes), F32)
    for i in range(n_blocks):
        ci = jnp.where(lane < n_blocks, val[:, i:i + 1], val[:, n_blocks + i:n_blocks + i + 1])
        ahead = (ci > val) | ((ci == val) & (blk > i))
        rank = rank + jnp.where(ahead, 1.0, 0.0)
    bias = jnp.where((rank < min(SEL_TOPN, n_blocks)) & (blk <= cur), 0.0, NEG)
    return [bias[:, g * n_blocks:(g + 1) * n_blocks] for g in range(KV_HEADS)]


def _nsa_prompt_body(q_ref, gates_ref, kvc_ref, kt_ref, sel_ref, win_ref, slope_ref, saug_ref, kconst_ref,
                     smat_ref, o_ref, *, t_len, kc, wb):
    bi = pl.program_id(1)
    p0 = bi * Q_BLOCK
    nrow = GQ * Q_BLOCK
    n_cmp = kvc_ref.shape[0]
    n_blocks = t_len // SEL_BLOCK
    kw = KV_HEADS * HEAD_DIM
    qpos = lax.broadcasted_iota(jnp.int32, (Q_BLOCK, 1), 0) + p0
    tpos = jnp.concatenate([qpos] * GQ, axis=0)
    gates = gates_ref[...]

    cpos = CMP_STRIDE * lax.broadcasted_iota(jnp.int32, (1, n_cmp), 1) + (CMP_STRIDE - 1)
    dist = tpos - cpos
    cmask = (dist >= 0) & (cpos >= 2 * CMP_STRIDE - 1)
    qgs, o_cmps, scores = [], [], []
    for g in range(KV_HEADS):
        qg = q_ref[g * GQ:(g + 1) * GQ].reshape(nrow, HEAD_DIM).astype(BF16)
        ks, vs = g * HEAD_DIM, (KV_HEADS + g) * HEAD_DIM
        s = _dot_nt(qg, kvc_ref[:, ks:ks + HEAD_DIM].astype(BF16)) - slope_ref[g] * dist.astype(F32)
        p_cmp = _softmax_rows(s, cmask)
        o_cmps.append(_dot(p_cmp.astype(BF16), kvc_ref[:, vs:vs + HEAD_DIM].astype(BF16)))
        imp = jnp.sum(p_cmp.reshape(GQ, Q_BLOCK, n_cmp), axis=0)
        scores.append(jnp.dot(imp, smat_ref[...], precision=HIGHEST, preferred_element_type=F32))
        qgs.append(qg)
    biases = _select_bias_pair(scores, bi, n_blocks)
    q_aug = [jnp.concatenate([qgs[g], saug_ref[g], jnp.concatenate([biases[g]] * GQ, axis=0).astype(BF16)], axis=1)
             for g in range(KV_HEADS)]

    def chunk_scores(g, c):
        k0 = pl.multiple_of(c * kc, kc)
        kat = jnp.concatenate([kt_ref[g * HEAD_DIM:(g + 1) * HEAD_DIM, pl.ds(k0, kc)],
                               kconst_ref[:, pl.ds(k0, kc)]], axis=0)
        return _dot(q_aug[g], kat), k0

    def accumulate(g, s, k0, carry):
        m_i, l_i, acc = carry
        vs = (KV_HEADS + g) * HEAD_DIM
        m_new = jnp.maximum(m_i, jnp.max(s, axis=1, keepdims=True))
        pr = jnp.exp(s - m_new)
        alpha = jnp.exp(m_i - m_new)
        vv = sel_ref[pl.ds(k0, kc), vs:vs + HEAD_DIM].astype(BF16)
        return (m_new, alpha * l_i + jnp.sum(pr, axis=1, keepdims=True), alpha * acc + _dot(pr.astype(BF16), vv))

    def sel_step(c, carry):
        out = []
        for g in range(KV_HEADS):
            s, k0 = chunk_scores(g, c)
            out.append(accumulate(g, s, k0, carry[g]))
        return tuple(out)

    one = (jnp.full((nrow, 1), NEG, F32), jnp.zeros((nrow, 1), F32), jnp.zeros((nrow, HEAD_DIM), F32))
    last = (p0 + Q_BLOCK - 1) // kc
    carry = lax.fori_loop(0, last, sel_step, (one,) * KV_HEADS)
    o_sels = []
    for g in range(KV_HEADS):
        s, k0 = chunk_scores(g, last)
        kpos = k0 + lax.broadcasted_iota(jnp.int32, (1, kc), 1)
        _, l_s, acc_s = accumulate(g, jnp.where(kpos <= tpos, s, NEG), k0, carry[g])
        o_sels.append(acc_s / jnp.maximum(l_s, 1e-30))

    wstart = pl.multiple_of(jnp.clip((p0 - WINDOW) // 128 * 128, 0, t_len - wb), 128)
    wpos = wstart + lax.broadcasted_iota(jnp.int32, (1, wb), 1)
    wmask = (wpos <= tpos) & (wpos > tpos - WINDOW)
    for g in range(KV_HEADS):
        vs = (KV_HEADS + g) * HEAD_DIM
        katw = jnp.concatenate([kt_ref[kw + g * HEAD_DIM:kw + (g + 1) * HEAD_DIM, pl.ds(wstart, wb)],
                                kconst_ref[0:POS_ROWS, pl.ds(wstart, wb)]], axis=0)
        p_win = _softmax_rows(_dot(q_aug[g][:, :HEAD_DIM + POS_ROWS], katw), wmask)
        o_win = _dot(p_win.astype(BF16), win_ref[pl.ds(wstart, wb), vs:vs + HEAD_DIM].astype(BF16))
        for i in range(GQ):
            hh = g * GQ + i
            r = slice(i * Q_BLOCK, (i + 1) * Q_BLOCK)
            o_ref[:, hh * HEAD_DIM:(hh + 1) * HEAD_DIM] = (
                gates[:, 3 * hh:3 * hh + 1] * o_cmps[g][r] + gates[:, 3 * hh + 1:3 * hh + 2] * o_sels[g][r]
                + gates[:, 3 * hh + 2:3 * hh + 3] * o_win[r])


def _nsa_prompt(q, gates, kvc, kt, rows, win):
    _, b, t, _ = q.shape
    assert t % 128 == 0
    kc = 512 if t % 512 == 0 else t
    wb = min(WINDOW + 128, t)
    n_cmp = kvc.shape[1]
    n_blocks = t // SEL_BLOCK
    ncol = 4 * HEAD_DIM
    slopes = _alibi_slopes(Q_BLOCK)
    saug = _slope_pieces(Q_BLOCK)
    kconst = _key_constants(t)
    smat = _score_matrix(n_cmp, n_blocks, n_blocks)
    return pl.pallas_call(
        functools.partial(_nsa_prompt_body, t_len=t, kc=kc, wb=wb), grid=(b, t // Q_BLOCK),
        in_specs=[pl.BlockSpec((NSA_HEADS, None, Q_BLOCK, HEAD_DIM), lambda i, j: (0, i, j, 0)),
                  pl.BlockSpec((None, Q_BLOCK, 3 * NSA_HEADS), lambda i, j: (i, j, 0)),
                  pl.BlockSpec((None, n_cmp, ncol), lambda i, j: (i, 0, 0)),
                  pl.BlockSpec((kt.shape[0], t), lambda i, j: (0, i)),
                  pl.BlockSpec((None, t, ncol), lambda i, j: (i, 0, 1)),
                  pl.BlockSpec((None, t, ncol), lambda i, j: (i, 0, 0)),
                  _const_spec(slopes.shape), _const_spec(saug.shape), _const_spec(kconst.shape),
                  _const_spec(smat.shape)],
        out_specs=pl.BlockSpec((None, Q_BLOCK, NSA_WIDTH), lambda i, j: (i, j, 0)),
        out_shape=jax.ShapeDtypeStruct((b, t, NSA_WIDTH), F32),
        compiler_params=_params(("parallel", "arbitrary")), name="nsa_prompt")(
            q, gates, kvc, kt, rows, win, slopes, saug, kconst, smat)


def _nsa_dec_select_body(q_ref, kvc_ref, slope_ref, smat_ref, ocmp_ref, sel_ref, *, past_len, tq, n_blocks):
    n_cmp = kvc_ref.shape[0]
    nrow = GQ * tq
    qpos = lax.broadcasted_iota(jnp.int32, (tq, 1), 0) + past_len
    tpos = jnp.concatenate([qpos] * GQ, axis=0)
    cpos = CMP_STRIDE * lax.broadcasted_iota(jnp.int32, (1, n_cmp), 1) + (CMP_STRIDE - 1)
    dist = tpos - cpos
    for g in range(KV_HEADS):
        qg = q_ref[g * GQ:(g + 1) * GQ].reshape(nrow, HEAD_DIM).astype(BF16)
        ks, vs = g * HEAD_DIM, (KV_HEADS + g) * HEAD_DIM
        s = _dot_nt(qg, kvc_ref[:, ks:ks + HEAD_DIM].astype(BF16)) - slope_ref[g] * dist.astype(F32)
        p_cmp = _softmax_rows(s, (dist >= 0) & (cpos >= 2 * CMP_STRIDE - 1))
        ocmp_ref[g] = _dot(p_cmp.astype(BF16), kvc_ref[:, vs:vs + HEAD_DIM].astype(BF16))
        imp = jnp.sum(p_cmp.reshape(GQ, tq, n_cmp), axis=0)
        score = jnp.dot(imp, smat_ref[...], precision=HIGHEST, preferred_element_type=F32)
        sel_ref[g] = _select_blocks(score, qpos // SEL_BLOCK, n_blocks)


def _nsa_dec_select(q, kvc, past_len, n_blocks):
    _, b, tq, _ = q.shape
    n_cmp = kvc.shape[1]
    lanes = -(-n_blocks // 128) * 128
    slopes = _alibi_slopes(tq)
    smat = _score_matrix(n_cmp, n_blocks, lanes)
    return pl.pallas_call(
        functools.partial(_nsa_dec_select_body, past_len=past_len, tq=tq, n_blocks=n_blocks), grid=(b,),
        in_specs=[pl.BlockSpec((NSA_HEADS, None, tq, HEAD_DIM), lambda i: (0, i, 0, 0)),
                  pl.BlockSpec((None, n_cmp, 4 * HEAD_DIM), lambda i: (i, 0, 0)),
                  _const_spec(slopes.shape), _const_spec(smat.shape)],
        out_specs=[pl.BlockSpec((None, KV_HEADS, GQ * tq, HEAD_DIM), lambda i: (i, 0, 0, 0)),
                   pl.BlockSpec((None, KV_HEADS, tq, lanes), lambda i: (i, 0, 0, 0))],
        out_shape=[jax.ShapeDtypeStruct((b, KV_HEADS, GQ * tq, HEAD_DIM), F32),
                   jax.ShapeDtypeStruct((b, KV_HEADS, tq, lanes), F32)],
        compiler_params=_params(("parallel",)), name="nsa_dec_select")(q, kvc, slopes, smat)


def _nsa_dec_attend_body(pt_ref, q_ref, selm_ref, ocmp_ref, gates_ref, newsel_ref, winbuf_ref, newwin_ref,
                         slope_ref, *rest, past_len, tq, t_valid, n_pages):
    pages = rest[:n_pages]
    o_ref, m_sc, l_sc, acc_sc = rest[n_pages:]
    step = pl.program_id(1)
    nrow = GQ * tq
    qpos = lax.broadcasted_iota(jnp.int32, (tq, 1), 0) + past_len
    tpos = jnp.concatenate([qpos] * GQ, axis=0)

    @pl.when(step == 0)
    def _():
        m_sc[...] = jnp.full(m_sc.shape, NEG, F32)
        l_sc[...] = jnp.zeros(l_sc.shape, F32)
        acc_sc[...] = jnp.zeros(acc_sc.shape, F32)

    def update(g, sc, ok, vals):
        sc = jnp.where(ok, sc, NEG)
        m_i = m_sc[g]
        m_new = jnp.maximum(m_i, jnp.max(sc, axis=1, keepdims=True))
        pr = jnp.where(ok, jnp.exp(sc - m_new), 0.0)
        alpha = jnp.exp(m_i - m_new)
        l_sc[g] = alpha * l_sc[g] + jnp.sum(pr, axis=1, keepdims=True)
        acc_sc[g] = alpha * acc_sc[g] + _dot(pr.astype(BF16), vals)
        m_sc[g] = m_new

    nk = n_pages * PAGE_SIZE
    k0 = step * nk
    kpos = k0 + lax.broadcasted_iota(jnp.int32, (1, nk), 1)
    for g in range(KV_HEADS):
        qg = q_ref[g * GQ:(g + 1) * GQ].reshape(nrow, HEAD_DIM).astype(BF16)
        ks, vs = g * HEAD_DIM, (KV_HEADS + g) * HEAD_DIM
        kk = jnp.concatenate([pg[:, ks:ks + HEAD_DIM] for pg in pages], axis=0).astype(BF16)
        vv = jnp.concatenate([pg[:, vs:vs + HEAD_DIM] for pg in pages], axis=0).astype(BF16)
        d = tpos - kpos
        sc = _dot_nt(qg, kk) - slope_ref[g] * d.astype(F32)
        ok = _expand_blocks(selm_ref[g], step * (nk // SEL_BLOCK), nk, GQ) & (d >= 0)
        update(g, sc, ok, vv)

    @pl.when(step == pl.num_programs(1) - 1)
    def _():
        gates = gates_ref[...]
        tn = newsel_ref.shape[0]
        npos = past_len + lax.broadcasted_iota(jnp.int32, (1, tn), 1)
        nd = tpos - npos
        valid_new = (nd >= 0) & (npos < past_len + t_valid)
        wlen = winbuf_ref.shape[0]
        wpos = past_len - wlen + lax.broadcasted_iota(jnp.int32, (1, wlen), 1)
        wd = tpos - wpos
        for g in range(KV_HEADS):
            qg = q_ref[g * GQ:(g + 1) * GQ].reshape(nrow, HEAD_DIM).astype(BF16)
            ks, vs = g * HEAD_DIM, (KV_HEADS + g) * HEAD_DIM
            slope = slope_ref[g]
            sc = _dot_nt(qg, newsel_ref[:, ks:ks + HEAD_DIM].astype(BF16)) - slope * nd.astype(F32)
            last_blk = past_len // SEL_BLOCK
            picked = selm_ref[g][:, last_blk:last_blk + 1] > 0.5
            ok = jnp.concatenate([picked] * GQ, axis=0) & valid_new
            update(g, sc, ok, newsel_ref[:, vs:vs + HEAD_DIM].astype(BF16))
            o_sel = acc_sc[g] / jnp.maximum(l_sc[g], 1e-30)
            s1 = _dot_nt(qg, winbuf_ref[:, ks:ks + HEAD_DIM].astype(BF16)) - slope * wd.astype(F32)
            s2 = _dot_nt(qg, newwin_ref[:, ks:ks + HEAD_DIM].astype(BF16)) - slope * nd.astype(F32)
            ok1 = (wd >= 0) & (wd < WINDOW) & (wpos >= 0)
            ok2 = valid_new & (nd < WINDOW)
            s1 = jnp.where(ok1, s1, -jnp.inf)
            s2 = jnp.where(ok2, s2, -jnp.inf)
            mx = jnp.maximum(jnp.max(s1, axis=1, keepdims=True), jnp.max(s2, axis=1, keepdims=True))
            mx = jnp.where(mx > -jnp.inf, mx, 0.0)
            e1 = jnp.exp(s1 - mx)
            e2 = jnp.exp(s2 - mx)
            den = jnp.maximum(jnp.sum(e1, axis=1, keepdims=True) + jnp.sum(e2, axis=1, keepdims=True), 1e-30)
            o_win = (_dot(e1.astype(BF16), winbuf_ref[:, vs:vs + HEAD_DIM].astype(BF16))
                     + _dot(e2.astype(BF16), newwin_ref[:, vs:vs + HEAD_DIM].astype(BF16))) / den
            o_cmp = ocmp_ref[g]
            for i in range(GQ):
                hh = g * GQ + i
                r = slice(i * tq, (i + 1) * tq)
                o_ref[:, hh * HEAD_DIM:(hh + 1) * HEAD_DIM] = (
                    gates[:, 3 * hh:3 * hh + 1] * o_cmp[r] + gates[:, 3 * hh + 1:3 * hh + 2] * o_sel[r]
                    + gates[:, 3 * hh + 2:3 * hh + 3] * o_win[r])


def _nsa_dec_attend(q, selmask, o_cmp, gates, newsel, winbuf, newwin, cache, page_table, past_len, t_valid):
    _, b, tq, _ = q.shape
    npg = page_table.shape[1]
    pg = 8
    ncol = 4 * HEAD_DIM
    lanes = selmask.shape[-1]
    slopes = _alibi_slopes(tq)
    wlen = winbuf.shape[1]
    bspec = lambda shape: pl.BlockSpec((None,) + shape, lambda i, j, pt: (i,) + (0,) * len(shape))
    page_specs = [pl.BlockSpec((None, PAGE_SIZE, ncol),
                               functools.partial(lambda i, j, pt, p: (pt[i, j * pg + p], 0, 1), p=p))
                  for p in range(pg)]
    return pl.pallas_call(
        functools.partial(_nsa_dec_attend_body, past_len=past_len, tq=tq, t_valid=t_valid, n_pages=pg),
        grid_spec=pltpu.PrefetchScalarGridSpec(
            num_scalar_prefetch=1, grid=(b, npg // pg),
            in_specs=[pl.BlockSpec((NSA_HEADS, None, tq, HEAD_DIM), lambda i, j, pt: (0, i, 0, 0)),
                      bspec((KV_HEADS, tq, lanes)), bspec((KV_HEADS, GQ * tq, HEAD_DIM)),
                      bspec((tq, 3 * NSA_HEADS)), bspec((tq, ncol)), bspec((wlen, ncol)), bspec((tq, ncol)),
                      pl.BlockSpec(slopes.shape, lambda i, j, pt: (0, 0, 0))] + page_specs,
            out_specs=bspec((tq, NSA_WIDTH)),
            scratch_shapes=[pltpu.VMEM((KV_HEADS, GQ * tq, 1), F32), pltpu.VMEM((KV_HEADS, GQ * tq, 1), F32),
                            pltpu.VMEM((KV_HEADS, GQ * tq, HEAD_DIM), F32)]),
        out_shape=jax.ShapeDtypeStruct((b, tq, NSA_WIDTH), F32),
        compiler_params=_params(("parallel", "arbitrary")), name="nsa_dec_attend")(
            page_table, q, selmask, o_cmp, gates, newsel, winbuf, newwin, slopes, *([cache] * pg))


def _pad_t(a, axis, to):
    pad = [(0, 0)] * a.ndim
    pad[axis] = (0, to - a.shape[axis])
    return jnp.pad(a, pad)


def _even_layer(x, b, t, pos0, pool_buf, c0, n0, m0, g, w_in, b_gate, w_pool, pool_scale, w_out):
    n = b * t
    u, q, k, v, og, vt, gc, gr = _even_in(x, g, w_in, b_gate)
    u3 = u.reshape(b, t, POOL_WIDTH)
    new_pool = jnp.concatenate([pool_buf, u3], axis=1)[:, -pool_buf.shape[1]:]
    h4 = ML_HEADS
    if t % 256 == 0:
        tp, L = t, 256
        vt_in, gr_in = vt, gr
    else:
        tp = L = -(-t // 8) * 8
        vt_in = _pad_t(vt.reshape(h4, ML_DH, b, t), 3, tp).transpose(0, 2, 1, 3)
        gr_in = _pad_t(gr.reshape(2 * h4, b, t), 2, tp).transpose(1, 0, 2)
    seq = lambda a: _pad_t(a.reshape(h4, b, t, ML_DH), 2, tp)
    ypool = _pool(_pad_t(u3, 1, tp), pool_buf, w_pool, pool_scale, pos0)[:, :t].reshape(n, POOL_WIDTH)
    hm, c, nn, m = _mlstm(seq(q), seq(k), seq(v), vt_in, _pad_t(gc.reshape(b, t, 2 * h4), 1, tp), gr_in,
                          c0, n0, m0, L=L, t_valid=t)
    hm = hm[:, :, :t].reshape(h4, n, ML_DH)
    x = _even_out(x, ypool, og, hm, w_out)
    return x, new_pool, c, nn.reshape(b, h4, ML_DH), m.reshape(b, h4)


def _odd_layer(x, b, t, pos0, cache, page_table, win_buf, conv_buf, g, w_in, cmp_pos_w, cmp_w, cmp_b,
               conv_w, w_out):
    n = b * t
    q, rows, win, kt, gates, ucv, bg = _odd_in(x, g, w_in)
    wmix, wbd, bias = _cmp_weights(cmp_pos_w, cmp_w, cmp_b)
    ncol = 4 * HEAD_DIM
    rows3 = rows.reshape(b, t, 2 * ncol)
    win3 = win.reshape(b, t, ncol)
    u3 = ucv.reshape(b, t, CONV_CH)
    new_conv = jnp.concatenate([conv_buf, u3], axis=1)[:, -(CONV_K - 1):]
    if cache is None:
        kvc = _compress_prompt(rows3, wmix, wbd, bias)
        o = _nsa_prompt(q.reshape(NSA_HEADS, b, t, HEAD_DIM), gates.reshape(b, t, 3 * NSA_HEADS), kvc, kt,
                        rows3, win3)
        new_win = win3[:, -min(WINDOW, t):]
        tp = t
    else:
        assert pos0 % SEL_BLOCK + t <= SEL_BLOCK
        tp = -(-t // 8) * 8
        n_pool = cache.shape[0]
        cache2 = cache.reshape(n_pool, PAGE_SIZE, 2 * ncol)
        kvc = _compress_paged(cache2, page_table, wmix, wbd, bias)
        qp = _pad_t(q.reshape(NSA_HEADS, b, t, HEAD_DIM), 2, tp)
        n_blocks = -(-(pos0 + t) // SEL_BLOCK)
        o_cmp, selmask = _nsa_dec_select(qp, kvc, pos0, n_blocks)
        wlen = win_buf.shape[1]
        winb = win_buf.reshape(b, wlen, ncol)
        o = _nsa_dec_attend(qp, selmask, o_cmp, _pad_t(gates.reshape(b, t, -1), 1, tp),
                            _pad_t(rows3[:, :, ncol:], 1, tp), winb, _pad_t(win3, 1, tp),
                            cache2, page_table, pos0, t)[:, :t]
        new_win = jnp.concatenate([winb, win3], axis=1)[:, -wlen:]
    yconv = _conv(_pad_t(u3, 1, tp), _pad_t(bg.reshape(b, t, CONV_CH), 1, tp), conv_buf, conv_w)[:, :t]
    x = _odd_out(x, o.reshape(n, NSA_WIDTH), yconv.reshape(n, CONV_CH), w_out)
    new_rows = rows3.reshape(b, t, 4, KV_HEADS, HEAD_DIM)
    new_win = new_win.reshape(b, new_win.shape[1], 2, KV_HEADS, HEAD_DIM)
    return x, new_rows, new_win, new_conv


def _trunk(x3, pos0, pool_buf, ml_c, ml_n, ml_m, kv_cache, page_table, win_buf, conv_buf, p):
    b, t, d = x3.shape
    depth = p["norm_g"].shape[0]
    x = x3.reshape(b * t, d)
    pools, cs, ns, ms, rows, wins, convs = [], [], [], [], [], [], []
    for l in range(depth):
        j = l // 2
        x = _ffn(x, p["norm_g"][l, 0], p["w_ffn_in"][l][0], p["w_ffn_out"][l][0])
        if l % 2 == 0:
            x, pb, c, n, m = _even_layer(x, b, t, pos0, pool_buf[j], ml_c[j], ml_n[j], ml_m[j], p["norm_g"][l, 1],
                                         p["w_in_even"][j], p["b_gate_even"][j], p["w_pool"][j],
                                         p["pool_scale"][j], p["w_out_even"][j])
            pools.append(pb)
            cs.append(c)
            ns.append(n)
            ms.append(m)
        else:
            cache = None if kv_cache is None else kv_cache[j]
            wb = None if win_buf is None else win_buf[j]
            x, r, wn, cn = _odd_layer(x, b, t, pos0, cache, page_table, wb, conv_buf[j], p["norm_g"][l, 1],
                                      p["w_in_odd"][j], p["cmp_pos_w"][j], p["cmp_w"][j], p["cmp_b"][j],
                                      p["conv_w"][j], p["w_out_odd"][j])
            rows.append(r)
            wins.append(wn)
            convs.append(cn)
        x = _ffn(x, p["norm_g"][l, 2], p["w_ffn_in"][l][1], p["w_ffn_out"][l][1],
                 final_g=p["final_g"] if l == depth - 1 else None)
    states = (jnp.stack(pools), jnp.stack(cs), jnp.stack(ns), jnp.stack(ms),
              jnp.stack(rows), jnp.stack(wins), jnp.stack(convs))
    return x.reshape(b, t, d), states


def kernel(x_prompt, x_sample, state_pool, state_mlstm_c, state_mlstm_n, state_mlstm_m, cache_nsa_kv, state_win_kv, state_conv, page_table, norm_g, final_g, w_ffn_in, w_ffn_out, w_in_even, b_gate_even, w_pool, pool_scale, w_out_even, w_in_odd, cmp_pos_w, cmp_w, cmp_b, conv_w, w_out_odd):
    bp = x_prompt.shape[0]
    n_even, n_odd = state_pool.shape[0], state_conv.shape[0]
    past_len = page_table.shape[1] * PAGE_SIZE
    p = dict(norm_g=norm_g, final_g=final_g, w_ffn_in=w_ffn_in.astype(BF16), w_ffn_out=w_ffn_out.astype(BF16),
             w_in_even=w_in_even, b_gate_even=b_gate_even, w_pool=w_pool, pool_scale=pool_scale,
             w_out_even=w_out_even, w_in_odd=w_in_odd, cmp_pos_w=cmp_pos_w, cmp_w=cmp_w, cmp_b=cmp_b,
             conv_w=conv_w, w_out_odd=w_out_odd)
    pool0 = jnp.zeros((n_even, bp) + state_pool.shape[2:], F32)
    c0 = jnp.zeros((n_even, bp) + state_mlstm_c.shape[2:], F32)
    n0 = jnp.zeros((n_even, bp) + state_mlstm_n.shape[2:], F32)
    m0 = jnp.zeros((n_even, bp) + state_mlstm_m.shape[2:], F32)
    conv0 = jnp.zeros((n_odd, bp) + state_conv.shape[2:], F32)
    y_p, (pool_p, c_p, n_p, m_p, kv_p, win_p, conv_p) = _trunk(
        x_prompt, 0, pool0, c0, n0, m0, None, None, None, conv0, p)
    y_s, (pool_s, c_s, n_s, m_s, kv_s, win_s, conv_s) = _trunk(
        x_sample, past_len, state_pool, state_mlstm_c, state_mlstm_n, state_mlstm_m,
        cache_nsa_kv, page_table, state_win_kv, state_conv, p)
    return (y_p, y_s, pool_p, pool_s, c_p, c_s, n_p, n_s, m_p, m_s,
            kv_p, kv_s, win_p, win_s, conv_p, conv_s)
```

```python
import functools

import numpy as np
import jax
import jax.numpy as jnp
from jax import lax
from jax.experimental import pallas as pl
from jax.experimental.pallas import tpu as pltpu

F32 = jnp.float32
BF16 = jnp.bfloat16
HIGHEST = lax.Precision.HIGHEST

EPS = 1e-6
POOL_WINDOWS = (2, 4, 8, 16)
POOL_GDIM = 64
POOL_WIDTH = 256
POOL_HALO = 16
ML_HEADS = 4
ML_DH = 192
ML_WIDTH = ML_HEADS * ML_DH
NSA_HEADS = 12
HEAD_DIM = 64
NSA_WIDTH = NSA_HEADS * HEAD_DIM
KV_HEADS = 2
GQ = NSA_HEADS // KV_HEADS
CMP_STRIDE = 16
SEL_BLOCK = 64
SEL_TOPN = 16
WINDOW = 512
Q_BLOCK = 64
PAGE_SIZE = 128
CONV_CH = 256
CONV_K = 3
CONV_HALO = 8

VMEM_LIMIT = 56 * 1024 * 1024
NEG = -1e30


def _params(sem, vmem=VMEM_LIMIT):
    return pltpu.CompilerParams(dimension_semantics=sem, vmem_limit_bytes=vmem)


def _const_spec(shape):
    nd = len(shape)
    return pl.BlockSpec(shape, lambda *_: (0,) * nd, pipeline_mode=pl.Buffered(1))


def _rms(x, g):
    return x * lax.rsqrt(jnp.mean(x * x, axis=-1, keepdims=True) + EPS) * g


def _dot(a, b):
    return jnp.dot(a, b, preferred_element_type=F32)


def _dot_nt(a, b):
    return lax.dot_general(a, b, (((1,), (1,)), ((), ())), preferred_element_type=F32)


def _ffn_body(x_ref, g_ref, win_ref, wout_ref, *rest, d_ff, chunks, has_final):
    o_ref = rest[-1]
    x = x_ref[...]
    hn = _rms(x, g_ref[...]).astype(BF16)
    acc = jnp.zeros(x.shape, F32)
    off = 0
    for fc in chunks:
        a = _dot(hn, win_ref[:, off:off + fc])
        b = _dot(hn, win_ref[:, d_ff + off:d_ff + off + fc])
        act = (a * jax.nn.sigmoid(a) * b).astype(BF16)
        acc = acc + _dot(act, wout_ref[off:off + fc, :])
        off += fc
    y = x + 0.5 * acc
    if has_final:
        y = _rms(y, rest[0][...])
    o_ref[...] = y


def _ffn(x, g, w_in, w_out, final_g=None):
    n, d = x.shape
    d_ff = w_out.shape[0]
    tm = 512 if n % 512 == 0 else n
    chunks, left = [], d_ff
    while left:
        chunks.append(min(1024, left))
        left -= chunks[-1]
    row = pl.BlockSpec((tm, d), lambda i: (i, 0))
    in_specs = [row, _const_spec((1, d)), _const_spec(w_in.shape), _const_spec(w_out.shape)]
    args = [x, g.reshape(1, d), w_in, w_out]
    if final_g is not None:
        in_specs.append(_const_spec((1, d)))
        args.append(final_g.reshape(1, d))
    return pl.pallas_call(
        functools.partial(_ffn_body, d_ff=d_ff, chunks=tuple(chunks), has_final=final_g is not None),
        grid=(n // tm,), in_specs=in_specs, out_specs=row,
        out_shape=jax.ShapeDtypeStruct((n, d), F32),
        compiler_params=_params(("parallel",)), name="ffn")(*args)


def _even_in_body(x_ref, g_ref, wu_ref, wh_ref, wvt_ref, wg_ref, wgt_ref, bg_ref, bgt_ref,
                  u_ref, q_ref, k_ref, v_ref, og_ref, vt_ref, gc_ref, gr_ref):
    hn = _rms(x_ref[...], g_ref[...]).astype(BF16)
    u_ref[...] = _dot(hn, wu_ref[...])
    for h in range(ML_HEADS):
        q_ref[h] = _dot(hn, wh_ref[h])
        k_ref[h] = _dot(hn, wh_ref[ML_HEADS + h])
        v_ref[h] = _dot(hn, wh_ref[2 * ML_HEADS + h])
        og_ref[h] = _dot(hn, wh_ref[3 * ML_HEADS + h])
        vt_ref[h] = _dot_nt(wvt_ref[h], hn)
    gc_ref[...] = _dot(hn, wg_ref[...]) + bg_ref[...]
    gr_ref[...] = _dot_nt(wgt_ref[...], hn) + bgt_ref[...]


def _even_in(x, g, w_in, b_gate):
    n, d = x.shape
    tm = 512 if n % 512 == 0 else n
    h4 = ML_HEADS
    wu = w_in[:, :POOL_WIDTH].astype(BF16)
    wh = w_in[:, POOL_WIDTH:POOL_WIDTH + 4 * ML_WIDTH].reshape(d, 4 * h4, ML_DH).transpose(1, 0, 2).astype(BF16)
    wvt = wh[2 * h4:3 * h4].transpose(0, 2, 1)
    wg = w_in[:, POOL_WIDTH + 4 * ML_WIDTH:].astype(BF16)
    row = lambda w: pl.BlockSpec((tm, w), lambda i: (i, 0))
    hrow = pl.BlockSpec((h4, tm, ML_DH), lambda i: (0, i, 0))
    hsh = jax.ShapeDtypeStruct((h4, n, ML_DH), F32)
    return pl.pallas_call(
        _even_in_body, grid=(n // tm,),
        in_specs=[row(d), _const_spec((1, d)), _const_spec(wu.shape), _const_spec(wh.shape),
                  _const_spec(wvt.shape), _const_spec(wg.shape), _const_spec((2 * h4, d)),
                  _const_spec((1, 2 * h4)), _const_spec((2 * h4, 1))],
        out_specs=[row(POOL_WIDTH), hrow, hrow, hrow, hrow,
                   pl.BlockSpec((h4, ML_DH, tm), lambda i: (0, 0, i)),
                   row(2 * h4), pl.BlockSpec((2 * h4, tm), lambda i: (0, i))],
        out_shape=[jax.ShapeDtypeStruct((n, POOL_WIDTH), F32), hsh, hsh, hsh, hsh,
                   jax.ShapeDtypeStruct((h4, ML_DH, n), F32),
                   jax.ShapeDtypeStruct((n, 2 * h4), F32), jax.ShapeDtypeStruct((2 * h4, n), F32)],
        compiler_params=_params(("parallel",)), name="even_in")(
            x, g.reshape(1, d), wu, wh, wvt, wg, wg.T, b_gate.reshape(1, -1), b_gate.reshape(-1, 1))


def _pool_body(u_ref, pre_ref, w_ref, sc_ref, y_ref, carry, full, *, tb, pos0):
    t = pl.program_id(1)

    @pl.when(t == 0)
    def _():
        carry[...] = pre_ref[...]

    u = u_ref[...]
    full[0:POOL_HALO] = carry[...]
    full[POOL_HALO:] = u
    acc = full[...]
    sums = []
    for sh in (1, 2, 4, 8):
        acc = acc + pltpu.roll(acc, sh, 0)
        sums.append(acc[POOL_HALO:])
    lane = lax.broadcasted_iota(jnp.int32, (tb, POOL_WIDTH), 1)
    grp = lane // POOL_GDIM
    win = jnp.where(grp == 0, sums[0], jnp.where(grp == 1, sums[1], jnp.where(grp == 2, sums[2], sums[3])))
    width = jnp.where(grp == 0, 2, jnp.where(grp == 1, 4, jnp.where(grp == 2, 8, 16)))
    pos = pos0 + t * tb + lax.broadcasted_iota(jnp.int32, (tb, POOL_WIDTH), 0)
    cnt = jnp.minimum(pos + 1, width).astype(F32)
    mixed = (win / cnt - u).astype(BF16)
    y_ref[...] = _dot(mixed, w_ref[...]) * sc_ref[...]
    carry[...] = full[tb:tb + POOL_HALO]


def _pool(u, prefix, w_pool, scale, pos0):
    b, t, c = u.shape
    tb = 512 if t % 512 == 0 else t
    pre = jnp.pad(prefix, ((0, 0), (POOL_HALO - prefix.shape[1], 0), (0, 0)))
    wbd = jax.scipy.linalg.block_diag(*[w_pool[i] for i in range(w_pool.shape[0])]).astype(BF16)
    return pl.pallas_call(
        functools.partial(_pool_body, tb=tb, pos0=pos0), grid=(b, t // tb),
        in_specs=[pl.BlockSpec((None, tb, c), lambda i, j: (i, j, 0)),
                  pl.BlockSpec((None, POOL_HALO, c), lambda i, j: (i, 0, 0)),
                  _const_spec((c, c)), _const_spec((1, c))],
        out_specs=pl.BlockSpec((None, tb, c), lambda i, j: (i, j, 0)),
        out_shape=jax.ShapeDtypeStruct((b, t, c), F32),
        scratch_shapes=[pltpu.VMEM((POOL_HALO, c), F32), pltpu.VMEM((tb + POOL_HALO, c), F32)],
        compiler_params=_params(("parallel", "arbitrary")), name="pool")(u, pre, wbd, scale.reshape(1, c))


def _log_sigmoid(x):
    return jnp.minimum(x, 0.0) - jnp.log(1.0 + jnp.exp(-jnp.abs(x)))


def _mlstm_body(q_ref, k_ref, v_ref, vt_ref, gc_ref, gr_ref, c0_ref, n0_ref, m0_ref,
                h_ref, c_ref, n_ref, m_ref, *, L, t_valid):
    @pl.when(pl.program_id(1) == 0)
    def _():
        c_ref[...] = c0_ref[...]
        n_ref[...] = n0_ref[...]
        m_ref[...] = m0_ref[...]

    gc = gc_ref[...]
    gr = gr_ref[...]
    lf_c = _log_sigmoid(gc)
    lf_r = _log_sigmoid(gr)
    row = lax.broadcasted_iota(jnp.int32, (L, L), 0)
    col = lax.broadcasted_iota(jnp.int32, (L, L), 1)
    tok_c = lax.broadcasted_iota(jnp.int32, (L, 1), 0) < t_valid
    tok_r = lax.broadcasted_iota(jnp.int32, (1, L), 1) < t_valid
    if t_valid < L:
        lf_c = jnp.where(tok_c, lf_c, 0.0)
        lf_r = jnp.where(tok_r, lf_r, 0.0)
    causal = row >= col
    tri = causal.astype(F32)
    cs_c = jnp.dot(tri, lf_c, precision=HIGHEST, preferred_element_type=F32)
    cs_r = lax.dot_general(lf_r, tri, (((1,), (1,)), ((), ())), precision=HIGHEST,
                           preferred_element_type=F32)
    for hd in range(ML_HEADS):
        i_c = gc[:, hd:hd + 1]
        i_r = gr[hd:hd + 1, :]
        if t_valid < L:
            i_c = jnp.where(tok_c, i_c, -jnp.inf)
            i_r = jnp.where(tok_r, i_r, -jnp.inf)
        b_c = cs_c[:, ML_HEADS + hd:ML_HEADS + hd + 1]
        b_r = cs_r[ML_HEADS + hd:ML_HEADS + hd + 1, :]
        m_prev = m_ref[hd]
        dmat = jnp.where(causal, b_c - b_r + i_r, -jnp.inf)
        inter = b_c + m_prev
        mt = jnp.maximum(jnp.max(dmat, axis=1, keepdims=True), inter)
        w = jnp.exp(dmat - mt)
        a = jnp.exp(inter - mt)
        q = q_ref[hd]
        kf = k_ref[hd] * (ML_DH ** -0.5)
        qb = q.astype(BF16)
        kb = kf.astype(BF16)
        s = _dot_nt(qb, kb) * w
        c_old = c_ref[hd]
        n_old = n_ref[hd]
        num = _dot(s.astype(BF16), v_ref[hd].astype(BF16)) + a * _dot_nt(qb, c_old.astype(BF16))
        den = jnp.sum(s, axis=1, keepdims=True) + a * jnp.sum(q * n_old, axis=1, keepdims=True)
        h_ref[hd] = num / jnp.maximum(jnp.abs(den), jnp.exp(-mt))
        b_last = b_c[L - 1:L, :]
        ge_r = b_last - b_r + i_r
        ge_c = b_last - b_c + i_c
        m_new = jnp.maximum(b_last + m_prev, jnp.max(ge_r, axis=1, keepdims=True))
        wk_r = jnp.exp(ge_r - m_new)
        wk_c = jnp.exp(ge_c - m_new)
        decay = jnp.exp(b_last + m_prev - m_new)
        c_ref[hd] = decay * c_old + _dot((vt_ref[hd] * wk_r).astype(BF16), kb)
        n_ref[hd] = decay * n_old + jnp.sum(kf * wk_c, axis=0, keepdims=True)
        m_ref[hd] = m_new


def _mlstm(q, k, v, vt, gc, gr, c0, n0, m0, *, L, t_valid):
    h4, b, t, dh = q.shape
    nc = t // L
    tok = pl.BlockSpec((h4, None, L, dh), lambda i, c: (0, i, c, 0))
    if vt.ndim == 4:
        vt_spec = pl.BlockSpec((h4, None, dh, L), lambda i, c: (0, i, 0, c))
        gr_spec = pl.BlockSpec((None, 2 * h4, L), lambda i, c: (i, 0, c))
    else:
        vt_spec = pl.BlockSpec((h4, dh, L), lambda i, c: (0, 0, i * nc + c))
        gr_spec = pl.BlockSpec((2 * h4, L), lambda i, c: (0, i * nc + c))
    st = lambda r, w: pl.BlockSpec((None, h4, r, w), lambda i, c: (i, 0, 0, 0))
    return pl.pallas_call(
        functools.partial(_mlstm_body, L=L, t_valid=t_valid), grid=(b, nc),
        in_specs=[tok, tok, tok, vt_spec, pl.BlockSpec((None, L, 2 * h4), lambda i, c: (i, c, 0)), gr_spec,
                  st(dh, dh), st(1, dh), st(1, 1)],
        out_specs=[tok, st(dh, dh), st(1, dh), st(1, 1)],
        out_shape=[jax.ShapeDtypeStruct((h4, b, t, dh), F32), jax.ShapeDtypeStruct((b, h4, dh, dh), F32),
                   jax.ShapeDtypeStruct((b, h4, 1, dh), F32), jax.ShapeDtypeStruct((b, h4, 1, 1), F32)],
        compiler_params=_params(("parallel", "arbitrary")), name="mlstm")(
            q, k, v, vt, gc, gr, c0, n0.reshape(b, h4, 1, dh), m0.reshape(b, h4, 1, 1))


def _even_out_body(x_ref, yp_ref, og_ref, hm_ref, w0_ref, w1_ref, o_ref):
    acc = _dot(yp_ref[...].astype(BF16), w0_ref[...])
    for h in range(ML_HEADS):
        acc = acc + _dot((jax.nn.sigmoid(og_ref[h]) * hm_ref[h]).astype(BF16), w1_ref[h])
    o_ref[...] = x_ref[...] + acc


def _even_out(x, ypool, og, hm, w_out):
    n, d = x.shape
    tm = 512 if n % 512 == 0 else n
    w0 = w_out[:POOL_WIDTH].astype(BF16)
    w1 = w_out[POOL_WIDTH:].reshape(ML_HEADS, ML_DH, d).astype(BF16)
    row = lambda w: pl.BlockSpec((tm, w), lambda i: (i, 0))
    hrow = pl.BlockSpec((ML_HEADS, tm, ML_DH), lambda i: (0, i, 0))
    return pl.pallas_call(
        _even_out_body, grid=(n // tm,),
        in_specs=[row(d), row(POOL_WIDTH), hrow, hrow, _const_spec(w0.shape), _const_spec(w1.shape)],
        out_specs=row(d), out_shape=jax.ShapeDtypeStruct((n, d), F32),
        compiler_params=_params(("parallel",)), name="even_out")(x, ypool, og, hm, w0, w1)


def _odd_in_body(x_ref, g_ref, wq_ref, wkvt_ref, wgt_ref, wc_ref,
                 q_ref, rows_ref, win_ref, gates_ref, ucv_ref, bg_ref):
    hn = _rms(x_ref[...], g_ref[...]).astype(BF16)
    for h in range(NSA_HEADS):
        q_ref[h] = _dot(hn, wq_ref[h]) * (HEAD_DIM ** -0.5)
    nrow = rows_ref.shape[0]
    rows_ref[...] = _dot_nt(wkvt_ref[:nrow], hn)
    win_ref[...] = _dot_nt(wkvt_ref[nrow:], hn)
    gates_ref[...] = jax.nn.sigmoid(_dot(hn, wgt_ref[...]))
    bg_ref[...] = _dot(hn, wc_ref[:, :CONV_CH])
    ucv_ref[...] = _dot(hn, wc_ref[:, CONV_CH:2 * CONV_CH]) * _dot(hn, wc_ref[:, 2 * CONV_CH:])


def _odd_in(x, g, w_in, seq):
    n, d = x.shape
    tm = 512 if seq % 512 == 0 else seq
    nt = seq // tm
    kvw = 6 * KV_HEADS * HEAD_DIM
    ngt = 3 * NSA_HEADS
    wq = w_in[:, :NSA_WIDTH].reshape(d, NSA_HEADS, HEAD_DIM).transpose(1, 0, 2).astype(BF16)
    wkvt = w_in[:, NSA_WIDTH:NSA_WIDTH + kvw].T.astype(BF16)
    wgt = w_in[:, NSA_WIDTH + kvw:NSA_WIDTH + kvw + ngt].astype(BF16)
    wc = w_in[:, NSA_WIDTH + kvw + ngt:].astype(BF16)
    nrow = 4 * KV_HEADS * HEAD_DIM
    row = lambda w: pl.BlockSpec((tm, w), lambda i: (i, 0))
    slab = lambda r: pl.BlockSpec((None, r, tm), lambda i: (i // nt, 0, i % nt))
    sh = lambda w: jax.ShapeDtypeStruct((n, w), F32)
    return pl.pallas_call(
        _odd_in_body, grid=(n // tm,),
        in_specs=[row(d), _const_spec((1, d)), _const_spec(wq.shape), _const_spec(wkvt.shape),
                  _const_spec(wgt.shape), _const_spec(wc.shape)],
        out_specs=[pl.BlockSpec((NSA_HEADS, tm, HEAD_DIM), lambda i: (0, i, 0)),
                   slab(nrow), slab(kvw - nrow), row(ngt), row(CONV_CH), row(CONV_CH)],
        out_shape=[jax.ShapeDtypeStruct((NSA_HEADS, n, HEAD_DIM), F32),
                   jax.ShapeDtypeStruct((n // seq, nrow, seq), F32),
                   jax.ShapeDtypeStruct((n // seq, kvw - nrow, seq), F32),
                   sh(ngt), sh(CONV_CH), sh(CONV_CH)],
        compiler_params=_params(("parallel",)), name="odd_in")(x, g.reshape(1, d), wq, wkvt, wgt, wc)


def _conv_body(u_ref, bg_ref, pre_ref, w_ref, y_ref, carry, full, *, tb):
    @pl.when(pl.program_id(1) == 0)
    def _():
        carry[...] = pre_ref[...]

    full[0:CONV_HALO] = carry[...]
    full[CONV_HALO:] = u_ref[...]
    f = full[...]
    w = w_ref[...]
    conv = f * w[2:3] + pltpu.roll(f, 1, 0) * w[1:2] + pltpu.roll(f, 2, 0) * w[0:1]
    y_ref[...] = bg_ref[...] * conv[CONV_HALO:]
    carry[...] = full[tb:tb + CONV_HALO]


def _conv(u, bg, prefix, conv_w):
    b, t, c = u.shape
    tb = 512 if t % 512 == 0 else t
    pre = jnp.pad(prefix, ((0, 0), (CONV_HALO - prefix.shape[1], 0), (0, 0)))
    blk = pl.BlockSpec((None, tb, c), lambda i, j: (i, j, 0))
    return pl.pallas_call(
        functools.partial(_conv_body, tb=tb), grid=(b, t // tb),
        in_specs=[blk, blk, pl.BlockSpec((None, CONV_HALO, c), lambda i, j: (i, 0, 0)), _const_spec((CONV_K, c))],
        out_specs=blk, out_shape=jax.ShapeDtypeStruct((b, t, c), F32),
        scratch_shapes=[pltpu.VMEM((CONV_HALO, c), F32), pltpu.VMEM((tb + CONV_HALO, c), F32)],
        compiler_params=_params(("parallel", "arbitrary")), name="conv")(u, bg, pre, conv_w)


def _odd_out_body(x_ref, o_ref, yc_ref, w0_ref, w1_ref, out_ref):
    out_ref[...] = (x_ref[...] + _dot(o_ref[...].astype(BF16), w0_ref[...])
                    + _dot(yc_ref[...].astype(BF16), w1_ref[...]))


def _odd_out(x, o, yconv, w_out):
    n, d = x.shape
    tm = 512 if n % 512 == 0 else n
    w0 = w_out[:NSA_WIDTH].astype(BF16)
    w1 = w_out[NSA_WIDTH:].astype(BF16)
    row = lambda w: pl.BlockSpec((tm, w), lambda i: (i, 0))
    return pl.pallas_call(
        _odd_out_body, grid=(n // tm,),
        in_specs=[row(d), row(NSA_WIDTH), row(CONV_CH), _const_spec(w0.shape), _const_spec(w1.shape)],
        out_specs=row(d), out_shape=jax.ShapeDtypeStruct((n, d), F32),
        compiler_params=_params(("parallel",)), name="odd_out")(x, o, yconv, w0, w1)


CMP_TILE = 2048
CMP_ROWS = 4 * HEAD_DIM


def _cmp_weights(cmp_pos_w, cmp_w, cmp_b, tile):
    m = tile // CMP_STRIDE
    cols = -(-(m + 1) // 128) * 128
    pos = jnp.arange(tile)
    chunk = (pos // CMP_STRIDE)[:, None]
    col = jnp.arange(cols)[None, :]
    mats = []
    for kv in range(2):
        wa = cmp_pos_w[kv, :CMP_STRIDE][pos % CMP_STRIDE][:, None]
        wb = cmp_pos_w[kv, CMP_STRIDE:][pos % CMP_STRIDE][:, None]
        mats.append(jnp.where(chunk == col, wb, 0.0) + jnp.where(chunk == col - 1, wa, 0.0))
    w2 = jnp.stack(mats).astype(BF16)
    wbdt = jax.scipy.linalg.block_diag(cmp_w[0], cmp_w[0], cmp_w[1], cmp_w[1]).T.astype(BF16)
    biast = jnp.concatenate([cmp_b[0], cmp_b[0], cmp_b[1], cmp_b[1]]).reshape(-1, 1)
    return w2, wbdt, biast


def _compress_body(*refs, n_pages, n_prefetch):
    refs = refs[n_prefetch:]
    pages = refs[:n_pages]
    w2_ref, wbdt_ref, biast_ref, out_ref, carry = refs[n_pages:]
    m_out = out_ref.shape[1]

    @pl.when(pl.program_id(1) == 0)
    def _():
        carry[...] = jnp.zeros(carry.shape, F32)

    x = jnp.concatenate([pg[...] for pg in pages], axis=1).astype(BF16)
    half = CMP_ROWS // 2
    res = jnp.concatenate([_dot(x[:half], w2_ref[0]), _dot(x[half:], w2_ref[1])], axis=0)
    first = lax.broadcasted_iota(jnp.int32, (CMP_ROWS, m_out), 1) == 0
    pre = res[:, :m_out] + jnp.where(first, carry[...], 0.0)
    carry[...] = jnp.broadcast_to(res[:, m_out:m_out + 1], carry.shape)
    out_ref[...] = _dot(wbdt_ref[...], pre.astype(BF16)) + biast_ref[...]


def _compress_prompt(rows_t, cmp_pos_w, cmp_w, cmp_b):
    b, _, t = rows_t.shape
    tile = min(CMP_TILE, t)
    assert t % tile == 0
    w2, wbdt, biast = _cmp_weights(cmp_pos_w, cmp_w, cmp_b, tile)
    m_out = tile // CMP_STRIDE
    return pl.pallas_call(
        functools.partial(_compress_body, n_pages=1, n_prefetch=0), grid=(b, t // tile),
        in_specs=[pl.BlockSpec((None, CMP_ROWS, tile), lambda i, j: (i, 0, j)),
                  _const_spec(w2.shape), _const_spec(wbdt.shape), _const_spec(biast.shape)],
        out_specs=pl.BlockSpec((None, CMP_ROWS, m_out), lambda i, j: (i, 0, j)),
        out_shape=jax.ShapeDtypeStruct((b, CMP_ROWS, t // CMP_STRIDE), F32),
        scratch_shapes=[pltpu.VMEM((CMP_ROWS, m_out), F32)],
        compiler_params=_params(("parallel", "arbitrary")), name="compress_prompt")(rows_t, w2, wbdt, biast)


def _compress_paged(cache_t, page_table, cmp_pos_w, cmp_w, cmp_b):
    b, npg = page_table.shape
    pg = min(CMP_TILE // PAGE_SIZE, npg)
    assert npg % pg == 0
    tile = pg * PAGE_SIZE
    w2, wbdt, biast = _cmp_weights(cmp_pos_w, cmp_w, cmp_b, tile)
    m_out = tile // CMP_STRIDE
    specs = [pl.BlockSpec((None, CMP_ROWS, PAGE_SIZE),
                          functools.partial(lambda i, j, pt, p: (pt[i, j * pg + p], 0, 0), p=p))
             for p in range(pg)]
    return pl.pallas_call(
        functools.partial(_compress_body, n_pages=pg, n_prefetch=1),
        grid_spec=pltpu.PrefetchScalarGridSpec(
            num_scalar_prefetch=1, grid=(b, npg // pg),
            in_specs=specs + [_const_spec(w2.shape), _const_spec(wbdt.shape), _const_spec(biast.shape)],
            out_specs=pl.BlockSpec((None, CMP_ROWS, m_out), lambda i, j, pt: (i, 0, j)),
            scratch_shapes=[pltpu.VMEM((CMP_ROWS, m_out), F32)]),
        out_shape=jax.ShapeDtypeStruct((b, CMP_ROWS, npg * PAGE_SIZE // CMP_STRIDE), F32),
        compiler_params=_params(("parallel", "arbitrary")), name="compress_paged")(
            page_table, *([cache_t] * pg), w2, wbdt, biast)


def _alibi_slopes(rows_per_head):
    sl = (2.0 ** (-8.0 * np.arange(1, NSA_HEADS + 1) / NSA_HEADS)).astype(np.float32).reshape(KV_HEADS, GQ)
    return jnp.asarray(np.repeat(sl, rows_per_head, axis=1)[:, :, None])


def _score_matrix(n_cmp_rows, n_blocks, lanes):
    m = np.arange(n_cmp_rows)[:, None]
    j = np.arange(lanes)[None, :]
    return jnp.asarray(((m >= 4 * j) & (m <= 4 * j + 4) & (m >= 1) & (j < n_blocks)).astype(np.float32))


def _softmax_rows(s, mask):
    s = jnp.where(mask, s, -jnp.inf)
    m = jnp.max(s, axis=-1, keepdims=True)
    m = jnp.where(m > -jnp.inf, m, 0.0)
    e = jnp.exp(s - m)
    return e / jnp.maximum(jnp.sum(e, axis=-1, keepdims=True), 1e-30)


def _select_blocks(score, cur, n_blocks):
    r, lanes = score.shape
    blk = lax.broadcasted_iota(jnp.int32, (r, lanes), 1)
    forced = (blk == 0) | (blk == cur) | (blk == cur - 1)
    val = jnp.where(forced, jnp.inf, jnp.where(blk <= cur, score, -jnp.inf))
    rank = jnp.zeros((r, lanes), jnp.int32)
    for i in range(n_blocks):
        ci = val[:, i:i + 1]
        ahead = (ci > val) | ((ci == val) & (blk > i))
        rank = rank + ahead.astype(jnp.int32)
    return ((rank < min(SEL_TOPN, n_blocks)) & (blk < n_blocks)).astype(F32)


def _expand_blocks(selmask, first_block, n_keys, reps):
    lanes = selmask.shape[1]
    b = lax.broadcasted_iota(jnp.int32, (lanes, n_keys), 0)
    k = lax.broadcasted_iota(jnp.int32, (lanes, n_keys), 1)
    onehot = jnp.where(b == first_block + k // SEL_BLOCK, 1.0, 0.0).astype(BF16)
    m = _dot(selmask.astype(BF16), onehot)
    return jnp.concatenate([m] * reps, axis=0) > 0.5


POS_ROWS = 16


def _slope_pieces(rows_per_head):
    sl = _alibi_slopes(rows_per_head)
    hi = sl.astype(BF16)
    mid = (sl - hi.astype(F32)).astype(BF16)
    lo = (sl - hi.astype(F32) - mid.astype(F32)).astype(BF16)
    pad = jnp.zeros(sl.shape[:2] + (POS_ROWS - 6,), BF16)
    return jnp.concatenate([hi, mid, lo, hi, mid, lo, pad], axis=2)


def _key_constants(t_len):
    k = np.arange(t_len)
    hi = (SEL_BLOCK * (k // SEL_BLOCK)).astype(np.float32)
    lo = (k % SEL_BLOCK).astype(np.float32)
    pos = np.stack([hi, hi, hi, lo, lo, lo] + [np.zeros(t_len, np.float32)] * (POS_ROWS - 6))
    onehot = (np.arange(t_len // SEL_BLOCK)[:, None] == (k // SEL_BLOCK)[None, :]).astype(np.float32)
    return jnp.asarray(np.concatenate([pos, onehot], axis=0)).astype(BF16)


def _select_bias_t(score_t, cur):
    n_blocks, r = score_t.shape
    blk = lax.broadcasted_iota(jnp.int32, (n_blocks, r), 0)
    forced = (blk == 0) | (blk == cur) | (blk == cur - 1)
    val = jnp.where(forced, jnp.inf, jnp.where(blk <= cur, score_t, -jnp.inf))
    rank = jnp.zeros((n_blocks, r), F32)
    for i in range(n_blocks):
        ci = val[i:i + 1, :]
        ahead = (ci > val) | ((ci == val) & (blk > i))
        rank = rank + jnp.where(ahead, 1.0, 0.0)
    return jnp.where((rank < min(SEL_TOPN, n_blocks)) & (blk <= cur), 0.0, NEG)


def _nsa_prompt_body(q_ref, gates_ref, kvc_ref, sel_ref, win_ref, slope_ref, saug_ref, kconst_ref,
                     smat_ref, o_ref, *, t_len, kc, wb):
    bi = pl.program_id(1)
    p0 = bi * Q_BLOCK
    nrow = GQ * Q_BLOCK
    n_cmp = kvc_ref.shape[1]
    n_blocks = t_len // SEL_BLOCK
    qpos = lax.broadcasted_iota(jnp.int32, (Q_BLOCK, 1), 0) + p0
    tpos = jnp.concatenate([qpos] * GQ, axis=0)
    gates = gates_ref[...]

    qgs = [q_ref[g * GQ:(g + 1) * GQ].reshape(nrow, HEAD_DIM).astype(BF16) for g in range(KV_HEADS)]
    q_pos = [jnp.concatenate([qgs[g], saug_ref[g]], axis=1) for g in range(KV_HEADS)]

    wstart = pl.multiple_of(jnp.clip((p0 - WINDOW) // 128 * 128, 0, t_len - wb), 128)
    wpos = wstart + lax.broadcasted_iota(jnp.int32, (1, wb), 1)
    wmask = (wpos <= tpos) & (wpos > tpos - WINDOW)
    o_wins = []
    for g in range(KV_HEADS):
        vs = (KV_HEADS + g) * HEAD_DIM
        katw = jnp.concatenate([win_ref[g * HEAD_DIM:(g + 1) * HEAD_DIM, pl.ds(wstart, wb)].astype(BF16),
                                kconst_ref[0:POS_ROWS, pl.ds(wstart, wb)]], axis=0)
        sw = jnp.where(wmask, _dot(q_pos[g], katw), -jnp.inf)
        e_win = jnp.exp(sw - jnp.max(sw, axis=1, keepdims=True))
        vwt = jnp.concatenate([win_ref[vs:vs + HEAD_DIM, pl.ds(wstart, wb)].astype(BF16),
                               jnp.ones((POS_ROWS, wb), BF16)], axis=0)
        acc_w = _dot_nt(e_win.astype(BF16), vwt)
        o_wins.append(acc_w[:, :HEAD_DIM] / acc_w[:, HEAD_DIM:HEAD_DIM + 1])

    cpos = CMP_STRIDE * lax.broadcasted_iota(jnp.int32, (1, n_cmp), 1) + (CMP_STRIDE - 1)
    dist = tpos - cpos
    cmask = (dist >= 0) & (cpos >= 2 * CMP_STRIDE - 1)
    o_cmps, imps = [], []
    for g in range(KV_HEADS):
        ks, vs = g * HEAD_DIM, (KV_HEADS + g) * HEAD_DIM
        s = _dot(qgs[g], kvc_ref[ks:ks + HEAD_DIM, :].astype(BF16)) - slope_ref[g] * dist.astype(F32)
        p_cmp = _softmax_rows(s, cmask)
        o_cmps.append(_dot_nt(p_cmp.astype(BF16), kvc_ref[vs:vs + HEAD_DIM, :].astype(BF16)))
        imps.append(jnp.sum(p_cmp.reshape(GQ, Q_BLOCK, n_cmp), axis=0))
        for i in range(GQ):
            hh = g * GQ + i
            r = slice(i * Q_BLOCK, (i + 1) * Q_BLOCK)
            o_ref[:, hh * HEAD_DIM:(hh + 1) * HEAD_DIM] = (
                gates[:, 3 * hh:3 * hh + 1] * o_cmps[g][r] + gates[:, 3 * hh + 2:3 * hh + 3] * o_wins[g][r])
    score_t = lax.dot_general(smat_ref[...], jnp.concatenate(imps, axis=0), (((1,), (1,)), ((), ())),
                              precision=HIGHEST, preferred_element_type=F32)
    bias_t = _select_bias_t(score_t, bi).astype(BF16)
    eye = (lax.broadcasted_iota(jnp.int32, (Q_BLOCK, Q_BLOCK), 0)
           == lax.broadcasted_iota(jnp.int32, (Q_BLOCK, Q_BLOCK), 1)).astype(BF16)
    biases = [_dot_nt(eye, bias_t[:, g * Q_BLOCK:(g + 1) * Q_BLOCK]) for g in range(KV_HEADS)]
    q_aug = [jnp.concatenate([q_pos[g], jnp.concatenate([biases[g]] * GQ, axis=0).astype(BF16)], axis=1)
             for g in range(KV_HEADS)]

    ones_rows = jnp.ones((POS_ROWS, kc), BF16)

    def chunk_scores(g, c):
        k0 = pl.multiple_of(c * kc, kc)
        kat = jnp.concatenate([sel_ref[g * HEAD_DIM:(g + 1) * HEAD_DIM, pl.ds(k0, kc)].astype(BF16),
                               kconst_ref[:, pl.ds(k0, kc)]], axis=0)
        return _dot(q_aug[g], kat), k0

    def accumulate(g, s, k0, carry):
        m_i, acc = carry
        vs = (KV_HEADS + g) * HEAD_DIM
        m_new = jnp.maximum(m_i, jnp.max(s, axis=1, keepdims=True))
        pr = jnp.exp(s - m_new)
        alpha = jnp.exp(m_i - m_new)
        vt = jnp.concatenate([sel_ref[vs:vs + HEAD_DIM, pl.ds(k0, kc)].astype(BF16), ones_rows], axis=0)
        return m_new, alpha * acc + _dot_nt(pr.astype(BF16), vt)

    def sel_step(c, carry):
        out = []
        for g in range(KV_HEADS):
            s, k0 = chunk_scores(g, c)
            out.append(accumulate(g, s, k0, carry[g]))
        return tuple(out)

    one = (jnp.full((nrow, 1), NEG, F32), jnp.zeros((nrow, HEAD_DIM + POS_ROWS), F32))
    last = (p0 + Q_BLOCK - 1) // kc
    carry = lax.fori_loop(0, last, sel_step, (one,) * KV_HEADS)
    for g in range(KV_HEADS):
        s, k0 = chunk_scores(g, last)
        kpos = k0 + lax.broadcasted_iota(jnp.int32, (1, kc), 1)
        _, acc_s = accumulate(g, jnp.where(kpos <= tpos, s, NEG), k0, carry[g])
        o_sel = acc_s[:, :HEAD_DIM] / jnp.maximum(acc_s[:, HEAD_DIM:HEAD_DIM + 1], 1e-30)
        for i in range(GQ):
            hh = g * GQ + i
            r = slice(i * Q_BLOCK, (i + 1) * Q_BLOCK)
            o_ref[:, hh * HEAD_DIM:(hh + 1) * HEAD_DIM] += gates[:, 3 * hh + 1:3 * hh + 2] * o_sel[r]


def _nsa_prompt(q, gates, kvc_t, rows_t, win_t):
    _, b, t, _ = q.shape
    assert t % 128 == 0
    kc = 512 if t % 512 == 0 else t
    wb = min(WINDOW + 128, t)
    n_cmp = kvc_t.shape[2]
    n_blocks = t // SEL_BLOCK
    ncol = 4 * HEAD_DIM
    slopes = _alibi_slopes(Q_BLOCK)
    saug = _slope_pieces(Q_BLOCK)
    kconst = _key_constants(t)
    smat = _score_matrix(n_cmp, n_blocks, n_blocks).T
    return pl.pallas_call(
        functools.partial(_nsa_prompt_body, t_len=t, kc=kc, wb=wb), grid=(b, t // Q_BLOCK),
        in_specs=[pl.BlockSpec((NSA_HEADS, None, Q_BLOCK, HEAD_DIM), lambda i, j: (0, i, j, 0)),
                  pl.BlockSpec((None, Q_BLOCK, 3 * NSA_HEADS), lambda i, j: (i, j, 0)),
                  pl.BlockSpec((None, ncol, n_cmp), lambda i, j: (i, 0, 0)),
                  pl.BlockSpec((None, ncol, t), lambda i, j: (i, 1, 0)),
                  pl.BlockSpec((None, ncol, t), lambda i, j: (i, 0, 0)),
                  _const_spec(slopes.shape), _const_spec(saug.shape), _const_spec(kconst.shape),
                  _const_spec(smat.shape)],
        out_specs=pl.BlockSpec((None, Q_BLOCK, NSA_WIDTH), lambda i, j: (i, j, 0)),
        out_shape=jax.ShapeDtypeStruct((b, t, NSA_WIDTH), F32),
        compiler_params=_params(("parallel", "arbitrary")), name="nsa_prompt")(
            q, gates, kvc_t, rows_t, win_t, slopes, saug, kconst, smat)


def _nsa_dec_select_body(q_ref, kvc_ref, slope_ref, smat_ref, ocmp_ref, sel_ref, *, past_len, tq, n_blocks):
    n_cmp = kvc_ref.shape[1]
    nrow = GQ * tq
    qpos = lax.broadcasted_iota(jnp.int32, (tq, 1), 0) + past_len
    tpos = jnp.concatenate([qpos] * GQ, axis=0)
    cpos = CMP_STRIDE * lax.broadcasted_iota(jnp.int32, (1, n_cmp), 1) + (CMP_STRIDE - 1)
    dist = tpos - cpos
    for g in range(KV_HEADS):
        qg = q_ref[g * GQ:(g + 1) * GQ].reshape(nrow, HEAD_DIM).astype(BF16)
        ks, vs = g * HEAD_DIM, (KV_HEADS + g) * HEAD_DIM
        s = _dot(qg, kvc_ref[ks:ks + HEAD_DIM, :].astype(BF16)) - slope_ref[g] * dist.astype(F32)
        p_cmp = _softmax_rows(s, (dist >= 0) & (cpos >= 2 * CMP_STRIDE - 1))
        ocmp_ref[g] = _dot_nt(p_cmp.astype(BF16), kvc_ref[vs:vs + HEAD_DIM, :].astype(BF16))
        imp = jnp.sum(p_cmp.reshape(GQ, tq, n_cmp), axis=0)
        score = jnp.dot(imp, smat_ref[...], precision=HIGHEST, preferred_element_type=F32)
        sel_ref[g] = _select_blocks(score, qpos // SEL_BLOCK, n_blocks)


def _nsa_dec_select(q, kvc, past_len, n_blocks):
    _, b, tq, _ = q.shape
    n_cmp = kvc.shape[2]
    lanes = -(-n_blocks // 128) * 128
    slopes = _alibi_slopes(tq)
    smat = _score_matrix(n_cmp, n_blocks, lanes)
    return pl.pallas_call(
        functools.partial(_nsa_dec_select_body, past_len=past_len, tq=tq, n_blocks=n_blocks), grid=(b,),
        in_specs=[pl.BlockSpec((NSA_HEADS, None, tq, HEAD_DIM), lambda i: (0, i, 0, 0)),
                  pl.BlockSpec((None, 4 * HEAD_DIM, n_cmp), lambda i: (i, 0, 0)),
                  _const_spec(slopes.shape), _const_spec(smat.shape)],
        out_specs=[pl.BlockSpec((None, KV_HEADS, GQ * tq, HEAD_DIM), lambda i: (i, 0, 0, 0)),
                   pl.BlockSpec((None, KV_HEADS, tq, lanes), lambda i: (i, 0, 0, 0))],
        out_shape=[jax.ShapeDtypeStruct((b, KV_HEADS, GQ * tq, HEAD_DIM), F32),
                   jax.ShapeDtypeStruct((b, KV_HEADS, tq, lanes), F32)],
        compiler_params=_params(("parallel",)), name="nsa_dec_select")(q, kvc, slopes, smat)


def _nsa_dec_attend_body(pt_ref, q_ref, selm_ref, ocmp_ref, gates_ref, newsel_ref, winbuf_ref, newwin_ref,
                         slope_ref, *rest, past_len, tq, t_valid, n_pages):
    pages = rest[:n_pages]
    o_ref, m_sc, l_sc, acc_sc = rest[n_pages:]
    step = pl.program_id(1)
    nrow = GQ * tq
    qpos = lax.broadcasted_iota(jnp.int32, (tq, 1), 0) + past_len
    tpos = jnp.concatenate([qpos] * GQ, axis=0)

    @pl.when(step == 0)
    def _():
        m_sc[...] = jnp.full(m_sc.shape, NEG, F32)
        l_sc[...] = jnp.zeros(l_sc.shape, F32)
        acc_sc[...] = jnp.zeros(acc_sc.shape, F32)

    def update(g, sc, ok, weighted_values):
        sc = jnp.where(ok, sc, NEG)
        m_i = m_sc[g]
        m_new = jnp.maximum(m_i, jnp.max(sc, axis=1, keepdims=True))
        pr = jnp.where(ok, jnp.exp(sc - m_new), 0.0)
        alpha = jnp.exp(m_i - m_new)
        l_sc[g] = alpha * l_sc[g] + jnp.sum(pr, axis=1, keepdims=True)
        acc_sc[g] = alpha * acc_sc[g] + weighted_values(pr.astype(BF16))
        m_sc[g] = m_new

    nk = n_pages * PAGE_SIZE
    k0 = step * nk
    kpos = k0 + lax.broadcasted_iota(jnp.int32, (1, nk), 1)
    for g in range(KV_HEADS):
        qg = q_ref[g * GQ:(g + 1) * GQ].reshape(nrow, HEAD_DIM).astype(BF16)
        ks, vs = g * HEAD_DIM, (KV_HEADS + g) * HEAD_DIM
        kt = jnp.concatenate([pg[ks:ks + HEAD_DIM, :] for pg in pages], axis=1).astype(BF16)
        vt = jnp.concatenate([pg[vs:vs + HEAD_DIM, :] for pg in pages], axis=1).astype(BF16)
        d = tpos - kpos
        sc = _dot(qg, kt) - slope_ref[g] * d.astype(F32)
        ok = _expand_blocks(selm_ref[g], step * (nk // SEL_BLOCK), nk, GQ) & (d >= 0)
        update(g, sc, ok, lambda p, vt=vt: _dot_nt(p, vt))

    @pl.when(step == pl.num_programs(1) - 1)
    def _():
        gates = gates_ref[...]
        tn = newsel_ref.shape[0]
        npos = past_len + lax.broadcasted_iota(jnp.int32, (1, tn), 1)
        nd = tpos - npos
        valid_new = (nd >= 0) & (npos < past_len + t_valid)
        wlen = winbuf_ref.shape[1]
        wpos = past_len - wlen + lax.broadcasted_iota(jnp.int32, (1, wlen), 1)
        wd = tpos - wpos
        for g in range(KV_HEADS):
            qg = q_ref[g * GQ:(g + 1) * GQ].reshape(nrow, HEAD_DIM).astype(BF16)
            ks, vs = g * HEAD_DIM, (KV_HEADS + g) * HEAD_DIM
            slope = slope_ref[g]
            sc = _dot_nt(qg, newsel_ref[:, ks:ks + HEAD_DIM].astype(BF16)) - slope * nd.astype(F32)
            last_blk = past_len // SEL_BLOCK
            picked = selm_ref[g][:, last_blk:last_blk + 1] > 0.5
            ok = jnp.concatenate([picked] * GQ, axis=0) & valid_new
            new_v = newsel_ref[:, vs:vs + HEAD_DIM].astype(BF16)
            update(g, sc, ok, lambda p, new_v=new_v: _dot(p, new_v))
            o_sel = acc_sc[g] / jnp.maximum(l_sc[g], 1e-30)
            s1 = _dot(qg, winbuf_ref[ks:ks + HEAD_DIM, :].astype(BF16)) - slope * wd.astype(F32)
            s2 = _dot_nt(qg, newwin_ref[:, ks:ks + HEAD_DIM].astype(BF16)) - slope * nd.astype(F32)
            ok1 = (wd >= 0) & (wd < WINDOW) & (wpos >= 0)
            ok2 = valid_new & (nd < WINDOW)
            s1 = jnp.where(ok1, s1, -jnp.inf)
            s2 = jnp.where(ok2, s2, -jnp.inf)
            mx = jnp.maximum(jnp.max(s1, axis=1, keepdims=True), jnp.max(s2, axis=1, keepdims=True))
            mx = jnp.where(mx > -jnp.inf, mx, 0.0)
            e1 = jnp.exp(s1 - mx)
            e2 = jnp.exp(s2 - mx)
            den = jnp.maximum(jnp.sum(e1, axis=1, keepdims=True) + jnp.sum(e2, axis=1, keepdims=True), 1e-30)
            o_win = (_dot_nt(e1.astype(BF16), winbuf_ref[vs:vs + HEAD_DIM, :].astype(BF16))
                     + _dot(e2.astype(BF16), newwin_ref[:, vs:vs + HEAD_DIM].astype(BF16))) / den
            o_cmp = ocmp_ref[g]
            for i in range(GQ):
                hh = g * GQ + i
                r = slice(i * tq, (i + 1) * tq)
                o_ref[:, hh * HEAD_DIM:(hh + 1) * HEAD_DIM] = (
                    gates[:, 3 * hh:3 * hh + 1] * o_cmp[r] + gates[:, 3 * hh + 1:3 * hh + 2] * o_sel[r]
                    + gates[:, 3 * hh + 2:3 * hh + 3] * o_win[r])


def _nsa_dec_attend(q, selmask, o_cmp, gates, newsel, winbuf, newwin, cache, page_table, past_len, t_valid):
    _, b, tq, _ = q.shape
    npg = page_table.shape[1]
    pg = min(16, npg)
    assert npg % pg == 0
    ncol = 4 * HEAD_DIM
    lanes = selmask.shape[-1]
    slopes = _alibi_slopes(tq)
    wlen = winbuf.shape[2]
    bspec = lambda shape: pl.BlockSpec((None,) + shape, lambda i, j, pt: (i,) + (0,) * len(shape))
    page_specs = [pl.BlockSpec((None, ncol, PAGE_SIZE),
                               functools.partial(lambda i, j, pt, p: (pt[i, j * pg + p], 1, 0), p=p))
                  for p in range(pg)]
    return pl.pallas_call(
        functools.partial(_nsa_dec_attend_body, past_len=past_len, tq=tq, t_valid=t_valid, n_pages=pg),
        grid_spec=pltpu.PrefetchScalarGridSpec(
            num_scalar_prefetch=1, grid=(b, npg // pg),
            in_specs=[pl.BlockSpec((NSA_HEADS, None, tq, HEAD_DIM), lambda i, j, pt: (0, i, 0, 0)),
                      bspec((KV_HEADS, tq, lanes)), bspec((KV_HEADS, GQ * tq, HEAD_DIM)),
                      bspec((tq, 3 * NSA_HEADS)), bspec((tq, ncol)), bspec((ncol, wlen)), bspec((tq, ncol)),
                      pl.BlockSpec(slopes.shape, lambda i, j, pt: (0, 0, 0))] + page_specs,
            out_specs=bspec((tq, NSA_WIDTH)),
            scratch_shapes=[pltpu.VMEM((KV_HEADS, GQ * tq, 1), F32), pltpu.VMEM((KV_HEADS, GQ * tq, 1), F32),
                            pltpu.VMEM((KV_HEADS, GQ * tq, HEAD_DIM), F32)]),
        out_shape=jax.ShapeDtypeStruct((b, tq, NSA_WIDTH), F32),
        compiler_params=_params(("parallel", "arbitrary")), name="nsa_dec_attend")(
            page_table, q, selmask, o_cmp, gates, newsel, winbuf, newwin, slopes, *([cache] * pg))


def _pad_t(a, axis, to):
    pad = [(0, 0)] * a.ndim
    pad[axis] = (0, to - a.shape[axis])
    return jnp.pad(a, pad)


def _even_layer(x, b, t, pos0, pool_buf, c0, n0, m0, g, w_in, b_gate, w_pool, pool_scale, w_out):
    n = b * t
    u, q, k, v, og, vt, gc, gr = _even_in(x, g, w_in, b_gate)
    u3 = u.reshape(b, t, POOL_WIDTH)
    new_pool = jnp.concatenate([pool_buf, u3], axis=1)[:, -pool_buf.shape[1]:]
    h4 = ML_HEADS
    if t % 256 == 0:
        tp, L = t, 256
        vt_in, gr_in = vt, gr
    else:
        tp = L = -(-t // 8) * 8
        vt_in = _pad_t(vt.reshape(h4, ML_DH, b, t), 3, tp).transpose(0, 2, 1, 3)
        gr_in = _pad_t(gr.reshape(2 * h4, b, t), 2, tp).transpose(1, 0, 2)
    seq = lambda a: _pad_t(a.reshape(h4, b, t, ML_DH), 2, tp)
    ypool = _pool(_pad_t(u3, 1, tp), pool_buf, w_pool, pool_scale, pos0)[:, :t].reshape(n, POOL_WIDTH)
    hm, c, nn, m = _mlstm(seq(q), seq(k), seq(v), vt_in, _pad_t(gc.reshape(b, t, 2 * h4), 1, tp), gr_in,
                          c0, n0, m0, L=L, t_valid=t)
    hm = hm[:, :, :t].reshape(h4, n, ML_DH)
    x = _even_out(x, ypool, og, hm, w_out)
    return x, new_pool, c, nn.reshape(b, h4, ML_DH), m.reshape(b, h4)


def _odd_layer(x, b, t, pos0, cache, page_table, win_buf, conv_buf, g, w_in, cmp_pos_w, cmp_w, cmp_b,
               conv_w, w_out):
    n = b * t
    ncol = 4 * HEAD_DIM
    prompt = cache is None
    q, rows_t, win_t, gates, ucv, bg = _odd_in(x, g, w_in, t if prompt else n)
    u3 = ucv.reshape(b, t, CONV_CH)
    new_conv = jnp.concatenate([conv_buf, u3], axis=1)[:, -(CONV_K - 1):]
    if prompt:
        kvc_t = _compress_prompt(rows_t, cmp_pos_w, cmp_w, cmp_b)
        o = _nsa_prompt(q.reshape(NSA_HEADS, b, t, HEAD_DIM), gates.reshape(b, t, 3 * NSA_HEADS), kvc_t,
                        rows_t, win_t)
        new_win_t = win_t[:, :, t - min(WINDOW, t):]
        new_rows = rows_t.reshape(b, 4, KV_HEADS, HEAD_DIM, t).transpose(0, 4, 1, 2, 3)
        tp = t
    else:
        assert pos0 % SEL_BLOCK + t <= SEL_BLOCK
        tp = -(-t // 8) * 8
        rows_bt = rows_t[0].T.reshape(b, t, 2 * ncol)
        win_bt = win_t[0].T.reshape(b, t, ncol)
        n_pool = cache.shape[0]
        cache_t = cache.transpose(0, 2, 3, 4, 1).reshape(n_pool, 2 * ncol, PAGE_SIZE)
        kvc_t = _compress_paged(cache_t, page_table, cmp_pos_w, cmp_w, cmp_b)
        qp = _pad_t(q.reshape(NSA_HEADS, b, t, HEAD_DIM), 2, tp)
        n_blocks = -(-(pos0 + t) // SEL_BLOCK)
        o_cmp, selmask = _nsa_dec_select(qp, kvc_t, pos0, n_blocks)
        wlen = win_buf.shape[1]
        winb_t = win_buf.transpose(0, 2, 3, 4, 1).reshape(b, ncol, wlen)
        o = _nsa_dec_attend(qp, selmask, o_cmp, _pad_t(gates.reshape(b, t, -1), 1, tp),
                            _pad_t(rows_bt[:, :, ncol:], 1, tp), winb_t, _pad_t(win_bt, 1, tp),
                            cache_t, page_table, pos0, t)[:, :t]
        new_win_t = jnp.concatenate([winb_t, win_bt.transpose(0, 2, 1)], axis=2)[:, :, -wlen:]
        new_rows = rows_bt.reshape(b, t, 4, KV_HEADS, HEAD_DIM)
    yconv = _conv(_pad_t(u3, 1, tp), _pad_t(bg.reshape(b, t, CONV_CH), 1, tp), conv_buf, conv_w)[:, :t]
    x = _odd_out(x, o.reshape(n, NSA_WIDTH), yconv.reshape(n, CONV_CH), w_out)
    new_win = new_win_t.reshape(b, 2, KV_HEADS, HEAD_DIM, new_win_t.shape[2]).transpose(0, 4, 1, 2, 3)
    return x, new_rows, new_win, new_conv


def _trunk(x3, pos0, pool_buf, ml_c, ml_n, ml_m, kv_cache, page_table, win_buf, conv_buf, p):
    b, t, d = x3.shape
    depth = p["norm_g"].shape[0]
    x = x3.reshape(b * t, d)
    pools, cs, ns, ms, rows, wins, convs = [], [], [], [], [], [], []
    for l in range(depth):
        j = l // 2
        x = _ffn(x, p["norm_g"][l, 0], p["w_ffn_in"][l][0], p["w_ffn_out"][l][0])
        if l % 2 == 0:
            x, pb, c, n, m = _even_layer(x, b, t, pos0, pool_buf[j], ml_c[j], ml_n[j], ml_m[j], p["norm_g"][l, 1],
                                         p["w_in_even"][j], p["b_gate_even"][j], p["w_pool"][j],
                                         p["pool_scale"][j], p["w_out_even"][j])
            pools.append(pb)
            cs.append(c)
            ns.append(n)
            ms.append(m)
        else:
            cache = None if kv_cache is None else kv_cache[j]
            wb = None if win_buf is None else win_buf[j]
            x, r, wn, cn = _odd_layer(x, b, t, pos0, cache, page_table, wb, conv_buf[j], p["norm_g"][l, 1],
                                      p["w_in_odd"][j], p["cmp_pos_w"][j], p["cmp_w"][j], p["cmp_b"][j],
                                      p["conv_w"][j], p["w_out_odd"][j])
            rows.append(r)
            wins.append(wn)
            convs.append(cn)
        x = _ffn(x, p["norm_g"][l, 2], p["w_ffn_in"][l][1], p["w_ffn_out"][l][1],
                 final_g=p["final_g"] if l == depth - 1 else None)
    states = (jnp.stack(pools), jnp.stack(cs), jnp.stack(ns), jnp.stack(ms),
              jnp.stack(rows), jnp.stack(wins), jnp.stack(convs))
    return x.reshape(b, t, d), states


def kernel(x_prompt, x_sample, state_pool, state_mlstm_c, state_mlstm_n, state_mlstm_m, cache_nsa_kv, state_win_kv, state_conv, page_table, norm_g, final_g, w_ffn_in, w_ffn_out, w_in_even, b_gate_even, w_pool, pool_scale, w_out_even, w_in_odd, cmp_pos_w, cmp_w, cmp_b, conv_w, w_out_odd):
    bp = x_prompt.shape[0]
    n_even, n_odd = state_pool.shape[0], state_conv.shape[0]
    past_len = page_table.shape[1] * PAGE_SIZE
    p = dict(norm_g=norm_g, final_g=final_g, w_ffn_in=w_ffn_in.astype(BF16), w_ffn_out=w_ffn_out.astype(BF16),
             w_in_even=w_in_even, b_gate_even=b_gate_even, w_pool=w_pool, pool_scale=pool_scale,
             w_out_even=w_out_even, w_in_odd=w_in_odd, cmp_pos_w=cmp_pos_w, cmp_w=cmp_w, cmp_b=cmp_b,
             conv_w=conv_w, w_out_odd=w_out_odd)
    pool0 = jnp.zeros((n_even, bp) + state_pool.shape[2:], F32)
    c0 = jnp.zeros((n_even, bp) + state_mlstm_c.shape[2:], F32)
    n0 = jnp.zeros((n_even, bp) + state_mlstm_n.shape[2:], F32)
    m0 = jnp.zeros((n_even, bp) + state_mlstm_m.shape[2:], F32)
    conv0 = jnp.zeros((n_odd, bp) + state_conv.shape[2:], F32)
    y_p, (pool_p, c_p, n_p, m_p, kv_p, win_p, conv_p) = _trunk(
        x_prompt, 0, pool0, c0, n0, m0, None, None, None, conv0, p)
    y_s, (pool_s, c_s, n_s, m_s, kv_s, win_s, conv_s) = _trunk(
        x_sample, past_len, state_pool, state_mlstm_c, state_mlstm_n, state_mlstm_m,
        cache_nsa_kv, page_table, state_win_kv, state_conv, p)
    return (y_p, y_s, pool_p, pool_s, c_p, c_s, n_p, n_s, m_p, m_s,
            kv_p, kv_s, win_p, win_s, conv_p, conv_s)
```

```python
import functools

import numpy as np
import jax
import jax.numpy as jnp
from jax import lax
from jax.experimental import pallas as pl
from jax.experimental.pallas import tpu as pltpu

F32 = jnp.float32
BF16 = jnp.bfloat16
HIGHEST = lax.Precision.HIGHEST

EPS = 1e-6
POOL_WINDOWS = (2, 4, 8, 16)
POOL_GDIM = 64
POOL_WIDTH = 256
POOL_HALO = 16
ML_HEADS = 4
ML_DH = 192
ML_WIDTH = ML_HEADS * ML_DH
NSA_HEADS = 12
HEAD_DIM = 64
NSA_WIDTH = NSA_HEADS * HEAD_DIM
KV_HEADS = 2
GQ = NSA_HEADS // KV_HEADS
CMP_STRIDE = 16
SEL_BLOCK = 64
SEL_TOPN = 16
WINDOW = 512
Q_BLOCK = 64
PAGE_SIZE = 128
CONV_CH = 256
CONV_K = 3
CONV_HALO = 8

VMEM_LIMIT = 56 * 1024 * 1024
NEG = -1e30
LOG2E = 1.4426950408889634


def _params(sem, vmem=VMEM_LIMIT):
    return pltpu.CompilerParams(dimension_semantics=sem, vmem_limit_bytes=vmem)


def _const_spec(shape):
    nd = len(shape)
    return pl.BlockSpec(shape, lambda *_: (0,) * nd, pipeline_mode=pl.Buffered(1))


def _rms(x, g):
    return x * lax.rsqrt(jnp.mean(x * x, axis=-1, keepdims=True) + EPS) * g


def _dot(a, b):
    return jnp.dot(a, b, preferred_element_type=F32)


def _dot_nt(a, b):
    return lax.dot_general(a, b, (((1,), (1,)), ((), ())), preferred_element_type=F32)


def _ffn_body(x_ref, g_ref, win_ref, wout_ref, *rest, d_ff, chunks, has_final):
    o_ref = rest[-1]
    x = x_ref[...]
    hn = _rms(x, g_ref[...]).astype(BF16)
    acc = jnp.zeros(x.shape, F32)
    off = 0
    for fc in chunks:
        a = _dot(hn, win_ref[:, off:off + fc])
        b = _dot(hn, win_ref[:, d_ff + off:d_ff + off + fc])
        act = (a * jax.nn.sigmoid(a) * b).astype(BF16)
        acc = acc + _dot(act, wout_ref[off:off + fc, :])
        off += fc
    y = x + 0.5 * acc
    if has_final:
        y = _rms(y, rest[0][...])
    o_ref[...] = y


def _ffn(x, g, w_in, w_out, final_g=None):
    n, d = x.shape
    d_ff = w_out.shape[0]
    tm = 512 if n % 512 == 0 else n
    chunks, left = [], d_ff
    while left:
        chunks.append(min(1024, left))
        left -= chunks[-1]
    row = pl.BlockSpec((tm, d), lambda i: (i, 0))
    in_specs = [row, _const_spec((1, d)), _const_spec(w_in.shape), _const_spec(w_out.shape)]
    args = [x, g.reshape(1, d), w_in, w_out]
    if final_g is not None:
        in_specs.append(_const_spec((1, d)))
        args.append(final_g.reshape(1, d))
    return pl.pallas_call(
        functools.partial(_ffn_body, d_ff=d_ff, chunks=tuple(chunks), has_final=final_g is not None),
        grid=(n // tm,), in_specs=in_specs, out_specs=row,
        out_shape=jax.ShapeDtypeStruct((n, d), F32),
        compiler_params=_params(("parallel",)), name="ffn")(*args)


def _even_in_body(x_ref, g_ref, wu_ref, wh_ref, wvt_ref, wg_ref, wgt_ref, bg_ref, bgt_ref,
                  u_ref, q_ref, k_ref, v_ref, og_ref, vt_ref, gc_ref, gr_ref):
    hn = _rms(x_ref[...], g_ref[...]).astype(BF16)
    u_ref[...] = _dot(hn, wu_ref[...])
    for h in range(ML_HEADS):
        q_ref[h] = _dot(hn, wh_ref[h])
        k_ref[h] = _dot(hn, wh_ref[ML_HEADS + h])
        v_ref[h] = _dot(hn, wh_ref[2 * ML_HEADS + h])
        og_ref[h] = _dot(hn, wh_ref[3 * ML_HEADS + h])
        vt_ref[h] = _dot_nt(wvt_ref[h], hn)
    gc_ref[...] = _dot(hn, wg_ref[...]) + bg_ref[...]
    gr_ref[...] = _dot_nt(wgt_ref[...], hn) + bgt_ref[...]


def _even_in(x, g, w_in, b_gate):
    n, d = x.shape
    tm = 512 if n % 512 == 0 else n
    h4 = ML_HEADS
    wu = w_in[:, :POOL_WIDTH].astype(BF16)
    wh = w_in[:, POOL_WIDTH:POOL_WIDTH + 4 * ML_WIDTH].reshape(d, 4 * h4, ML_DH).transpose(1, 0, 2).astype(BF16)
    wvt = wh[2 * h4:3 * h4].transpose(0, 2, 1)
    wg = w_in[:, POOL_WIDTH + 4 * ML_WIDTH:].astype(BF16)
    row = lambda w: pl.BlockSpec((tm, w), lambda i: (i, 0))
    hrow = pl.BlockSpec((h4, tm, ML_DH), lambda i: (0, i, 0))
    hsh = jax.ShapeDtypeStruct((h4, n, ML_DH), F32)
    return pl.pallas_call(
        _even_in_body, grid=(n // tm,),
        in_specs=[row(d), _const_spec((1, d)), _const_spec(wu.shape), _const_spec(wh.shape),
                  _const_spec(wvt.shape), _const_spec(wg.shape), _const_spec((2 * h4, d)),
                  _const_spec((1, 2 * h4)), _const_spec((2 * h4, 1))],
        out_specs=[row(POOL_WIDTH), hrow, hrow, hrow, hrow,
                   pl.BlockSpec((h4, ML_DH, tm), lambda i: (0, 0, i)),
                   row(2 * h4), pl.BlockSpec((2 * h4, tm), lambda i: (0, i))],
        out_shape=[jax.ShapeDtypeStruct((n, POOL_WIDTH), F32), hsh, hsh, hsh, hsh,
                   jax.ShapeDtypeStruct((h4, ML_DH, n), F32),
                   jax.ShapeDtypeStruct((n, 2 * h4), F32), jax.ShapeDtypeStruct((2 * h4, n), F32)],
        compiler_params=_params(("parallel",)), name="even_in")(
            x, g.reshape(1, d), wu, wh, wvt, wg, wg.T, b_gate.reshape(1, -1), b_gate.reshape(-1, 1))


def _pool_body(u_ref, pre_ref, w_ref, sc_ref, y_ref, carry, full, *, tb, pos0):
    t = pl.program_id(1)

    @pl.when(t == 0)
    def _():
        carry[...] = pre_ref[...]

    u = u_ref[...]
    full[0:POOL_HALO] = carry[...]
    full[POOL_HALO:] = u
    acc = full[...]
    sums = []
    for sh in (1, 2, 4, 8):
        acc = acc + pltpu.roll(acc, sh, 0)
        sums.append(acc[POOL_HALO:])
    lane = lax.broadcasted_iota(jnp.int32, (tb, POOL_WIDTH), 1)
    grp = lane // POOL_GDIM
    win = jnp.where(grp == 0, sums[0], jnp.where(grp == 1, sums[1], jnp.where(grp == 2, sums[2], sums[3])))
    width = jnp.where(grp == 0, 2, jnp.where(grp == 1, 4, jnp.where(grp == 2, 8, 16)))
    pos = pos0 + t * tb + lax.broadcasted_iota(jnp.int32, (tb, POOL_WIDTH), 0)
    cnt = jnp.minimum(pos + 1, width).astype(F32)
    mixed = (win / cnt - u).astype(BF16)
    y_ref[...] = _dot(mixed, w_ref[...]) * sc_ref[...]
    carry[...] = full[tb:tb + POOL_HALO]


def _pool(u, prefix, w_pool, scale, pos0):
    b, t, c = u.shape
    tb = 512 if t % 512 == 0 else t
    pre = jnp.pad(prefix, ((0, 0), (POOL_HALO - prefix.shape[1], 0), (0, 0)))
    wbd = jax.scipy.linalg.block_diag(*[w_pool[i] for i in range(w_pool.shape[0])]).astype(BF16)
    return pl.pallas_call(
        functools.partial(_pool_body, tb=tb, pos0=pos0), grid=(b, t // tb),
        in_specs=[pl.BlockSpec((None, tb, c), lambda i, j: (i, j, 0)),
                  pl.BlockSpec((None, POOL_HALO, c), lambda i, j: (i, 0, 0)),
                  _const_spec((c, c)), _const_spec((1, c))],
        out_specs=pl.BlockSpec((None, tb, c), lambda i, j: (i, j, 0)),
        out_shape=jax.ShapeDtypeStruct((b, t, c), F32),
        scratch_shapes=[pltpu.VMEM((POOL_HALO, c), F32), pltpu.VMEM((tb + POOL_HALO, c), F32)],
        compiler_params=_params(("parallel", "arbitrary")), name="pool")(u, pre, wbd, scale.reshape(1, c))


def _log_sigmoid(x):
    return jnp.minimum(x, 0.0) - jnp.log(1.0 + jnp.exp(-jnp.abs(x)))


def _mlstm_body(q_ref, k_ref, v_ref, vt_ref, gc_ref, gr_ref, c0_ref, n0_ref, m0_ref,
                h_ref, c_ref, n_ref, m_ref, *, L, t_valid):
    @pl.when(pl.program_id(1) == 0)
    def _():
        c_ref[...] = c0_ref[...]
        n_ref[...] = n0_ref[...]
        m_ref[...] = m0_ref[...]

    gc = gc_ref[...]
    gr = gr_ref[...]
    lf_c = _log_sigmoid(gc)
    lf_r = _log_sigmoid(gr)
    row = lax.broadcasted_iota(jnp.int32, (L, L), 0)
    col = lax.broadcasted_iota(jnp.int32, (L, L), 1)
    tok_c = lax.broadcasted_iota(jnp.int32, (L, 1), 0) < t_valid
    tok_r = lax.broadcasted_iota(jnp.int32, (1, L), 1) < t_valid
    if t_valid < L:
        lf_c = jnp.where(tok_c, lf_c, 0.0)
        lf_r = jnp.where(tok_r, lf_r, 0.0)
    causal = row >= col
    tri = causal.astype(F32)
    cs_c = jnp.dot(tri, lf_c, precision=HIGHEST, preferred_element_type=F32)
    cs_r = lax.dot_general(lf_r, tri, (((1,), (1,)), ((), ())), precision=HIGHEST,
                           preferred_element_type=F32)
    for hd in range(ML_HEADS):
        i_c = gc[:, hd:hd + 1]
        i_r = gr[hd:hd + 1, :]
        if t_valid < L:
            i_c = jnp.where(tok_c, i_c, -jnp.inf)
            i_r = jnp.where(tok_r, i_r, -jnp.inf)
        b_c = cs_c[:, ML_HEADS + hd:ML_HEADS + hd + 1]
        b_r = cs_r[ML_HEADS + hd:ML_HEADS + hd + 1, :]
        m_prev = m_ref[hd]
        dmat = jnp.where(causal, b_c - b_r + i_r, -jnp.inf)
        inter = b_c + m_prev
        mt = jnp.maximum(jnp.max(dmat, axis=1, keepdims=True), inter)
        w = jnp.exp(dmat - mt)
        a = jnp.exp(inter - mt)
        q = q_ref[hd]
        kf = k_ref[hd] * (ML_DH ** -0.5)
        qb = q.astype(BF16)
        kb = kf.astype(BF16)
        s = _dot_nt(qb, kb) * w
        c_old = c_ref[hd]
        n_old = n_ref[hd]
        num = _dot(s.astype(BF16), v_ref[hd].astype(BF16)) + a * _dot_nt(qb, c_old.astype(BF16))
        den = jnp.sum(s, axis=1, keepdims=True) + a * jnp.sum(q * n_old, axis=1, keepdims=True)
        h_ref[hd] = num / jnp.maximum(jnp.abs(den), jnp.exp(-mt))
        b_last = b_c[L - 1:L, :]
        ge_r = b_last - b_r + i_r
        ge_c = b_last - b_c + i_c
        m_new = jnp.maximum(b_last + m_prev, jnp.max(ge_r, axis=1, keepdims=True))
        wk_r = jnp.exp(ge_r - m_new)
        wk_c = jnp.exp(ge_c - m_new)
        decay = jnp.exp(b_last + m_prev - m_new)
        c_ref[hd] = decay * c_old + _dot((vt_ref[hd] * wk_r).astype(BF16), kb)
        n_ref[hd] = decay * n_old + jnp.sum(kf * wk_c, axis=0, keepdims=True)
        m_ref[hd] = m_new


def _mlstm(q, k, v, vt, gc, gr, c0, n0, m0, *, L, t_valid):
    h4, b, t, dh = q.shape
    nc = t // L
    tok = pl.BlockSpec((h4, None, L, dh), lambda i, c: (0, i, c, 0))
    if vt.ndim == 4:
        vt_spec = pl.BlockSpec((h4, None, dh, L), lambda i, c: (0, i, 0, c))
        gr_spec = pl.BlockSpec((None, 2 * h4, L), lambda i, c: (i, 0, c))
    else:
        vt_spec = pl.BlockSpec((h4, dh, L), lambda i, c: (0, 0, i * nc + c))
        gr_spec = pl.BlockSpec((2 * h4, L), lambda i, c: (0, i * nc + c))
    st = lambda r, w: pl.BlockSpec((None, h4, r, w), lambda i, c: (i, 0, 0, 0))
    return pl.pallas_call(
        functools.partial(_mlstm_body, L=L, t_valid=t_valid), grid=(b, nc),
        in_specs=[tok, tok, tok, vt_spec, pl.BlockSpec((None, L, 2 * h4), lambda i, c: (i, c, 0)), gr_spec,
                  st(dh, dh), st(1, dh), st(1, 1)],
        out_specs=[tok, st(dh, dh), st(1, dh), st(1, 1)],
        out_shape=[jax.ShapeDtypeStruct((h4, b, t, dh), F32), jax.ShapeDtypeStruct((b, h4, dh, dh), F32),
                   jax.ShapeDtypeStruct((b, h4, 1, dh), F32), jax.ShapeDtypeStruct((b, h4, 1, 1), F32)],
        compiler_params=_params(("parallel", "arbitrary")), name="mlstm")(
            q, k, v, vt, gc, gr, c0, n0.reshape(b, h4, 1, dh), m0.reshape(b, h4, 1, 1))


def _even_out_body(x_ref, yp_ref, og_ref, hm_ref, w0_ref, w1_ref, o_ref):
    acc = _dot(yp_ref[...].astype(BF16), w0_ref[...])
    for h in range(ML_HEADS):
        acc = acc + _dot((jax.nn.sigmoid(og_ref[h]) * hm_ref[h]).astype(BF16), w1_ref[h])
    o_ref[...] = x_ref[...] + acc


def _even_out(x, ypool, og, hm, w_out):
    n, d = x.shape
    tm = 512 if n % 512 == 0 else n
    w0 = w_out[:POOL_WIDTH].astype(BF16)
    w1 = w_out[POOL_WIDTH:].reshape(ML_HEADS, ML_DH, d).astype(BF16)
    row = lambda w: pl.BlockSpec((tm, w), lambda i: (i, 0))
    hrow = pl.BlockSpec((ML_HEADS, tm, ML_DH), lambda i: (0, i, 0))
    return pl.pallas_call(
        _even_out_body, grid=(n // tm,),
        in_specs=[row(d), row(POOL_WIDTH), hrow, hrow, _const_spec(w0.shape), _const_spec(w1.shape)],
        out_specs=row(d), out_shape=jax.ShapeDtypeStruct((n, d), F32),
        compiler_params=_params(("parallel",)), name="even_out")(x, ypool, og, hm, w0, w1)


def _odd_in_body(x_ref, g_ref, wq_ref, wkvt_ref, wgt_ref, wc_ref,
                 q_ref, rows_ref, win_ref, gates_ref, ucv_ref, bg_ref):
    hn = _rms(x_ref[...], g_ref[...]).astype(BF16)
    for h in range(NSA_HEADS):
        q_ref[h] = _dot(hn, wq_ref[h]) * (LOG2E * HEAD_DIM ** -0.5)
    nrow = rows_ref.shape[0]
    rows_ref[...] = _dot_nt(wkvt_ref[:nrow], hn)
    win_ref[...] = _dot_nt(wkvt_ref[nrow:], hn)
    gates_ref[...] = jax.nn.sigmoid(_dot(hn, wgt_ref[...]))
    bg_ref[...] = _dot(hn, wc_ref[:, :CONV_CH])
    ucv_ref[...] = _dot(hn, wc_ref[:, CONV_CH:2 * CONV_CH]) * _dot(hn, wc_ref[:, 2 * CONV_CH:])


def _odd_in(x, g, w_in, seq):
    n, d = x.shape
    tm = 512 if seq % 512 == 0 else seq
    nt = seq // tm
    kvw = 6 * KV_HEADS * HEAD_DIM
    ngt = 3 * NSA_HEADS
    wq = w_in[:, :NSA_WIDTH].reshape(d, NSA_HEADS, HEAD_DIM).transpose(1, 0, 2).astype(BF16)
    wkvt = w_in[:, NSA_WIDTH:NSA_WIDTH + kvw].T.astype(BF16)
    wgt = w_in[:, NSA_WIDTH + kvw:NSA_WIDTH + kvw + ngt].astype(BF16)
    wc = w_in[:, NSA_WIDTH + kvw + ngt:].astype(BF16)
    nrow = 4 * KV_HEADS * HEAD_DIM
    row = lambda w: pl.BlockSpec((tm, w), lambda i: (i, 0))
    slab = lambda r: pl.BlockSpec((None, r, tm), lambda i: (i // nt, 0, i % nt))
    sh = lambda w: jax.ShapeDtypeStruct((n, w), F32)
    return pl.pallas_call(
        _odd_in_body, grid=(n // tm,),
        in_specs=[row(d), _const_spec((1, d)), _const_spec(wq.shape), _const_spec(wkvt.shape),
                  _const_spec(wgt.shape), _const_spec(wc.shape)],
        out_specs=[pl.BlockSpec((NSA_HEADS, tm, HEAD_DIM), lambda i: (0, i, 0)),
                   slab(nrow), slab(kvw - nrow), row(ngt), row(CONV_CH), row(CONV_CH)],
        out_shape=[jax.ShapeDtypeStruct((NSA_HEADS, n, HEAD_DIM), F32),
                   jax.ShapeDtypeStruct((n // seq, nrow, seq), F32),
                   jax.ShapeDtypeStruct((n // seq, kvw - nrow, seq), F32),
                   sh(ngt), sh(CONV_CH), sh(CONV_CH)],
        compiler_params=_params(("parallel",)), name="odd_in")(x, g.reshape(1, d), wq, wkvt, wgt, wc)


def _conv_body(u_ref, bg_ref, pre_ref, w_ref, y_ref, carry, full, *, tb):
    @pl.when(pl.program_id(1) == 0)
    def _():
        carry[...] = pre_ref[...]

    full[0:CONV_HALO] = carry[...]
    full[CONV_HALO:] = u_ref[...]
    f = full[...]
    w = w_ref[...]
    conv = f * w[2:3] + pltpu.roll(f, 1, 0) * w[1:2] + pltpu.roll(f, 2, 0) * w[0:1]
    y_ref[...] = bg_ref[...] * conv[CONV_HALO:]
    carry[...] = full[tb:tb + CONV_HALO]


def _conv(u, bg, prefix, conv_w):
    b, t, c = u.shape
    tb = 512 if t % 512 == 0 else t
    pre = jnp.pad(prefix, ((0, 0), (CONV_HALO - prefix.shape[1], 0), (0, 0)))
    blk = pl.BlockSpec((None, tb, c), lambda i, j: (i, j, 0))
    return pl.pallas_call(
        functools.partial(_conv_body, tb=tb), grid=(b, t // tb),
        in_specs=[blk, blk, pl.BlockSpec((None, CONV_HALO, c), lambda i, j: (i, 0, 0)), _const_spec((CONV_K, c))],
        out_specs=blk, out_shape=jax.ShapeDtypeStruct((b, t, c), F32),
        scratch_shapes=[pltpu.VMEM((CONV_HALO, c), F32), pltpu.VMEM((tb + CONV_HALO, c), F32)],
        compiler_params=_params(("parallel", "arbitrary")), name="conv")(u, bg, pre, conv_w)


def _odd_out_body(x_ref, o_ref, yc_ref, w0_ref, w1_ref, out_ref):
    out_ref[...] = (x_ref[...] + _dot(o_ref[...].astype(BF16), w0_ref[...])
                    + _dot(yc_ref[...].astype(BF16), w1_ref[...]))


def _odd_out(x, o, yconv, w_out):
    n, d = x.shape
    tm = 512 if n % 512 == 0 else n
    w0 = w_out[:NSA_WIDTH].astype(BF16)
    w1 = w_out[NSA_WIDTH:].astype(BF16)
    row = lambda w: pl.BlockSpec((tm, w), lambda i: (i, 0))
    return pl.pallas_call(
        _odd_out_body, grid=(n // tm,),
        in_specs=[row(d), row(NSA_WIDTH), row(CONV_CH), _const_spec(w0.shape), _const_spec(w1.shape)],
        out_specs=row(d), out_shape=jax.ShapeDtypeStruct((n, d), F32),
        compiler_params=_params(("parallel",)), name="odd_out")(x, o, yconv, w0, w1)


CMP_TILE = 2048
CMP_ROWS = 4 * HEAD_DIM


def _cmp_weights(cmp_pos_w, cmp_w, cmp_b, tile):
    m = tile // CMP_STRIDE
    cols = -(-(m + 1) // 128) * 128
    pos = jnp.arange(tile)
    chunk = (pos // CMP_STRIDE)[:, None]
    col = jnp.arange(cols)[None, :]
    mats = []
    for kv in range(2):
        wa = cmp_pos_w[kv, :CMP_STRIDE][pos % CMP_STRIDE][:, None]
        wb = cmp_pos_w[kv, CMP_STRIDE:][pos % CMP_STRIDE][:, None]
        mats.append(jnp.where(chunk == col, wb, 0.0) + jnp.where(chunk == col - 1, wa, 0.0))
    w2 = jnp.stack(mats).astype(BF16)
    wbdt = jax.scipy.linalg.block_diag(cmp_w[0], cmp_w[0], cmp_w[1], cmp_w[1]).T.astype(BF16)
    biast = jnp.concatenate([cmp_b[0], cmp_b[0], cmp_b[1], cmp_b[1]]).reshape(-1, 1)
    return w2, wbdt, biast


def _compress_body(*refs, n_pages, n_prefetch):
    refs = refs[n_prefetch:]
    pages = refs[:n_pages]
    w2_ref, wbdt_ref, biast_ref, out_ref, carry = refs[n_pages:]
    m_out = out_ref.shape[1]

    @pl.when(pl.program_id(1) == 0)
    def _():
        carry[...] = jnp.zeros(carry.shape, F32)

    x = jnp.concatenate([pg[...] for pg in pages], axis=1).astype(BF16)
    half = CMP_ROWS // 2
    res = jnp.concatenate([_dot(x[:half], w2_ref[0]), _dot(x[half:], w2_ref[1])], axis=0)
    first = lax.broadcasted_iota(jnp.int32, (CMP_ROWS, m_out), 1) == 0
    pre = res[:, :m_out] + jnp.where(first, carry[...], 0.0)
    carry[...] = jnp.broadcast_to(res[:, m_out:m_out + 1], carry.shape)
    out_ref[...] = _dot(wbdt_ref[...], pre.astype(BF16)) + biast_ref[...]


def _compress_prompt(rows_t, cmp_pos_w, cmp_w, cmp_b):
    b, _, t = rows_t.shape
    tile = min(CMP_TILE, t)
    assert t % tile == 0
    w2, wbdt, biast = _cmp_weights(cmp_pos_w, cmp_w, cmp_b, tile)
    m_out = tile // CMP_STRIDE
    return pl.pallas_call(
        functools.partial(_compress_body, n_pages=1, n_prefetch=0), grid=(b, t // tile),
        in_specs=[pl.BlockSpec((None, CMP_ROWS, tile), lambda i, j: (i, 0, j)),
                  _const_spec(w2.shape), _const_spec(wbdt.shape), _const_spec(biast.shape)],
        out_specs=pl.BlockSpec((None, CMP_ROWS, m_out), lambda i, j: (i, 0, j)),
        out_shape=jax.ShapeDtypeStruct((b, CMP_ROWS, t // CMP_STRIDE), F32),
        scratch_shapes=[pltpu.VMEM((CMP_ROWS, m_out), F32)],
        compiler_params=_params(("parallel", "arbitrary")), name="compress_prompt")(rows_t, w2, wbdt, biast)


def _compress_paged(cache_t, page_table, cmp_pos_w, cmp_w, cmp_b):
    b, npg = page_table.shape
    pg = min(CMP_TILE // PAGE_SIZE, npg)
    assert npg % pg == 0
    tile = pg * PAGE_SIZE
    w2, wbdt, biast = _cmp_weights(cmp_pos_w, cmp_w, cmp_b, tile)
    m_out = tile // CMP_STRIDE
    specs = [pl.BlockSpec((None, CMP_ROWS, PAGE_SIZE),
                          functools.partial(lambda i, j, pt, p: (pt[i, j * pg + p], 0, 0), p=p))
             for p in range(pg)]
    return pl.pallas_call(
        functools.partial(_compress_body, n_pages=pg, n_prefetch=1),
        grid_spec=pltpu.PrefetchScalarGridSpec(
            num_scalar_prefetch=1, grid=(b, npg // pg),
            in_specs=specs + [_const_spec(w2.shape), _const_spec(wbdt.shape), _const_spec(biast.shape)],
            out_specs=pl.BlockSpec((None, CMP_ROWS, m_out), lambda i, j, pt: (i, 0, j)),
            scratch_shapes=[pltpu.VMEM((CMP_ROWS, m_out), F32)]),
        out_shape=jax.ShapeDtypeStruct((b, CMP_ROWS, npg * PAGE_SIZE // CMP_STRIDE), F32),
        compiler_params=_params(("parallel", "arbitrary")), name="compress_paged")(
            page_table, *([cache_t] * pg), w2, wbdt, biast)


def _alibi_slopes(rows_per_head):
    sl = (LOG2E * 2.0 ** (-8.0 * np.arange(1, NSA_HEADS + 1) / NSA_HEADS)).astype(np.float32).reshape(KV_HEADS, GQ)
    return jnp.asarray(np.repeat(sl, rows_per_head, axis=1)[:, :, None])


def _score_matrix(n_cmp_rows, n_blocks, lanes):
    m = np.arange(n_cmp_rows)[:, None]
    j = np.arange(lanes)[None, :]
    return jnp.asarray(((m >= 4 * j) & (m <= 4 * j + 4) & (m >= 1) & (j < n_blocks)).astype(np.float32))


def _softmax_rows(s, mask):
    s = jnp.where(mask, s, -jnp.inf)
    m = jnp.max(s, axis=-1, keepdims=True)
    m = jnp.where(m > -jnp.inf, m, 0.0)
    e = jnp.exp2(s - m)
    return e / jnp.maximum(jnp.sum(e, axis=-1, keepdims=True), 1e-30)


def _select_blocks(score, cur, n_blocks):
    r, lanes = score.shape
    blk = lax.broadcasted_iota(jnp.int32, (r, lanes), 1)
    forced = (blk == 0) | (blk == cur) | (blk == cur - 1)
    val = jnp.where(forced, jnp.inf, jnp.where(blk <= cur, score, -jnp.inf))
    rank = jnp.zeros((r, lanes), jnp.int32)
    for i in range(n_blocks):
        ci = val[:, i:i + 1]
        ahead = (ci > val) | ((ci == val) & (blk > i))
        rank = rank + ahead.astype(jnp.int32)
    return ((rank < min(SEL_TOPN, n_blocks)) & (blk < n_blocks)).astype(F32)


POS_ROWS = 16


def _slope_pieces(rows_per_head):
    sl = _alibi_slopes(rows_per_head)
    hi = sl.astype(BF16)
    mid = (sl - hi.astype(F32)).astype(BF16)
    lo = (sl - hi.astype(F32) - mid.astype(F32)).astype(BF16)
    pad = jnp.zeros(sl.shape[:2] + (POS_ROWS - 6,), BF16)
    return jnp.concatenate([hi, mid, lo, hi, mid, lo, pad], axis=2)


def _pos_rows(t_len):
    k = np.arange(t_len)
    hi = (SEL_BLOCK * (k // SEL_BLOCK)).astype(np.float32)
    lo = (k % SEL_BLOCK).astype(np.float32)
    return np.stack([hi, hi, hi, lo, lo, lo] + [np.zeros(t_len, np.float32)] * (POS_ROWS - 6))


def _block_rows(n_rows, n_keys):
    return (np.arange(n_rows)[:, None] == (np.arange(n_keys) // SEL_BLOCK)[None, :]).astype(np.float32)


def _key_constants(t_len):
    both = np.concatenate([_pos_rows(t_len), _block_rows(t_len // SEL_BLOCK, t_len)], axis=0)
    return jnp.asarray(both).astype(BF16)


def _select_bias_t(score_t, cur):
    n_blocks, r = score_t.shape
    blk = lax.broadcasted_iota(jnp.int32, (n_blocks, r), 0)
    forced = (blk == 0) | (blk == cur) | (blk == cur - 1)
    val = jnp.where(forced, jnp.inf, jnp.where(blk <= cur, score_t, -jnp.inf))
    rank = jnp.zeros((n_blocks, r), F32)
    for i in range(n_blocks):
        ci = val[i:i + 1, :]
        ahead = (ci > val) | ((ci == val) & (blk > i))
        rank = rank + jnp.where(ahead, 1.0, 0.0)
    return jnp.where((rank < min(SEL_TOPN, n_blocks)) & (blk <= cur), 0.0, NEG)


def _nsa_prompt_body(q_ref, gates_ref, kvc_ref, sel_ref, win_ref, slope_ref, saug_ref, kconst_ref,
                     smat_ref, o_ref, need_sc, *, t_len, kc, wb):
    bi = pl.program_id(1)
    p0 = bi * Q_BLOCK
    nrow = GQ * Q_BLOCK
    n_cmp = kvc_ref.shape[1]
    n_blocks = t_len // SEL_BLOCK
    qpos = lax.broadcasted_iota(jnp.int32, (Q_BLOCK, 1), 0) + p0
    tpos = jnp.concatenate([qpos] * GQ, axis=0)
    gates = gates_ref[...]

    qgs = [q_ref[g * GQ:(g + 1) * GQ].reshape(nrow, HEAD_DIM).astype(BF16) for g in range(KV_HEADS)]
    q_pos = [jnp.concatenate([qgs[g], saug_ref[g]], axis=1) for g in range(KV_HEADS)]

    wstart = pl.multiple_of(jnp.clip((p0 - WINDOW) // 128 * 128, 0, t_len - wb), 128)
    wpos = wstart + lax.broadcasted_iota(jnp.int32, (1, wb), 1)
    wmask = (wpos <= tpos) & (wpos > tpos - WINDOW)
    o_wins = []
    for g in range(KV_HEADS):
        vs = (KV_HEADS + g) * HEAD_DIM
        katw = jnp.concatenate([win_ref[g * HEAD_DIM:(g + 1) * HEAD_DIM, pl.ds(wstart, wb)].astype(BF16),
                                kconst_ref[0:POS_ROWS, pl.ds(wstart, wb)]], axis=0)
        sw = jnp.where(wmask, _dot(q_pos[g], katw), -jnp.inf)
        e_win = jnp.exp2(sw - jnp.max(sw, axis=1, keepdims=True))
        vwt = jnp.concatenate([win_ref[vs:vs + HEAD_DIM, pl.ds(wstart, wb)].astype(BF16),
                               jnp.ones((POS_ROWS, wb), BF16)], axis=0)
        acc_w = _dot_nt(e_win.astype(BF16), vwt)
        o_wins.append(acc_w[:, :HEAD_DIM] / acc_w[:, HEAD_DIM:HEAD_DIM + 1])

    cpos = CMP_STRIDE * lax.broadcasted_iota(jnp.int32, (1, n_cmp), 1) + (CMP_STRIDE - 1)
    dist = tpos - cpos
    cmask = (dist >= 0) & (cpos >= 2 * CMP_STRIDE - 1)
    o_cmps, imps = [], []
    for g in range(KV_HEADS):
        ks, vs = g * HEAD_DIM, (KV_HEADS + g) * HEAD_DIM
        s = _dot(qgs[g], kvc_ref[ks:ks + HEAD_DIM, :].astype(BF16)) - slope_ref[g] * dist.astype(F32)
        p_cmp = _softmax_rows(s, cmask)
        o_cmps.append(_dot_nt(p_cmp.astype(BF16), kvc_ref[vs:vs + HEAD_DIM, :].astype(BF16)))
        imps.append(jnp.sum(p_cmp.reshape(GQ, Q_BLOCK, n_cmp), axis=0))
        for i in range(GQ):
            hh = g * GQ + i
            r = slice(i * Q_BLOCK, (i + 1) * Q_BLOCK)
            o_ref[:, hh * HEAD_DIM:(hh + 1) * HEAD_DIM] = (
                gates[:, 3 * hh:3 * hh + 1] * o_cmps[g][r] + gates[:, 3 * hh + 2:3 * hh + 3] * o_wins[g][r])
    score_t = lax.dot_general(smat_ref[...], jnp.concatenate(imps, axis=0), (((1,), (1,)), ((), ())),
                              precision=HIGHEST, preferred_element_type=F32)
    bias_f = _select_bias_t(score_t, bi)
    bias_t = bias_f.astype(BF16)
    eye = (lax.broadcasted_iota(jnp.int32, (Q_BLOCK, Q_BLOCK), 0)
           == lax.broadcasted_iota(jnp.int32, (Q_BLOCK, Q_BLOCK), 1)).astype(BF16)
    biases = [_dot_nt(eye, bias_t[:, g * Q_BLOCK:(g + 1) * Q_BLOCK]) for g in range(KV_HEADS)]
    q_aug = [jnp.concatenate([q_pos[g], jnp.concatenate([biases[g]] * GQ, axis=0).astype(BF16)], axis=1)
             for g in range(KV_HEADS)]

    ones_rows = jnp.ones((POS_ROWS, kc), BF16)

    def chunk_scores(g, c):
        k0 = pl.multiple_of(c * kc, kc)
        kat = jnp.concatenate([sel_ref[g * HEAD_DIM:(g + 1) * HEAD_DIM, pl.ds(k0, kc)].astype(BF16),
                               kconst_ref[:, pl.ds(k0, kc)]], axis=0)
        return _dot(q_aug[g], kat), k0

    def accumulate(g, s, k0, carry):
        m_i, acc = carry
        vs = (KV_HEADS + g) * HEAD_DIM
        m_new = jnp.maximum(m_i, jnp.max(s, axis=1, keepdims=True))
        pr = jnp.exp2(s - m_new)
        alpha = jnp.exp2(m_i - m_new)
        vt = jnp.concatenate([sel_ref[vs:vs + HEAD_DIM, pl.ds(k0, kc)].astype(BF16), ones_rows], axis=0)
        return m_new, alpha * acc + _dot_nt(pr.astype(BF16), vt)

    last = (p0 + Q_BLOCK - 1) // kc
    blocks_per_chunk = kc // SEL_BLOCK
    n_need = jnp.int32(0)
    for c in range(t_len // kc - 1):
        picked = bias_f[c * blocks_per_chunk:(c + 1) * blocks_per_chunk, :] == 0.0
        need = jnp.where(c < last, jnp.max(jnp.where(picked, 1, 0)), 0)
        need_sc[n_need] = c
        n_need = n_need + need

    def sel_step(i, carry):
        c = need_sc[i]
        out = []
        for g in range(KV_HEADS):
            s, k0 = chunk_scores(g, c)
            out.append(accumulate(g, s, k0, carry[g]))
        return tuple(out)

    one = (jnp.full((nrow, 1), NEG, F32), jnp.zeros((nrow, HEAD_DIM + POS_ROWS), F32))
    carry = lax.fori_loop(0, n_need, sel_step, (one,) * KV_HEADS)
    for g in range(KV_HEADS):
        s, k0 = chunk_scores(g, last)
        kpos = k0 + lax.broadcasted_iota(jnp.int32, (1, kc), 1)
        _, acc_s = accumulate(g, jnp.where(kpos <= tpos, s, NEG), k0, carry[g])
        o_sel = acc_s[:, :HEAD_DIM] / jnp.maximum(acc_s[:, HEAD_DIM:HEAD_DIM + 1], 1e-30)
        for i in range(GQ):
            hh = g * GQ + i
            r = slice(i * Q_BLOCK, (i + 1) * Q_BLOCK)
            o_ref[:, hh * HEAD_DIM:(hh + 1) * HEAD_DIM] += gates[:, 3 * hh + 1:3 * hh + 2] * o_sel[r]


def _nsa_prompt(q, gates, kvc_t, rows_t, win_t):
    _, b, t, _ = q.shape
    assert t % 128 == 0
    kc = 512 if t % 512 == 0 else t
    wb = min(WINDOW + 128, t)
    n_cmp = kvc_t.shape[2]
    n_blocks = t // SEL_BLOCK
    ncol = 4 * HEAD_DIM
    slopes = _alibi_slopes(Q_BLOCK)
    saug = _slope_pieces(Q_BLOCK)
    kconst = _key_constants(t)
    smat = _score_matrix(n_cmp, n_blocks, n_blocks).T
    return pl.pallas_call(
        functools.partial(_nsa_prompt_body, t_len=t, kc=kc, wb=wb), grid=(b, t // Q_BLOCK),
        in_specs=[pl.BlockSpec((NSA_HEADS, None, Q_BLOCK, HEAD_DIM), lambda i, j: (0, i, j, 0)),
                  pl.BlockSpec((None, Q_BLOCK, 3 * NSA_HEADS), lambda i, j: (i, j, 0)),
                  pl.BlockSpec((None, ncol, n_cmp), lambda i, j: (i, 0, 0)),
                  pl.BlockSpec((None, ncol, t), lambda i, j: (i, 1, 0)),
                  pl.BlockSpec((None, ncol, t), lambda i, j: (i, 0, 0)),
                  _const_spec(slopes.shape), _const_spec(saug.shape), _const_spec(kconst.shape),
                  _const_spec(smat.shape)],
        out_specs=pl.BlockSpec((None, Q_BLOCK, NSA_WIDTH), lambda i, j: (i, j, 0)),
        out_shape=jax.ShapeDtypeStruct((b, t, NSA_WIDTH), F32),
        scratch_shapes=[pltpu.SMEM((t // kc,), jnp.int32)],
        compiler_params=_params(("parallel", "arbitrary")), name="nsa_prompt")(
            q, gates, kvc_t, rows_t, win_t, slopes, saug, kconst, smat)


def _nsa_dec_select_body(q_ref, kvc_ref, slope_ref, smat_ref, ocmp_ref, sel_ref, *, past_len, tq, n_blocks,
                         step_blocks):
    n_cmp = kvc_ref.shape[1]
    n_steps = sel_ref.shape[1] - 1
    lanes_out = sel_ref.shape[3]
    nrow = GQ * tq
    qpos = lax.broadcasted_iota(jnp.int32, (tq, 1), 0) + past_len
    tpos = jnp.concatenate([qpos] * GQ, axis=0)
    cpos = CMP_STRIDE * lax.broadcasted_iota(jnp.int32, (1, n_cmp), 1) + (CMP_STRIDE - 1)
    dist = tpos - cpos
    for g in range(KV_HEADS):
        qg = q_ref[g * GQ:(g + 1) * GQ].reshape(nrow, HEAD_DIM).astype(BF16)
        ks, vs = g * HEAD_DIM, (KV_HEADS + g) * HEAD_DIM
        s = _dot(qg, kvc_ref[ks:ks + HEAD_DIM, :].astype(BF16)) - slope_ref[g] * dist.astype(F32)
        p_cmp = _softmax_rows(s, (dist >= 0) & (cpos >= 2 * CMP_STRIDE - 1))
        ocmp_ref[g] = _dot_nt(p_cmp.astype(BF16), kvc_ref[vs:vs + HEAD_DIM, :].astype(BF16))
        imp = jnp.sum(p_cmp.reshape(GQ, tq, n_cmp), axis=0)
        score = jnp.dot(imp, smat_ref[...], precision=HIGHEST, preferred_element_type=F32)
        bias = jnp.where(_select_blocks(score, qpos // SEL_BLOCK, n_blocks) > 0.5, 0.0, NEG)
        pad = jnp.zeros((tq, lanes_out - step_blocks), F32)
        for st in range(n_steps):
            sel_ref[g, st] = jnp.concatenate([bias[:, st * step_blocks:(st + 1) * step_blocks], pad], axis=1)
        last_blk = n_steps * step_blocks
        sel_ref[g, n_steps] = jnp.concatenate([bias[:, last_blk:last_blk + 1],
                                               jnp.zeros((tq, lanes_out - 1), F32)], axis=1)


def _nsa_dec_select(q, kvc, past_len, n_blocks, step_blocks):
    _, b, tq, _ = q.shape
    n_cmp = kvc.shape[2]
    lanes = -(-n_blocks // 128) * 128
    n_steps = (n_blocks - 1) // step_blocks
    assert n_steps * step_blocks == n_blocks - 1 and step_blocks <= 128
    slopes = _alibi_slopes(tq)
    smat = _score_matrix(n_cmp, n_blocks, lanes)
    return pl.pallas_call(
        functools.partial(_nsa_dec_select_body, past_len=past_len, tq=tq, n_blocks=n_blocks,
                          step_blocks=step_blocks), grid=(b,),
        in_specs=[pl.BlockSpec((NSA_HEADS, None, tq, HEAD_DIM), lambda i: (0, i, 0, 0)),
                  pl.BlockSpec((None, 4 * HEAD_DIM, n_cmp), lambda i: (i, 0, 0)),
                  _const_spec(slopes.shape), _const_spec(smat.shape)],
        out_specs=[pl.BlockSpec((None, KV_HEADS, GQ * tq, HEAD_DIM), lambda i: (i, 0, 0, 0)),
                   pl.BlockSpec((None, KV_HEADS, n_steps + 1, tq, 128), lambda i: (i, 0, 0, 0, 0))],
        out_shape=[jax.ShapeDtypeStruct((b, KV_HEADS, GQ * tq, HEAD_DIM), F32),
                   jax.ShapeDtypeStruct((b, KV_HEADS, n_steps + 1, tq, 128), F32)],
        compiler_params=_params(("parallel",)), name="nsa_dec_select")(q, kvc, slopes, smat)


def _nsa_dec_attend_body(pt_ref, q_ref, selb_ref, ocmp_ref, gates_ref, newsel_ref, winbuf_ref, newwin_ref,
                         slope_ref, saug_ref, pos_ref, blk_ref, *rest, past_len, tq, t_valid, n_pages):
    pages = rest[:n_pages]
    o_ref, m_sc, acc_sc = rest[n_pages:]
    step = pl.program_id(1)
    nrow = GQ * tq
    qpos = lax.broadcasted_iota(jnp.int32, (tq, 1), 0) + past_len
    tpos = jnp.concatenate([qpos] * GQ, axis=0)

    @pl.when(step == 0)
    def _():
        m_sc[...] = jnp.full(m_sc.shape, NEG, F32)
        acc_sc[...] = jnp.zeros(acc_sc.shape, F32)

    def update(g, sc, weighted_values):
        m_i = m_sc[g]
        m_new = jnp.maximum(m_i, jnp.max(sc, axis=1, keepdims=True))
        pr = jnp.exp2(sc - m_new)
        acc_sc[g] = jnp.exp2(m_i - m_new) * acc_sc[g] + weighted_values(pr.astype(BF16))
        m_sc[g] = m_new

    nk = n_pages * PAGE_SIZE
    ones_rows = jnp.ones((POS_ROWS, nk), BF16)
    for g in range(KV_HEADS):
        qg = q_ref[g * GQ:(g + 1) * GQ].reshape(nrow, HEAD_DIM).astype(BF16)
        ks, vs = g * HEAD_DIM, (KV_HEADS + g) * HEAD_DIM
        bias = jnp.concatenate([selb_ref[g, step]] * GQ, axis=0).astype(BF16)
        q_aug = jnp.concatenate([qg, saug_ref[g], bias], axis=1)
        kat = jnp.concatenate([jnp.concatenate([pg[ks:ks + HEAD_DIM, :] for pg in pages], axis=1).astype(BF16),
                               pos_ref[...], blk_ref[...]], axis=0)
        vt = jnp.concatenate([jnp.concatenate([pg[vs:vs + HEAD_DIM, :] for pg in pages], axis=1).astype(BF16),
                              ones_rows], axis=0)
        update(g, _dot(q_aug, kat), lambda p, vt=vt: _dot_nt(p, vt))

    @pl.when(step == pl.num_programs(1) - 1)
    def _():
        gates = gates_ref[...]
        tn = newsel_ref.shape[0]
        npos = past_len + lax.broadcasted_iota(jnp.int32, (1, tn), 1)
        nd = tpos - npos
        valid_new = (nd >= 0) & (npos < past_len + t_valid)
        wlen = winbuf_ref.shape[1]
        wpos = past_len - wlen + lax.broadcasted_iota(jnp.int32, (1, wlen), 1)
        wd = tpos - wpos
        for g in range(KV_HEADS):
            qg = q_ref[g * GQ:(g + 1) * GQ].reshape(nrow, HEAD_DIM).astype(BF16)
            ks, vs = g * HEAD_DIM, (KV_HEADS + g) * HEAD_DIM
            slope = slope_ref[g]
            block_bias = jnp.concatenate([selb_ref[g, selb_ref.shape[1] - 1][:, 0:1]] * GQ, axis=0)
            sc = (_dot_nt(qg, newsel_ref[:, ks:ks + HEAD_DIM].astype(BF16)) + slope * npos.astype(F32)
                  + block_bias)
            new_v = jnp.concatenate([newsel_ref[:, vs:vs + HEAD_DIM].astype(BF16),
                                     jnp.ones((tn, POS_ROWS), BF16)], axis=1)
            update(g, jnp.where(valid_new, sc, NEG), lambda p, new_v=new_v: _dot(p, new_v))
            acc = acc_sc[g]
            o_sel = acc[:, :HEAD_DIM] / jnp.maximum(acc[:, HEAD_DIM:HEAD_DIM + 1], 1e-30)
            s1 = _dot(qg, winbuf_ref[ks:ks + HEAD_DIM, :].astype(BF16)) - slope * wd.astype(F32)
            s2 = _dot_nt(qg, newwin_ref[:, ks:ks + HEAD_DIM].astype(BF16)) - slope * nd.astype(F32)
            ok1 = (wd >= 0) & (wd < WINDOW) & (wpos >= 0)
            ok2 = valid_new & (nd < WINDOW)
            s1 = jnp.where(ok1, s1, -jnp.inf)
            s2 = jnp.where(ok2, s2, -jnp.inf)
            mx = jnp.maximum(jnp.max(s1, axis=1, keepdims=True), jnp.max(s2, axis=1, keepdims=True))
            mx = jnp.where(mx > -jnp.inf, mx, 0.0)
            e1 = jnp.exp2(s1 - mx)
            e2 = jnp.exp2(s2 - mx)
            den = jnp.maximum(jnp.sum(e1, axis=1, keepdims=True) + jnp.sum(e2, axis=1, keepdims=True), 1e-30)
            o_win = (_dot_nt(e1.astype(BF16), winbuf_ref[vs:vs + HEAD_DIM, :].astype(BF16))
                     + _dot(e2.astype(BF16), newwin_ref[:, vs:vs + HEAD_DIM].astype(BF16))) / den
            o_cmp = ocmp_ref[g]
            for i in range(GQ):
                hh = g * GQ + i
                r = slice(i * tq, (i + 1) * tq)
                o_ref[:, hh * HEAD_DIM:(hh + 1) * HEAD_DIM] = (
                    gates[:, 3 * hh:3 * hh + 1] * o_cmp[r] + gates[:, 3 * hh + 1:3 * hh + 2] * o_sel[r]
                    + gates[:, 3 * hh + 2:3 * hh + 3] * o_win[r])


DEC_PAGES = 16


def _nsa_dec_attend(q, selbias, o_cmp, gates, newsel, winbuf, newwin, cache, page_table, past_len, t_valid):
    _, b, tq, _ = q.shape
    npg = page_table.shape[1]
    pg = min(DEC_PAGES, npg)
    assert npg % pg == 0 and selbias.shape[2] == npg // pg + 1
    nk = pg * PAGE_SIZE
    ncol = 4 * HEAD_DIM
    slopes = _alibi_slopes(tq)
    saug = _slope_pieces(tq)
    pos_rows = jnp.asarray(_pos_rows(past_len)).astype(BF16)
    blk_rows = jnp.asarray(_block_rows(selbias.shape[-1], nk)).astype(BF16)
    wlen = winbuf.shape[2]
    bspec = lambda shape: pl.BlockSpec((None,) + shape, lambda i, j, pt: (i,) + (0,) * len(shape))
    page_specs = [pl.BlockSpec((None, ncol, PAGE_SIZE),
                               functools.partial(lambda i, j, pt, p: (pt[i, j * pg + p], 1, 0), p=p))
                  for p in range(pg)]
    return pl.pallas_call(
        functools.partial(_nsa_dec_attend_body, past_len=past_len, tq=tq, t_valid=t_valid, n_pages=pg),
        grid_spec=pltpu.PrefetchScalarGridSpec(
            num_scalar_prefetch=1, grid=(b, npg // pg),
            in_specs=[pl.BlockSpec((NSA_HEADS, None, tq, HEAD_DIM), lambda i, j, pt: (0, i, 0, 0)),
                      bspec(selbias.shape[1:]), bspec((KV_HEADS, GQ * tq, HEAD_DIM)),
                      bspec((tq, 3 * NSA_HEADS)), bspec((tq, ncol)), bspec((ncol, wlen)), bspec((tq, ncol)),
                      _const_spec(slopes.shape), _const_spec(saug.shape),
                      pl.BlockSpec((POS_ROWS, nk), lambda i, j, pt: (0, j)), _const_spec(blk_rows.shape)]
            + page_specs,
            out_specs=bspec((tq, NSA_WIDTH)),
            scratch_shapes=[pltpu.VMEM((KV_HEADS, GQ * tq, 1), F32),
                            pltpu.VMEM((KV_HEADS, GQ * tq, HEAD_DIM + POS_ROWS), F32)]),
        out_shape=jax.ShapeDtypeStruct((b, tq, NSA_WIDTH), F32),
        compiler_params=_params(("parallel", "arbitrary")), name="nsa_dec_attend")(
            page_table, q, selbias, o_cmp, gates, newsel, winbuf, newwin, slopes, saug, pos_rows, blk_rows,
            *([cache] * pg))


def _pad_t(a, axis, to):
    pad = [(0, 0)] * a.ndim
    pad[axis] = (0, to - a.shape[axis])
    return jnp.pad(a, pad)


def _even_layer(x, b, t, pos0, pool_buf, c0, n0, m0, g, w_in, b_gate, w_pool, pool_scale, w_out):
    n = b * t
    u, q, k, v, og, vt, gc, gr = _even_in(x, g, w_in, b_gate)
    u3 = u.reshape(b, t, POOL_WIDTH)
    new_pool = jnp.concatenate([pool_buf, u3], axis=1)[:, -pool_buf.shape[1]:]
    h4 = ML_HEADS
    if t % 256 == 0:
        tp, L = t, 256
        vt_in, gr_in = vt, gr
    else:
        tp = L = -(-t // 8) * 8
        vt_in = _pad_t(vt.reshape(h4, ML_DH, b, t), 3, tp).transpose(0, 2, 1, 3)
        gr_in = _pad_t(gr.reshape(2 * h4, b, t), 2, tp).transpose(1, 0, 2)
    seq = lambda a: _pad_t(a.reshape(h4, b, t, ML_DH), 2, tp)
    ypool = _pool(_pad_t(u3, 1, tp), pool_buf, w_pool, pool_scale, pos0)[:, :t].reshape(n, POOL_WIDTH)
    hm, c, nn, m = _mlstm(seq(q), seq(k), seq(v), vt_in, _pad_t(gc.reshape(b, t, 2 * h4), 1, tp), gr_in,
                          c0, n0, m0, L=L, t_valid=t)
    hm = hm[:, :, :t].reshape(h4, n, ML_DH)
    x = _even_out(x, ypool, og, hm, w_out)
    return x, new_pool, c, nn.reshape(b, h4, ML_DH), m.reshape(b, h4)


def _odd_layer(x, b, t, pos0, cache, page_table, win_buf, conv_buf, g, w_in, cmp_pos_w, cmp_w, cmp_b,
               conv_w, w_out):
    n = b * t
    ncol = 4 * HEAD_DIM
    prompt = cache is None
    q, rows_t, win_t, gates, ucv, bg = _odd_in(x, g, w_in, t if prompt else n)
    u3 = ucv.reshape(b, t, CONV_CH)
    new_conv = jnp.concatenate([conv_buf, u3], axis=1)[:, -(CONV_K - 1):]
    if prompt:
        kvc_t = _compress_prompt(rows_t, cmp_pos_w, cmp_w, cmp_b)
        o = _nsa_prompt(q.reshape(NSA_HEADS, b, t, HEAD_DIM), gates.reshape(b, t, 3 * NSA_HEADS), kvc_t,
                        rows_t, win_t)
        new_win_t = win_t[:, :, t - min(WINDOW, t):]
        new_rows = rows_t.reshape(b, 4, KV_HEADS, HEAD_DIM, t).transpose(0, 4, 1, 2, 3)
        tp = t
    else:
        assert pos0 % SEL_BLOCK + t <= SEL_BLOCK
        tp = -(-t // 8) * 8
        rows_bt = rows_t[0].T.reshape(b, t, 2 * ncol)
        win_bt = win_t[0].T.reshape(b, t, ncol)
        n_pool = cache.shape[0]
        cache_t = cache.transpose(0, 2, 3, 4, 1).reshape(n_pool, 2 * ncol, PAGE_SIZE)
        kvc_t = _compress_paged(cache_t, page_table, cmp_pos_w, cmp_w, cmp_b)
        qp = _pad_t(q.reshape(NSA_HEADS, b, t, HEAD_DIM), 2, tp)
        n_blocks = -(-(pos0 + t) // SEL_BLOCK)
        step_blocks = min(DEC_PAGES, page_table.shape[1]) * PAGE_SIZE // SEL_BLOCK
        o_cmp, selmask = _nsa_dec_select(qp, kvc_t, pos0, n_blocks, step_blocks)
        wlen = win_buf.shape[1]
        winb_t = win_buf.transpose(0, 2, 3, 4, 1).reshape(b, ncol, wlen)
        o = _nsa_dec_attend(qp, selmask, o_cmp, _pad_t(gates.reshape(b, t, -1), 1, tp),
                            _pad_t(rows_bt[:, :, ncol:], 1, tp), winb_t, _pad_t(win_bt, 1, tp),
                            cache_t, page_table, pos0, t)[:, :t]
        new_win_t = jnp.concatenate([winb_t, win_bt.transpose(0, 2, 1)], axis=2)[:, :, -wlen:]
        new_rows = rows_bt.reshape(b, t, 4, KV_HEADS, HEAD_DIM)
    yconv = _conv(_pad_t(u3, 1, tp), _pad_t(bg.reshape(b, t, CONV_CH), 1, tp), conv_buf, conv_w)[:, :t]
    x = _odd_out(x, o.reshape(n, NSA_WIDTH), yconv.reshape(n, CONV_CH), w_out)
    new_win = new_win_t.reshape(b, 2, KV_HEADS, HEAD_DIM, new_win_t.shape[2]).transpose(0, 4, 1, 2, 3)
    return x, new_rows, new_win, new_conv


def _trunk(x3, pos0, pool_buf, ml_c, ml_n, ml_m, kv_cache, page_table, win_buf, conv_buf, p):
    b, t, d = x3.shape
    depth = p["norm_g"].shape[0]
    x = x3.reshape(b * t, d)
    pools, cs, ns, ms, rows, wins, convs = [], [], [], [], [], [], []
    for l in range(depth):
        j = l // 2
        x = _ffn(x, p["norm_g"][l, 0], p["w_ffn_in"][l][0], p["w_ffn_out"][l][0])
        if l % 2 == 0:
            x, pb, c, n, m = _even_layer(x, b, t, pos0, pool_buf[j], ml_c[j], ml_n[j], ml_m[j], p["norm_g"][l, 1],
                                         p["w_in_even"][j], p["b_gate_even"][j], p["w_pool"][j],
                                         p["pool_scale"][j], p["w_out_even"][j])
            pools.append(pb)
            cs.append(c)
            ns.append(n)
            ms.append(m)
        else:
            cache = None if kv_cache is None else kv_cache[j]
            wb = None if win_buf is None else win_buf[j]
            x, r, wn, cn = _odd_layer(x, b, t, pos0, cache, page_table, wb, conv_buf[j], p["norm_g"][l, 1],
                                      p["w_in_odd"][j], p["cmp_pos_w"][j], p["cmp_w"][j], p["cmp_b"][j],
                                      p["conv_w"][j], p["w_out_odd"][j])
            rows.append(r)
            wins.append(wn)
            convs.append(cn)
        x = _ffn(x, p["norm_g"][l, 2], p["w_ffn_in"][l][1], p["w_ffn_out"][l][1],
                 final_g=p["final_g"] if l == depth - 1 else None)
    states = (jnp.stack(pools), jnp.stack(cs), jnp.stack(ns), jnp.stack(ms),
              jnp.stack(rows), jnp.stack(wins), jnp.stack(convs))
    return x.reshape(b, t, d), states


def kernel(x_prompt, x_sample, state_pool, state_mlstm_c, state_mlstm_n, state_mlstm_m, cache_nsa_kv, state_win_kv, state_conv, page_table, norm_g, final_g, w_ffn_in, w_ffn_out, w_in_even, b_gate_even, w_pool, pool_scale, w_out_even, w_in_odd, cmp_pos_w, cmp_w, cmp_b, conv_w, w_out_odd):
    bp = x_prompt.shape[0]
    n_even, n_odd = state_pool.shape[0], state_conv.shape[0]
    past_len = page_table.shape[1] * PAGE_SIZE
    p = dict(norm_g=norm_g, final_g=final_g, w_ffn_in=w_ffn_in.astype(BF16), w_ffn_out=w_ffn_out.astype(BF16),
             w_in_even=w_in_even, b_gate_even=b_gate_even, w_pool=w_pool, pool_scale=pool_scale,
             w_out_even=w_out_even, w_in_odd=w_in_odd, cmp_pos_w=cmp_pos_w, cmp_w=cmp_w, cmp_b=cmp_b,
             conv_w=conv_w, w_out_odd=w_out_odd)
    pool0 = jnp.zeros((n_even, bp) + state_pool.shape[2:], F32)
    c0 = jnp.zeros((n_even, bp) + state_mlstm_c.shape[2:], F32)
    n0 = jnp.zeros((n_even, bp) + state_mlstm_n.shape[2:], F32)
    m0 = jnp.zeros((n_even, bp) + state_mlstm_m.shape[2:], F32)
    conv0 = jnp.zeros((n_odd, bp) + state_conv.shape[2:], F32)
    y_p, (pool_p, c_p, n_p, m_p, kv_p, win_p, conv_p) = _trunk(
        x_prompt, 0, pool0, c0, n0, m0, None, None, None, conv0, p)
    y_s, (pool_s, c_s, n_s, m_s, kv_s, win_s, conv_s) = _trunk(
        x_sample, past_len, state_pool, state_mlstm_c, state_mlstm_n, state_mlstm_m,
        cache_nsa_kv, page_table, state_win_kv, state_conv, p)
    return (y_p, y_s, pool_p, pool_s, c_p, c_s, n_p, n_s, m_p, m_s,
            kv_p, kv_s, win_p, win_s, conv_p, conv_s)
```

```python
import functools

import numpy as np
import jax
import jax.numpy as jnp
from jax import lax
from jax.experimental import pallas as pl
from jax.experimental.pallas import tpu as pltpu

F32 = jnp.float32
BF16 = jnp.bfloat16
HIGHEST = lax.Precision.HIGHEST

EPS = 1e-6
POOL_WINDOWS = (2, 4, 8, 16)
POOL_GDIM = 64
POOL_WIDTH = 256
POOL_HALO = 16
ML_HEADS = 4
ML_DH = 192
ML_WIDTH = ML_HEADS * ML_DH
NSA_HEADS = 12
HEAD_DIM = 64
NSA_WIDTH = NSA_HEADS * HEAD_DIM
KV_HEADS = 2
GQ = NSA_HEADS // KV_HEADS
CMP_STRIDE = 16
SEL_BLOCK = 64
SEL_TOPN = 16
WINDOW = 512
Q_BLOCK = 64
PAGE_SIZE = 128
CONV_CH = 256
CONV_K = 3
CONV_HALO = 8

VMEM_LIMIT = 56 * 1024 * 1024
NEG = -1e30
LOG2E = 1.4426950408889634


def _params(sem, vmem=VMEM_LIMIT):
    return pltpu.CompilerParams(dimension_semantics=sem, vmem_limit_bytes=vmem)


def _const_spec(shape):
    nd = len(shape)
    return pl.BlockSpec(shape, lambda *_: (0,) * nd, pipeline_mode=pl.Buffered(1))


def _rms(x, g):
    return x * lax.rsqrt(jnp.mean(x * x, axis=-1, keepdims=True) + EPS) * g


def _dot(a, b):
    return jnp.dot(a, b, preferred_element_type=F32)


def _dot_nt(a, b):
    return lax.dot_general(a, b, (((1,), (1,)), ((), ())), preferred_element_type=F32)


def _ffn_body(x_ref, g_ref, win_ref, wout_ref, *rest, d_ff, chunks, has_final):
    o_ref = rest[-1]
    x = x_ref[...]
    hn = _rms(x, g_ref[...]).astype(BF16)
    acc = jnp.zeros(x.shape, F32)
    off = 0
    for fc in chunks:
        a = _dot(hn, win_ref[:, off:off + fc])
        b = _dot(hn, win_ref[:, d_ff + off:d_ff + off + fc])
        act = (a * jax.nn.sigmoid(a) * b).astype(BF16)
        acc = acc + _dot(act, wout_ref[off:off + fc, :])
        off += fc
    y = x + 0.5 * acc
    if has_final:
        y = _rms(y, rest[0][...])
    o_ref[...] = y


def _ffn(x, g, w_in, w_out, final_g=None):
    n, d = x.shape
    d_ff = w_out.shape[0]
    tm = 512 if n % 512 == 0 else n
    chunks, left = [], d_ff
    while left:
        chunks.append(min(1024, left))
        left -= chunks[-1]
    row = pl.BlockSpec((tm, d), lambda i: (i, 0))
    in_specs = [row, _const_spec((1, d)), _const_spec(w_in.shape), _const_spec(w_out.shape)]
    args = [x, g.reshape(1, d), w_in, w_out]
    if final_g is not None:
        in_specs.append(_const_spec((1, d)))
        args.append(final_g.reshape(1, d))
    return pl.pallas_call(
        functools.partial(_ffn_body, d_ff=d_ff, chunks=tuple(chunks), has_final=final_g is not None),
        grid=(n // tm,), in_specs=in_specs, out_specs=row,
        out_shape=jax.ShapeDtypeStruct((n, d), F32),
        compiler_params=_params(("parallel",)), name="ffn")(*args)


def _even_in_body(x_ref, g_ref, wu_ref, wh_ref, wvt_ref, wg_ref, wgt_ref, bg_ref, bgt_ref,
                  u_ref, q_ref, k_ref, v_ref, og_ref, vt_ref, gc_ref, gr_ref):
    hn = _rms(x_ref[...], g_ref[...]).astype(BF16)
    u_ref[...] = _dot(hn, wu_ref[...])
    for seg, out_ref in enumerate((q_ref, k_ref, v_ref, og_ref)):
        z = _dot(hn, wh_ref[seg])
        for h in range(ML_HEADS):
            out_ref[h] = z[:, h * ML_DH:(h + 1) * ML_DH]
    for h in range(ML_HEADS):
        vt_ref[h] = _dot_nt(wvt_ref[h], hn)
    gc_ref[...] = _dot(hn, wg_ref[...]) + bg_ref[...]
    gr_ref[...] = _dot_nt(wgt_ref[...], hn) + bgt_ref[...]


def _even_in(x, g, w_in, b_gate):
    n, d = x.shape
    tm = 512 if n % 512 == 0 else n
    h4 = ML_HEADS
    wu = w_in[:, :POOL_WIDTH].astype(BF16)
    wh = w_in[:, POOL_WIDTH:POOL_WIDTH + 4 * ML_WIDTH].reshape(d, 4, ML_WIDTH).transpose(1, 0, 2).astype(BF16)
    wvt = wh[2].reshape(d, h4, ML_DH).transpose(1, 2, 0)
    wg = w_in[:, POOL_WIDTH + 4 * ML_WIDTH:].astype(BF16)
    row = lambda w: pl.BlockSpec((tm, w), lambda i: (i, 0))
    hrow = pl.BlockSpec((h4, tm, ML_DH), lambda i: (0, i, 0))
    hsh = jax.ShapeDtypeStruct((h4, n, ML_DH), F32)
    return pl.pallas_call(
        _even_in_body, grid=(n // tm,),
        in_specs=[row(d), _const_spec((1, d)), _const_spec(wu.shape), _const_spec(wh.shape),
                  _const_spec(wvt.shape), _const_spec(wg.shape), _const_spec((2 * h4, d)),
                  _const_spec((1, 2 * h4)), _const_spec((2 * h4, 1))],
        out_specs=[row(POOL_WIDTH), hrow, hrow, hrow, hrow,
                   pl.BlockSpec((h4, ML_DH, tm), lambda i: (0, 0, i)),
                   row(2 * h4), pl.BlockSpec((2 * h4, tm), lambda i: (0, i))],
        out_shape=[jax.ShapeDtypeStruct((n, POOL_WIDTH), F32), hsh, hsh, hsh, hsh,
                   jax.ShapeDtypeStruct((h4, ML_DH, n), F32),
                   jax.ShapeDtypeStruct((n, 2 * h4), F32), jax.ShapeDtypeStruct((2 * h4, n), F32)],
        compiler_params=_params(("parallel",)), name="even_in")(
            x, g.reshape(1, d), wu, wh, wvt, wg, wg.T, b_gate.reshape(1, -1), b_gate.reshape(-1, 1))


def _pool_body(u_ref, pre_ref, w_ref, sc_ref, y_ref, carry, full, *, tb, pos0):
    t = pl.program_id(1)

    @pl.when(t == 0)
    def _():
        carry[...] = pre_ref[...]

    u = u_ref[...]
    full[0:POOL_HALO] = carry[...]
    full[POOL_HALO:] = u
    acc = full[...]
    sums = []
    for sh in (1, 2, 4, 8):
        acc = acc + pltpu.roll(acc, sh, 0)
        sums.append(acc[POOL_HALO:])
    lane = lax.broadcasted_iota(jnp.int32, (tb, POOL_WIDTH), 1)
    grp = lane // POOL_GDIM
    win = jnp.where(grp == 0, sums[0], jnp.where(grp == 1, sums[1], jnp.where(grp == 2, sums[2], sums[3])))
    width = jnp.where(grp == 0, 2, jnp.where(grp == 1, 4, jnp.where(grp == 2, 8, 16)))
    pos = pos0 + t * tb + lax.broadcasted_iota(jnp.int32, (tb, POOL_WIDTH), 0)
    cnt = jnp.minimum(pos + 1, width).astype(F32)
    mixed = (win / cnt - u).astype(BF16)
    y_ref[...] = _dot(mixed, w_ref[...]) * sc_ref[...]
    carry[...] = full[tb:tb + POOL_HALO]


def _pool(u, prefix, w_pool, scale, pos0):
    b, t, c = u.shape
    tb = 512 if t % 512 == 0 else t
    pre = jnp.pad(prefix, ((0, 0), (POOL_HALO - prefix.shape[1], 0), (0, 0)))
    wbd = jax.scipy.linalg.block_diag(*[w_pool[i] for i in range(w_pool.shape[0])]).astype(BF16)
    return pl.pallas_call(
        functools.partial(_pool_body, tb=tb, pos0=pos0), grid=(b, t // tb),
        in_specs=[pl.BlockSpec((None, tb, c), lambda i, j: (i, j, 0)),
                  pl.BlockSpec((None, POOL_HALO, c), lambda i, j: (i, 0, 0)),
                  _const_spec((c, c)), _const_spec((1, c))],
        out_specs=pl.BlockSpec((None, tb, c), lambda i, j: (i, j, 0)),
        out_shape=jax.ShapeDtypeStruct((b, t, c), F32),
        scratch_shapes=[pltpu.VMEM((POOL_HALO, c), F32), pltpu.VMEM((tb + POOL_HALO, c), F32)],
        compiler_params=_params(("parallel", "arbitrary")), name="pool")(u, pre, wbd, scale.reshape(1, c))


def _log_sigmoid(x):
    return jnp.minimum(x, 0.0) - jnp.log(1.0 + jnp.exp(-jnp.abs(x)))


def _mlstm_body(q_ref, k_ref, v_ref, vt_ref, gc_ref, gr_ref, c0_ref, n0_ref, m0_ref,
                h_ref, c_ref, n_ref, m_ref, *, L, t_valid):
    @pl.when(pl.program_id(1) == 0)
    def _():
        c_ref[...] = c0_ref[...]
        n_ref[...] = n0_ref[...]
        m_ref[...] = m0_ref[...]

    gc = gc_ref[...]
    gr = gr_ref[...]
    lf_c = _log_sigmoid(gc)
    lf_r = _log_sigmoid(gr)
    row = lax.broadcasted_iota(jnp.int32, (L, L), 0)
    col = lax.broadcasted_iota(jnp.int32, (L, L), 1)
    tok_c = lax.broadcasted_iota(jnp.int32, (L, 1), 0) < t_valid
    tok_r = lax.broadcasted_iota(jnp.int32, (1, L), 1) < t_valid
    if t_valid < L:
        lf_c = jnp.where(tok_c, lf_c, 0.0)
        lf_r = jnp.where(tok_r, lf_r, 0.0)
    causal = row >= col
    tri = causal.astype(F32)
    cs_c = jnp.dot(tri, lf_c, precision=HIGHEST, preferred_element_type=F32)
    cs_r = lax.dot_general(lf_r, tri, (((1,), (1,)), ((), ())), precision=HIGHEST,
                           preferred_element_type=F32)
    for hd in range(ML_HEADS):
        i_c = gc[:, hd:hd + 1]
        i_r = gr[hd:hd + 1, :]
        if t_valid < L:
            i_c = jnp.where(tok_c, i_c, -jnp.inf)
            i_r = jnp.where(tok_r, i_r, -jnp.inf)
        b_c = cs_c[:, ML_HEADS + hd:ML_HEADS + hd + 1]
        b_r = cs_r[ML_HEADS + hd:ML_HEADS + hd + 1, :]
        m_prev = m_ref[hd]
        dmat = jnp.where(causal, b_c - b_r + i_r, -jnp.inf)
        inter = b_c + m_prev
        mt = jnp.maximum(jnp.max(dmat, axis=1, keepdims=True), inter)
        w = jnp.exp(dmat - mt)
        a = jnp.exp(inter - mt)
        q = q_ref[hd]
        kf = k_ref[hd] * (ML_DH ** -0.5)
        qb = q.astype(BF16)
        kb = kf.astype(BF16)
        s = _dot_nt(qb, kb) * w
        c_old = c_ref[hd]
        n_old = n_ref[hd]
        num = _dot(s.astype(BF16), v_ref[hd].astype(BF16)) + a * _dot_nt(qb, c_old.astype(BF16))
        den = jnp.sum(s, axis=1, keepdims=True) + a * jnp.sum(q * n_old, axis=1, keepdims=True)
        h_ref[hd] = num / jnp.maximum(jnp.abs(den), jnp.exp(-mt))
        b_last = b_c[L - 1:L, :]
        ge_r = b_last - b_r + i_r
        ge_c = b_last - b_c + i_c
        m_new = jnp.maximum(b_last + m_prev, jnp.max(ge_r, axis=1, keepdims=True))
        wk_r = jnp.exp(ge_r - m_new)
        wk_c = jnp.exp(ge_c - m_new)
        decay = jnp.exp(b_last + m_prev - m_new)
        c_ref[hd] = decay * c_old + _dot((vt_ref[hd] * wk_r).astype(BF16), kb)
        n_ref[hd] = decay * n_old + jnp.sum(kf * wk_c, axis=0, keepdims=True)
        m_ref[hd] = m_new


def _mlstm(q, k, v, vt, gc, gr, c0, n0, m0, *, L, t_valid):
    h4, b, t, dh = q.shape
    nc = t // L
    tok = pl.BlockSpec((h4, None, L, dh), lambda i, c: (0, i, c, 0))
    if vt.ndim == 4:
        vt_spec = pl.BlockSpec((h4, None, dh, L), lambda i, c: (0, i, 0, c))
        gr_spec = pl.BlockSpec((None, 2 * h4, L), lambda i, c: (i, 0, c))
    else:
        vt_spec = pl.BlockSpec((h4, dh, L), lambda i, c: (0, 0, i * nc + c))
        gr_spec = pl.BlockSpec((2 * h4, L), lambda i, c: (0, i * nc + c))
    st = lambda r, w: pl.BlockSpec((None, h4, r, w), lambda i, c: (i, 0, 0, 0))
    return pl.pallas_call(
        functools.partial(_mlstm_body, L=L, t_valid=t_valid), grid=(b, nc),
        in_specs=[tok, tok, tok, vt_spec, pl.BlockSpec((None, L, 2 * h4), lambda i, c: (i, c, 0)), gr_spec,
                  st(dh, dh), st(1, dh), st(1, 1)],
        out_specs=[tok, st(dh, dh), st(1, dh), st(1, 1)],
        out_shape=[jax.ShapeDtypeStruct((h4, b, t, dh), F32), jax.ShapeDtypeStruct((b, h4, dh, dh), F32),
                   jax.ShapeDtypeStruct((b, h4, 1, dh), F32), jax.ShapeDtypeStruct((b, h4, 1, 1), F32)],
        compiler_params=_params(("parallel", "arbitrary")), name="mlstm")(
            q, k, v, vt, gc, gr, c0, n0.reshape(b, h4, 1, dh), m0.reshape(b, h4, 1, 1))


def _even_out_body(x_ref, yp_ref, og_ref, hm_ref, w0_ref, w1_ref, o_ref):
    acc = _dot(yp_ref[...].astype(BF16), w0_ref[...])
    for h in range(ML_HEADS):
        acc = acc + _dot((jax.nn.sigmoid(og_ref[h]) * hm_ref[h]).astype(BF16), w1_ref[h])
    o_ref[...] = x_ref[...] + acc


def _even_out(x, ypool, og, hm, w_out):
    n, d = x.shape
    tm = 512 if n % 512 == 0 else n
    w0 = w_out[:POOL_WIDTH].astype(BF16)
    w1 = w_out[POOL_WIDTH:].reshape(ML_HEADS, ML_DH, d).astype(BF16)
    row = lambda w: pl.BlockSpec((tm, w), lambda i: (i, 0))
    hrow = pl.BlockSpec((ML_HEADS, tm, ML_DH), lambda i: (0, i, 0))
    return pl.pallas_call(
        _even_out_body, grid=(n // tm,),
        in_specs=[row(d), row(POOL_WIDTH), hrow, hrow, _const_spec(w0.shape), _const_spec(w1.shape)],
        out_specs=row(d), out_shape=jax.ShapeDtypeStruct((n, d), F32),
        compiler_params=_params(("parallel",)), name="even_out")(x, ypool, og, hm, w0, w1)


def _odd_in_body(x_ref, g_ref, wq_ref, wkvt_ref, wgt_ref, wc_ref,
                 q_ref, rows_ref, win_ref, gates_ref, ucv_ref, bg_ref):
    hn = _rms(x_ref[...], g_ref[...]).astype(BF16)
    q_ref[...] = _dot(hn, wq_ref[...]) * (LOG2E * HEAD_DIM ** -0.5)
    nrow = rows_ref.shape[0]
    rows_ref[...] = _dot_nt(wkvt_ref[:nrow], hn)
    win_ref[...] = _dot_nt(wkvt_ref[nrow:], hn)
    gates_ref[...] = jax.nn.sigmoid(_dot(hn, wgt_ref[...]))
    bg_ref[...] = _dot(hn, wc_ref[:, :CONV_CH])
    ucv_ref[...] = _dot(hn, wc_ref[:, CONV_CH:2 * CONV_CH]) * _dot(hn, wc_ref[:, 2 * CONV_CH:])


def _odd_in(x, g, w_in, seq):
    n, d = x.shape
    tm = 512 if seq % 512 == 0 else seq
    nt = seq // tm
    kvw = 6 * KV_HEADS * HEAD_DIM
    ngt = 3 * NSA_HEADS
    wq = w_in[:, :NSA_WIDTH].astype(BF16)
    wkvt = w_in[:, NSA_WIDTH:NSA_WIDTH + kvw].T.astype(BF16)
    wgt = w_in[:, NSA_WIDTH + kvw:NSA_WIDTH + kvw + ngt].astype(BF16)
    wc = w_in[:, NSA_WIDTH + kvw + ngt:].astype(BF16)
    nrow = 4 * KV_HEADS * HEAD_DIM
    row = lambda w: pl.BlockSpec((tm, w), lambda i: (i, 0))
    slab = lambda r: pl.BlockSpec((None, r, tm), lambda i: (i // nt, 0, i % nt))
    sh = lambda w: jax.ShapeDtypeStruct((n, w), F32)
    return pl.pallas_call(
        _odd_in_body, grid=(n // tm,),
        in_specs=[row(d), _const_spec((1, d)), _const_spec(wq.shape), _const_spec(wkvt.shape),
                  _const_spec(wgt.shape), _const_spec(wc.shape)],
        out_specs=[row(NSA_WIDTH),
                   slab(nrow), slab(kvw - nrow), row(ngt), row(CONV_CH), row(CONV_CH)],
        out_shape=[sh(NSA_WIDTH),
                   jax.ShapeDtypeStruct((n // seq, nrow, seq), F32),
                   jax.ShapeDtypeStruct((n // seq, kvw - nrow, seq), F32),
                   sh(ngt), sh(CONV_CH), sh(CONV_CH)],
        compiler_params=_params(("parallel",)), name="odd_in")(x, g.reshape(1, d), wq, wkvt, wgt, wc)


def _conv_body(u_ref, bg_ref, pre_ref, w_ref, y_ref, carry, full, *, tb):
    @pl.when(pl.program_id(1) == 0)
    def _():
        carry[...] = pre_ref[...]

    full[0:CONV_HALO] = carry[...]
    full[CONV_HALO:] = u_ref[...]
    f = full[...]
    w = w_ref[...]
    conv = f * w[2:3] + pltpu.roll(f, 1, 0) * w[1:2] + pltpu.roll(f, 2, 0) * w[0:1]
    y_ref[...] = bg_ref[...] * conv[CONV_HALO:]
    carry[...] = full[tb:tb + CONV_HALO]


def _conv(u, bg, prefix, conv_w):
    b, t, c = u.shape
    tb = 512 if t % 512 == 0 else t
    pre = jnp.pad(prefix, ((0, 0), (CONV_HALO - prefix.shape[1], 0), (0, 0)))
    blk = pl.BlockSpec((None, tb, c), lambda i, j: (i, j, 0))
    return pl.pallas_call(
        functools.partial(_conv_body, tb=tb), grid=(b, t // tb),
        in_specs=[blk, blk, pl.BlockSpec((None, CONV_HALO, c), lambda i, j: (i, 0, 0)), _const_spec((CONV_K, c))],
        out_specs=blk, out_shape=jax.ShapeDtypeStruct((b, t, c), F32),
        scratch_shapes=[pltpu.VMEM((CONV_HALO, c), F32), pltpu.VMEM((tb + CONV_HALO, c), F32)],
        compiler_params=_params(("parallel", "arbitrary")), name="conv")(u, bg, pre, conv_w)


def _odd_out_body(x_ref, o_ref, yc_ref, w0_ref, w1_ref, out_ref):
    out_ref[...] = (x_ref[...] + _dot(o_ref[...].astype(BF16), w0_ref[...])
                    + _dot(yc_ref[...].astype(BF16), w1_ref[...]))


def _odd_out(x, o, yconv, w_out):
    n, d = x.shape
    tm = 512 if n % 512 == 0 else n
    w0 = w_out[:NSA_WIDTH].astype(BF16)
    w1 = w_out[NSA_WIDTH:].astype(BF16)
    row = lambda w: pl.BlockSpec((tm, w), lambda i: (i, 0))
    return pl.pallas_call(
        _odd_out_body, grid=(n // tm,),
        in_specs=[row(d), row(NSA_WIDTH), row(CONV_CH), _const_spec(w0.shape), _const_spec(w1.shape)],
        out_specs=row(d), out_shape=jax.ShapeDtypeStruct((n, d), F32),
        compiler_params=_params(("parallel",)), name="odd_out")(x, o, yconv, w0, w1)


CMP_TILE = 2048
CMP_ROWS = 4 * HEAD_DIM


def _cmp_weights(cmp_pos_w, cmp_w, cmp_b, tile):
    m = tile // CMP_STRIDE
    cols = -(-(m + 1) // 128) * 128
    pos = jnp.arange(tile)
    chunk = (pos // CMP_STRIDE)[:, None]
    col = jnp.arange(cols)[None, :]
    mats = []
    for kv in range(2):
        wa = cmp_pos_w[kv, :CMP_STRIDE][pos % CMP_STRIDE][:, None]
        wb = cmp_pos_w[kv, CMP_STRIDE:][pos % CMP_STRIDE][:, None]
        mats.append(jnp.where(chunk == col, wb, 0.0) + jnp.where(chunk == col - 1, wa, 0.0))
    w2 = jnp.stack(mats).astype(BF16)
    wbdt = jax.scipy.linalg.block_diag(cmp_w[0], cmp_w[0], cmp_w[1], cmp_w[1]).T.astype(BF16)
    biast = jnp.concatenate([cmp_b[0], cmp_b[0], cmp_b[1], cmp_b[1]]).reshape(-1, 1)
    return w2, wbdt, biast


def _compress_body(*refs, n_pages, n_prefetch):
    refs = refs[n_prefetch:]
    pages = refs[:n_pages]
    w2_ref, wbdt_ref, biast_ref, out_ref, carry = refs[n_pages:]
    m_out = out_ref.shape[1]

    @pl.when(pl.program_id(1) == 0)
    def _():
        carry[...] = jnp.zeros(carry.shape, F32)

    x = jnp.concatenate([pg[...] for pg in pages], axis=1).astype(BF16)
    half = CMP_ROWS // 2
    res = jnp.concatenate([_dot(x[:half], w2_ref[0]), _dot(x[half:], w2_ref[1])], axis=0)
    first = lax.broadcasted_iota(jnp.int32, (CMP_ROWS, m_out), 1) == 0
    pre = res[:, :m_out] + jnp.where(first, carry[...], 0.0)
    carry[...] = jnp.broadcast_to(res[:, m_out:m_out + 1], carry.shape)
    out_ref[...] = _dot(wbdt_ref[...], pre.astype(BF16)) + biast_ref[...]


def _compress_prompt(rows_t, cmp_pos_w, cmp_w, cmp_b):
    b, _, t = rows_t.shape
    tile = min(CMP_TILE, t)
    assert t % tile == 0
    w2, wbdt, biast = _cmp_weights(cmp_pos_w, cmp_w, cmp_b, tile)
    m_out = tile // CMP_STRIDE
    return pl.pallas_call(
        functools.partial(_compress_body, n_pages=1, n_prefetch=0), grid=(b, t // tile),
        in_specs=[pl.BlockSpec((None, CMP_ROWS, tile), lambda i, j: (i, 0, j)),
                  _const_spec(w2.shape), _const_spec(wbdt.shape), _const_spec(biast.shape)],
        out_specs=pl.BlockSpec((None, CMP_ROWS, m_out), lambda i, j: (i, 0, j)),
        out_shape=jax.ShapeDtypeStruct((b, CMP_ROWS, t // CMP_STRIDE), F32),
        scratch_shapes=[pltpu.VMEM((CMP_ROWS, m_out), F32)],
        compiler_params=_params(("parallel", "arbitrary")), name="compress_prompt")(rows_t, w2, wbdt, biast)


def _compress_paged(cache_t, page_table, cmp_pos_w, cmp_w, cmp_b):
    b, npg = page_table.shape
    pg = min(CMP_TILE // PAGE_SIZE, npg)
    assert npg % pg == 0
    tile = pg * PAGE_SIZE
    w2, wbdt, biast = _cmp_weights(cmp_pos_w, cmp_w, cmp_b, tile)
    m_out = tile // CMP_STRIDE
    specs = [pl.BlockSpec((None, CMP_ROWS, PAGE_SIZE),
                          functools.partial(lambda i, j, pt, p: (pt[i, j * pg + p], 0, 0), p=p))
             for p in range(pg)]
    return pl.pallas_call(
        functools.partial(_compress_body, n_pages=pg, n_prefetch=1),
        grid_spec=pltpu.PrefetchScalarGridSpec(
            num_scalar_prefetch=1, grid=(b, npg // pg),
            in_specs=specs + [_const_spec(w2.shape), _const_spec(wbdt.shape), _const_spec(biast.shape)],
            out_specs=pl.BlockSpec((None, CMP_ROWS, m_out), lambda i, j, pt: (i, 0, j)),
            scratch_shapes=[pltpu.VMEM((CMP_ROWS, m_out), F32)]),
        out_shape=jax.ShapeDtypeStruct((b, CMP_ROWS, npg * PAGE_SIZE // CMP_STRIDE), F32),
        compiler_params=_params(("parallel", "arbitrary")), name="compress_paged")(
            page_table, *([cache_t] * pg), w2, wbdt, biast)


def _alibi_slopes(rows_per_head):
    sl = (LOG2E * 2.0 ** (-8.0 * np.arange(1, NSA_HEADS + 1) / NSA_HEADS)).astype(np.float32).reshape(KV_HEADS, GQ)
    return jnp.asarray(np.repeat(sl, rows_per_head, axis=1)[:, :, None])


def _score_matrix(n_cmp_rows, n_blocks, lanes):
    m = np.arange(n_cmp_rows)[:, None]
    j = np.arange(lanes)[None, :]
    return jnp.asarray(((m >= 4 * j) & (m <= 4 * j + 4) & (m >= 1) & (j < n_blocks)).astype(np.float32))


def _group_queries(q_ref, g):
    heads = [q_ref[:, (g * GQ + i) * HEAD_DIM:(g * GQ + i + 1) * HEAD_DIM] for i in range(GQ)]
    return jnp.concatenate(heads, axis=0).astype(BF16)


def _softmax_rows(s, mask):
    s = jnp.where(mask, s, -jnp.inf)
    m = jnp.max(s, axis=-1, keepdims=True)
    m = jnp.where(m > -jnp.inf, m, 0.0)
    e = jnp.exp2(s - m)
    return e / jnp.maximum(jnp.sum(e, axis=-1, keepdims=True), 1e-30)


def _select_blocks(score, cur, n_blocks):
    r, lanes = score.shape
    blk = lax.broadcasted_iota(jnp.int32, (r, lanes), 1)
    forced = (blk == 0) | (blk == cur) | (blk == cur - 1)
    val = jnp.where(forced, jnp.inf, jnp.where(blk <= cur, score, -jnp.inf))
    rank = jnp.zeros((r, lanes), jnp.int32)
    for i in range(n_blocks):
        ci = val[:, i:i + 1]
        ahead = (ci > val) | ((ci == val) & (blk > i))
        rank = rank + ahead.astype(jnp.int32)
    return ((rank < min(SEL_TOPN, n_blocks)) & (blk < n_blocks)).astype(F32)


POS_ROWS = 16


def _slope_pieces(rows_per_head):
    sl = _alibi_slopes(rows_per_head)
    hi = sl.astype(BF16)
    mid = (sl - hi.astype(F32)).astype(BF16)
    lo = (sl - hi.astype(F32) - mid.astype(F32)).astype(BF16)
    pad = jnp.zeros(sl.shape[:2] + (POS_ROWS - 6,), BF16)
    return jnp.concatenate([hi, mid, lo, hi, mid, lo, pad], axis=2)


def _pos_rows(t_len):
    k = np.arange(t_len)
    hi = (SEL_BLOCK * (k // SEL_BLOCK)).astype(np.float32)
    lo = (k % SEL_BLOCK).astype(np.float32)
    return np.stack([hi, hi, hi, lo, lo, lo] + [np.zeros(t_len, np.float32)] * (POS_ROWS - 6))


def _block_rows(n_rows, n_keys):
    return (np.arange(n_rows)[:, None] == (np.arange(n_keys) // SEL_BLOCK)[None, :]).astype(np.float32)


def _key_constants(t_len):
    both = np.concatenate([_pos_rows(t_len), _block_rows(t_len // SEL_BLOCK, t_len)], axis=0)
    return jnp.asarray(both).astype(BF16)


def _select_bias_t(score_t, cur):
    n_blocks, r = score_t.shape
    blk = lax.broadcasted_iota(jnp.int32, (n_blocks, r), 0)
    forced = (blk == 0) | (blk == cur) | (blk == cur - 1)
    val = jnp.where(forced, jnp.inf, jnp.where(blk <= cur, score_t, -jnp.inf))
    rank = jnp.zeros((n_blocks, r), F32)
    for i in range(n_blocks):
        ci = val[i:i + 1, :]
        ahead = (ci > val) | ((ci == val) & (blk > i))
        rank = rank + jnp.where(ahead, 1.0, 0.0)
    return jnp.where((rank < min(SEL_TOPN, n_blocks)) & (blk <= cur), 0.0, NEG)


def _nsa_prompt_body(q_ref, gates_ref, kvc_ref, sel_ref, win_ref, slope_ref, saug_ref, kconst_ref,
                     smat_ref, o_ref, need_sc, *, t_len, kc, wb):
    bi = pl.program_id(1)
    p0 = bi * Q_BLOCK
    nrow = GQ * Q_BLOCK
    n_cmp = kvc_ref.shape[1]
    n_blocks = t_len // SEL_BLOCK
    qpos = lax.broadcasted_iota(jnp.int32, (Q_BLOCK, 1), 0) + p0
    tpos = jnp.concatenate([qpos] * GQ, axis=0)
    gates = gates_ref[...]

    qgs = [_group_queries(q_ref, g) for g in range(KV_HEADS)]
    q_pos = [jnp.concatenate([qgs[g], saug_ref[g]], axis=1) for g in range(KV_HEADS)]

    wstart = pl.multiple_of(jnp.clip((p0 - WINDOW) // 128 * 128, 0, t_len - wb), 128)
    wpos = wstart + lax.broadcasted_iota(jnp.int32, (1, wb), 1)
    wmask = (wpos <= tpos) & (wpos > tpos - WINDOW)
    o_wins = []
    for g in range(KV_HEADS):
        vs = (KV_HEADS + g) * HEAD_DIM
        katw = jnp.concatenate([win_ref[g * HEAD_DIM:(g + 1) * HEAD_DIM, pl.ds(wstart, wb)].astype(BF16),
                                kconst_ref[0:POS_ROWS, pl.ds(wstart, wb)]], axis=0)
        sw = jnp.where(wmask, _dot(q_pos[g], katw), -jnp.inf)
        e_win = jnp.exp2(sw - jnp.max(sw, axis=1, keepdims=True))
        vwt = jnp.concatenate([win_ref[vs:vs + HEAD_DIM, pl.ds(wstart, wb)].astype(BF16),
                               jnp.ones((POS_ROWS, wb), BF16)], axis=0)
        acc_w = _dot_nt(e_win.astype(BF16), vwt)
        o_wins.append(acc_w[:, :HEAD_DIM] / acc_w[:, HEAD_DIM:HEAD_DIM + 1])

    cpos = CMP_STRIDE * lax.broadcasted_iota(jnp.int32, (1, n_cmp), 1) + (CMP_STRIDE - 1)
    dist = tpos - cpos
    cmask = (dist >= 0) & (cpos >= 2 * CMP_STRIDE - 1)
    o_cmps, imps = [], []
    for g in range(KV_HEADS):
        ks, vs = g * HEAD_DIM, (KV_HEADS + g) * HEAD_DIM
        s = _dot(qgs[g], kvc_ref[ks:ks + HEAD_DIM, :].astype(BF16)) - slope_ref[g] * dist.astype(F32)
        p_cmp = _softmax_rows(s, cmask)
        o_cmps.append(_dot_nt(p_cmp.astype(BF16), kvc_ref[vs:vs + HEAD_DIM, :].astype(BF16)))
        imps.append(jnp.sum(p_cmp.reshape(GQ, Q_BLOCK, n_cmp), axis=0))
        for i in range(GQ):
            hh = g * GQ + i
            r = slice(i * Q_BLOCK, (i + 1) * Q_BLOCK)
            o_ref[:, hh * HEAD_DIM:(hh + 1) * HEAD_DIM] = (
                gates[:, 3 * hh:3 * hh + 1] * o_cmps[g][r] + gates[:, 3 * hh + 2:3 * hh + 3] * o_wins[g][r])
    score_t = lax.dot_general(smat_ref[...], jnp.concatenate(imps, axis=0), (((1,), (1,)), ((), ())),
                              precision=HIGHEST, preferred_element_type=F32)
    bias_f = _select_bias_t(score_t, bi)
    bias_t = bias_f.astype(BF16)
    eye = (lax.broadcasted_iota(jnp.int32, (Q_BLOCK, Q_BLOCK), 0)
           == lax.broadcasted_iota(jnp.int32, (Q_BLOCK, Q_BLOCK), 1)).astype(BF16)
    biases = [_dot_nt(eye, bias_t[:, g * Q_BLOCK:(g + 1) * Q_BLOCK]) for g in range(KV_HEADS)]
    q_aug = [jnp.concatenate([q_pos[g], jnp.concatenate([biases[g]] * GQ, axis=0).astype(BF16)], axis=1)
             for g in range(KV_HEADS)]

    ones_rows = jnp.ones((POS_ROWS, kc), BF16)

    def chunk_scores(g, c):
        k0 = pl.multiple_of(c * kc, kc)
        kat = jnp.concatenate([sel_ref[g * HEAD_DIM:(g + 1) * HEAD_DIM, pl.ds(k0, kc)].astype(BF16),
                               kconst_ref[:, pl.ds(k0, kc)]], axis=0)
        return _dot(q_aug[g], kat), k0

    def accumulate(g, s, k0, carry):
        m_i, acc = carry
        vs = (KV_HEADS + g) * HEAD_DIM
        m_new = jnp.maximum(m_i, jnp.max(s, axis=1, keepdims=True))
        pr = jnp.exp2(s - m_new)
        alpha = jnp.exp2(m_i - m_new)
        vt = jnp.concatenate([sel_ref[vs:vs + HEAD_DIM, pl.ds(k0, kc)].astype(BF16), ones_rows], axis=0)
        return m_new, alpha * acc + _dot_nt(pr.astype(BF16), vt)

    last = (p0 + Q_BLOCK - 1) // kc
    blocks_per_chunk = kc // SEL_BLOCK
    n_need = jnp.int32(0)
    for c in range(t_len // kc - 1):
        picked = bias_f[c * blocks_per_chunk:(c + 1) * blocks_per_chunk, :] == 0.0
        need = jnp.where(c < last, jnp.max(jnp.where(picked, 1, 0)), 0)
        need_sc[n_need] = c
        n_need = n_need + need

    def sel_step(i, carry):
        c = need_sc[i]
        out = []
        for g in range(KV_HEADS):
            s, k0 = chunk_scores(g, c)
            out.append(accumulate(g, s, k0, carry[g]))
        return tuple(out)

    one = (jnp.full((nrow, 1), NEG, F32), jnp.zeros((nrow, HEAD_DIM + POS_ROWS), F32))
    carry = lax.fori_loop(0, n_need, sel_step, (one,) * KV_HEADS)
    for g in range(KV_HEADS):
        s, k0 = chunk_scores(g, last)
        kpos = k0 + lax.broadcasted_iota(jnp.int32, (1, kc), 1)
        _, acc_s = accumulate(g, jnp.where(kpos <= tpos, s, NEG), k0, carry[g])
        o_sel = acc_s[:, :HEAD_DIM] / jnp.maximum(acc_s[:, HEAD_DIM:HEAD_DIM + 1], 1e-30)
        for i in range(GQ):
            hh = g * GQ + i
            r = slice(i * Q_BLOCK, (i + 1) * Q_BLOCK)
            o_ref[:, hh * HEAD_DIM:(hh + 1) * HEAD_DIM] += gates[:, 3 * hh + 1:3 * hh + 2] * o_sel[r]


def _nsa_prompt(q, gates, kvc_t, rows_t, win_t):
    b, t, _ = q.shape
    assert t % 128 == 0
    kc = 512 if t % 512 == 0 else t
    wb = min(WINDOW + 128, t)
    n_cmp = kvc_t.shape[2]
    n_blocks = t // SEL_BLOCK
    ncol = 4 * HEAD_DIM
    slopes = _alibi_slopes(Q_BLOCK)
    saug = _slope_pieces(Q_BLOCK)
    kconst = _key_constants(t)
    smat = _score_matrix(n_cmp, n_blocks, n_blocks).T
    return pl.pallas_call(
        functools.partial(_nsa_prompt_body, t_len=t, kc=kc, wb=wb), grid=(b, t // Q_BLOCK),
        in_specs=[pl.BlockSpec((None, Q_BLOCK, NSA_WIDTH), lambda i, j: (i, j, 0)),
                  pl.BlockSpec((None, Q_BLOCK, 3 * NSA_HEADS), lambda i, j: (i, j, 0)),
                  pl.BlockSpec((None, ncol, n_cmp), lambda i, j: (i, 0, 0)),
                  pl.BlockSpec((None, ncol, t), lambda i, j: (i, 1, 0)),
                  pl.BlockSpec((None, ncol, t), lambda i, j: (i, 0, 0)),
                  _const_spec(slopes.shape), _const_spec(saug.shape), _const_spec(kconst.shape),
                  _const_spec(smat.shape)],
        out_specs=pl.BlockSpec((None, Q_BLOCK, NSA_WIDTH), lambda i, j: (i, j, 0)),
        out_shape=jax.ShapeDtypeStruct((b, t, NSA_WIDTH), F32),
        scratch_shapes=[pltpu.SMEM((t // kc,), jnp.int32)],
        compiler_params=_params(("parallel", "arbitrary")), name="nsa_prompt")(
            q, gates, kvc_t, rows_t, win_t, slopes, saug, kconst, smat)


def _nsa_dec_select_body(q_ref, kvc_ref, slope_ref, smat_ref, ocmp_ref, sel_ref, *, past_len, tq, n_blocks,
                         step_blocks):
    n_cmp = kvc_ref.shape[1]
    n_steps = sel_ref.shape[1] - 1
    lanes_out = sel_ref.shape[3]
    nrow = GQ * tq
    qpos = lax.broadcasted_iota(jnp.int32, (tq, 1), 0) + past_len
    tpos = jnp.concatenate([qpos] * GQ, axis=0)
    cpos = CMP_STRIDE * lax.broadcasted_iota(jnp.int32, (1, n_cmp), 1) + (CMP_STRIDE - 1)
    dist = tpos - cpos
    for g in range(KV_HEADS):
        qg = _group_queries(q_ref, g)
        ks, vs = g * HEAD_DIM, (KV_HEADS + g) * HEAD_DIM
        s = _dot(qg, kvc_ref[ks:ks + HEAD_DIM, :].astype(BF16)) - slope_ref[g] * dist.astype(F32)
        p_cmp = _softmax_rows(s, (dist >= 0) & (cpos >= 2 * CMP_STRIDE - 1))
        ocmp_ref[g] = _dot_nt(p_cmp.astype(BF16), kvc_ref[vs:vs + HEAD_DIM, :].astype(BF16))
        imp = jnp.sum(p_cmp.reshape(GQ, tq, n_cmp), axis=0)
        score = jnp.dot(imp, smat_ref[...], precision=HIGHEST, preferred_element_type=F32)
        bias = jnp.where(_select_blocks(score, qpos // SEL_BLOCK, n_blocks) > 0.5, 0.0, NEG)
        pad = jnp.zeros((tq, lanes_out - step_blocks), F32)
        for st in range(n_steps):
            sel_ref[g, st] = jnp.concatenate([bias[:, st * step_blocks:(st + 1) * step_blocks], pad], axis=1)
        last_blk = n_steps * step_blocks
        sel_ref[g, n_steps] = jnp.concatenate([bias[:, last_blk:last_blk + 1],
                                               jnp.zeros((tq, lanes_out - 1), F32)], axis=1)


def _nsa_dec_select(q, kvc, past_len, n_blocks, step_blocks):
    b, tq, _ = q.shape
    n_cmp = kvc.shape[2]
    lanes = -(-n_blocks // 128) * 128
    n_steps = (n_blocks - 1) // step_blocks
    assert n_steps * step_blocks == n_blocks - 1 and step_blocks <= 128
    slopes = _alibi_slopes(tq)
    smat = _score_matrix(n_cmp, n_blocks, lanes)
    return pl.pallas_call(
        functools.partial(_nsa_dec_select_body, past_len=past_len, tq=tq, n_blocks=n_blocks,
                          step_blocks=step_blocks), grid=(b,),
        in_specs=[pl.BlockSpec((None, tq, NSA_WIDTH), lambda i: (i, 0, 0)),
                  pl.BlockSpec((None, 4 * HEAD_DIM, n_cmp), lambda i: (i, 0, 0)),
                  _const_spec(slopes.shape), _const_spec(smat.shape)],
        out_specs=[pl.BlockSpec((None, KV_HEADS, GQ * tq, HEAD_DIM), lambda i: (i, 0, 0, 0)),
                   pl.BlockSpec((None, KV_HEADS, n_steps + 1, tq, 128), lambda i: (i, 0, 0, 0, 0))],
        out_shape=[jax.ShapeDtypeStruct((b, KV_HEADS, GQ * tq, HEAD_DIM), F32),
                   jax.ShapeDtypeStruct((b, KV_HEADS, n_steps + 1, tq, 128), F32)],
        compiler_params=_params(("parallel",)), name="nsa_dec_select")(q, kvc, slopes, smat)


def _nsa_dec_attend_body(pt_ref, q_ref, selb_ref, ocmp_ref, gates_ref, newsel_ref, winbuf_ref, newwin_ref,
                         slope_ref, saug_ref, pos_ref, blk_ref, *rest, past_len, tq, t_valid, n_pages):
    pages = rest[:n_pages]
    o_ref, m_sc, acc_sc = rest[n_pages:]
    step = pl.program_id(1)
    nrow = GQ * tq
    qpos = lax.broadcasted_iota(jnp.int32, (tq, 1), 0) + past_len
    tpos = jnp.concatenate([qpos] * GQ, axis=0)

    @pl.when(step == 0)
    def _():
        m_sc[...] = jnp.full(m_sc.shape, NEG, F32)
        acc_sc[...] = jnp.zeros(acc_sc.shape, F32)

    def update(g, sc, weighted_values):
        m_i = m_sc[g]
        m_new = jnp.maximum(m_i, jnp.max(sc, axis=1, keepdims=True))
        pr = jnp.exp2(sc - m_new)
        acc_sc[g] = jnp.exp2(m_i - m_new) * acc_sc[g] + weighted_values(pr.astype(BF16))
        m_sc[g] = m_new

    nk = n_pages * PAGE_SIZE
    ones_rows = jnp.ones((POS_ROWS, nk), BF16)
    for g in range(KV_HEADS):
        qg = _group_queries(q_ref, g)
        ks, vs = g * HEAD_DIM, (KV_HEADS + g) * HEAD_DIM
        bias = jnp.concatenate([selb_ref[g, step]] * GQ, axis=0).astype(BF16)
        q_aug = jnp.concatenate([qg, saug_ref[g], bias], axis=1)
        kat = jnp.concatenate([jnp.concatenate([pg[ks:ks + HEAD_DIM, :] for pg in pages], axis=1).astype(BF16),
                               pos_ref[...], blk_ref[...]], axis=0)
        vt = jnp.concatenate([jnp.concatenate([pg[vs:vs + HEAD_DIM, :] for pg in pages], axis=1).astype(BF16),
                              ones_rows], axis=0)
        update(g, _dot(q_aug, kat), lambda p, vt=vt: _dot_nt(p, vt))

    @pl.when(step == pl.num_programs(1) - 1)
    def _():
        gates = gates_ref[...]
        tn = newsel_ref.shape[0]
        npos = past_len + lax.broadcasted_iota(jnp.int32, (1, tn), 1)
        nd = tpos - npos
        valid_new = (nd >= 0) & (npos < past_len + t_valid)
        wlen = winbuf_ref.shape[1]
        wpos = past_len - wlen + lax.broadcasted_iota(jnp.int32, (1, wlen), 1)
        wd = tpos - wpos
        for g in range(KV_HEADS):
            qg = _group_queries(q_ref, g)
            ks, vs = g * HEAD_DIM, (KV_HEADS + g) * HEAD_DIM
            slope = slope_ref[g]
            block_bias = jnp.concatenate([selb_ref[g, selb_ref.shape[1] - 1][:, 0:1]] * GQ, axis=0)
            sc = (_dot_nt(qg, newsel_ref[:, ks:ks + HEAD_DIM].astype(BF16)) + slope * npos.astype(F32)
                  + block_bias)
            new_v = jnp.concatenate([newsel_ref[:, vs:vs + HEAD_DIM].astype(BF16),
                                     jnp.ones((tn, POS_ROWS), BF16)], axis=1)
            update(g, jnp.where(valid_new, sc, NEG), lambda p, new_v=new_v: _dot(p, new_v))
            acc = acc_sc[g]
            o_sel = acc[:, :HEAD_DIM] / jnp.maximum(acc[:, HEAD_DIM:HEAD_DIM + 1], 1e-30)
            s1 = _dot(qg, winbuf_ref[ks:ks + HEAD_DIM, :].astype(BF16)) - slope * wd.astype(F32)
            s2 = _dot_nt(qg, newwin_ref[:, ks:ks + HEAD_DIM].astype(BF16)) - slope * nd.astype(F32)
            ok1 = (wd >= 0) & (wd < WINDOW) & (wpos >= 0)
            ok2 = valid_new & (nd < WINDOW)
            s1 = jnp.where(ok1, s1, -jnp.inf)
            s2 = jnp.where(ok2, s2, -jnp.inf)
            mx = jnp.maximum(jnp.max(s1, axis=1, keepdims=True), jnp.max(s2, axis=1, keepdims=True))
            mx = jnp.where(mx > -jnp.inf, mx, 0.0)
            e1 = jnp.exp2(s1 - mx)
            e2 = jnp.exp2(s2 - mx)
            den = jnp.maximum(jnp.sum(e1, axis=1, keepdims=True) + jnp.sum(e2, axis=1, keepdims=True), 1e-30)
            o_win = (_dot_nt(e1.astype(BF16), winbuf_ref[vs:vs + HEAD_DIM, :].astype(BF16))
                     + _dot(e2.astype(BF16), newwin_ref[:, vs:vs + HEAD_DIM].astype(BF16))) / den
            o_cmp = ocmp_ref[g]
            for i in range(GQ):
                hh = g * GQ + i
                r = slice(i * tq, (i + 1) * tq)
                o_ref[:, hh * HEAD_DIM:(hh + 1) * HEAD_DIM] = (
                    gates[:, 3 * hh:3 * hh + 1] * o_cmp[r] + gates[:, 3 * hh + 1:3 * hh + 2] * o_sel[r]
                    + gates[:, 3 * hh + 2:3 * hh + 3] * o_win[r])


DEC_PAGES = 16


def _nsa_dec_attend(q, selbias, o_cmp, gates, newsel, winbuf, newwin, cache, page_table, past_len, t_valid):
    b, tq, _ = q.shape
    npg = page_table.shape[1]
    pg = min(DEC_PAGES, npg)
    assert npg % pg == 0 and selbias.shape[2] == npg // pg + 1
    nk = pg * PAGE_SIZE
    ncol = 4 * HEAD_DIM
    slopes = _alibi_slopes(tq)
    saug = _slope_pieces(tq)
    pos_rows = jnp.asarray(_pos_rows(past_len)).astype(BF16)
    blk_rows = jnp.asarray(_block_rows(selbias.shape[-1], nk)).astype(BF16)
    wlen = winbuf.shape[2]
    bspec = lambda shape: pl.BlockSpec((None,) + shape, lambda i, j, pt: (i,) + (0,) * len(shape))
    page_specs = [pl.BlockSpec((None, ncol, PAGE_SIZE),
                               functools.partial(lambda i, j, pt, p: (pt[i, j * pg + p], 1, 0), p=p))
                  for p in range(pg)]
    return pl.pallas_call(
        functools.partial(_nsa_dec_attend_body, past_len=past_len, tq=tq, t_valid=t_valid, n_pages=pg),
        grid_spec=pltpu.PrefetchScalarGridSpec(
            num_scalar_prefetch=1, grid=(b, npg // pg),
            in_specs=[pl.BlockSpec((None, tq, NSA_WIDTH), lambda i, j, pt: (i, 0, 0)),
                      bspec(selbias.shape[1:]), bspec((KV_HEADS, GQ * tq, HEAD_DIM)),
                      bspec((tq, 3 * NSA_HEADS)), bspec((tq, ncol)), bspec((ncol, wlen)), bspec((tq, ncol)),
                      _const_spec(slopes.shape), _const_spec(saug.shape),
                      pl.BlockSpec((POS_ROWS, nk), lambda i, j, pt: (0, j)), _const_spec(blk_rows.shape)]
            + page_specs,
            out_specs=bspec((tq, NSA_WIDTH)),
            scratch_shapes=[pltpu.VMEM((KV_HEADS, GQ * tq, 1), F32),
                            pltpu.VMEM((KV_HEADS, GQ * tq, HEAD_DIM + POS_ROWS), F32)]),
        out_shape=jax.ShapeDtypeStruct((b, tq, NSA_WIDTH), F32),
        compiler_params=_params(("parallel", "arbitrary")), name="nsa_dec_attend")(
            page_table, q, selbias, o_cmp, gates, newsel, winbuf, newwin, slopes, saug, pos_rows, blk_rows,
            *([cache] * pg))


def _pad_t(a, axis, to):
    pad = [(0, 0)] * a.ndim
    pad[axis] = (0, to - a.shape[axis])
    return jnp.pad(a, pad)


def _even_layer(x, b, t, pos0, pool_buf, c0, n0, m0, g, w_in, b_gate, w_pool, pool_scale, w_out):
    n = b * t
    u, q, k, v, og, vt, gc, gr = _even_in(x, g, w_in, b_gate)
    u3 = u.reshape(b, t, POOL_WIDTH)
    new_pool = jnp.concatenate([pool_buf, u3], axis=1)[:, -pool_buf.shape[1]:]
    h4 = ML_HEADS
    if t % 256 == 0:
        tp, L = t, 256
        vt_in, gr_in = vt, gr
    else:
        tp = L = -(-t // 8) * 8
        vt_in = _pad_t(vt.reshape(h4, ML_DH, b, t), 3, tp).transpose(0, 2, 1, 3)
        gr_in = _pad_t(gr.reshape(2 * h4, b, t), 2, tp).transpose(1, 0, 2)
    seq = lambda a: _pad_t(a.reshape(h4, b, t, ML_DH), 2, tp)
    ypool = _pool(_pad_t(u3, 1, tp), pool_buf, w_pool, pool_scale, pos0)[:, :t].reshape(n, POOL_WIDTH)
    hm, c, nn, m = _mlstm(seq(q), seq(k), seq(v), vt_in, _pad_t(gc.reshape(b, t, 2 * h4), 1, tp), gr_in,
                          c0, n0, m0, L=L, t_valid=t)
    hm = hm[:, :, :t].reshape(h4, n, ML_DH)
    x = _even_out(x, ypool, og, hm, w_out)
    return x, new_pool, c, nn.reshape(b, h4, ML_DH), m.reshape(b, h4)


def _odd_layer(x, b, t, pos0, cache, page_table, win_buf, conv_buf, g, w_in, cmp_pos_w, cmp_w, cmp_b,
               conv_w, w_out):
    n = b * t
    ncol = 4 * HEAD_DIM
    prompt = cache is None
    q, rows_t, win_t, gates, ucv, bg = _odd_in(x, g, w_in, t if prompt else n)
    u3 = ucv.reshape(b, t, CONV_CH)
    new_conv = jnp.concatenate([conv_buf, u3], axis=1)[:, -(CONV_K - 1):]
    if prompt:
        kvc_t = _compress_prompt(rows_t, cmp_pos_w, cmp_w, cmp_b)
        o = _nsa_prompt(q.reshape(b, t, NSA_WIDTH), gates.reshape(b, t, 3 * NSA_HEADS), kvc_t, rows_t, win_t)
        new_win_t = win_t[:, :, t - min(WINDOW, t):]
        new_rows = rows_t.reshape(b, 4, KV_HEADS, HEAD_DIM, t).transpose(0, 4, 1, 2, 3)
        tp = t
    else:
        assert pos0 % SEL_BLOCK + t <= SEL_BLOCK
        tp = -(-t // 8) * 8
        rows_bt = rows_t[0].T.reshape(b, t, 2 * ncol)
        win_bt = win_t[0].T.reshape(b, t, ncol)
        n_pool = cache.shape[0]
        cache_t = cache.transpose(0, 2, 3, 4, 1).reshape(n_pool, 2 * ncol, PAGE_SIZE)
        kvc_t = _compress_paged(cache_t, page_table, cmp_pos_w, cmp_w, cmp_b)
        qp = _pad_t(q.reshape(b, t, NSA_WIDTH), 1, tp)
        n_blocks = -(-(pos0 + t) // SEL_BLOCK)
        step_blocks = min(DEC_PAGES, page_table.shape[1]) * PAGE_SIZE // SEL_BLOCK
        o_cmp, selmask = _nsa_dec_select(qp, kvc_t, pos0, n_blocks, step_blocks)
        wlen = win_buf.shape[1]
        winb_t = win_buf.transpose(0, 2, 3, 4, 1).reshape(b, ncol, wlen)
        o = _nsa_dec_attend(qp, selmask, o_cmp, _pad_t(gates.reshape(b, t, -1), 1, tp),
                            _pad_t(rows_bt[:, :, ncol:], 1, tp), winb_t, _pad_t(win_bt, 1, tp),
                            cache_t, page_table, pos0, t)[:, :t]
        new_win_t = jnp.concatenate([winb_t, win_bt.transpose(0, 2, 1)], axis=2)[:, :, -wlen:]
        new_rows = rows_bt.reshape(b, t, 4, KV_HEADS, HEAD_DIM)
    yconv = _conv(_pad_t(u3, 1, tp), _pad_t(bg.reshape(b, t, CONV_CH), 1, tp), conv_buf, conv_w)[:, :t]
    x = _odd_out(x, o.reshape(n, NSA_WIDTH), yconv.reshape(n, CONV_CH), w_out)
    new_win = new_win_t.reshape(b, 2, KV_HEADS, HEAD_DIM, new_win_t.shape[2]).transpose(0, 4, 1, 2, 3)
    return x, new_rows, new_win, new_conv


def _trunk(x3, pos0, pool_buf, ml_c, ml_n, ml_m, kv_cache, page_table, win_buf, conv_buf, p):
    b, t, d = x3.shape
    depth = p["norm_g"].shape[0]
    x = x3.reshape(b * t, d)
    pools, cs, ns, ms, rows, wins, convs = [], [], [], [], [], [], []
    for l in range(depth):
        j = l // 2
        x = _ffn(x, p["norm_g"][l, 0], p["w_ffn_in"][l][0], p["w_ffn_out"][l][0])
        if l % 2 == 0:
            x, pb, c, n, m = _even_layer(x, b, t, pos0, pool_buf[j], ml_c[j], ml_n[j], ml_m[j], p["norm_g"][l, 1],
                                         p["w_in_even"][j], p["b_gate_even"][j], p["w_pool"][j],
                                         p["pool_scale"][j], p["w_out_even"][j])
            pools.append(pb)
            cs.append(c)
            ns.append(n)
            ms.append(m)
        else:
            cache = None if kv_cache is None else kv_cache[j]
            wb = None if win_buf is None else win_buf[j]
            x, r, wn, cn = _odd_layer(x, b, t, pos0, cache, page_table, wb, conv_buf[j], p["norm_g"][l, 1],
                                      p["w_in_odd"][j], p["cmp_pos_w"][j], p["cmp_w"][j], p["cmp_b"][j],
                                      p["conv_w"][j], p["w_out_odd"][j])
            rows.append(r)
            wins.append(wn)
            convs.append(cn)
        x = _ffn(x, p["norm_g"][l, 2], p["w_ffn_in"][l][1], p["w_ffn_out"][l][1],
                 final_g=p["final_g"] if l == depth - 1 else None)
    states = (jnp.stack(pools), jnp.stack(cs), jnp.stack(ns), jnp.stack(ms),
              jnp.stack(rows), jnp.stack(wins), jnp.stack(convs))
    return x.reshape(b, t, d), states


def kernel(x_prompt, x_sample, state_pool, state_mlstm_c, state_mlstm_n, state_mlstm_m, cache_nsa_kv, state_win_kv, state_conv, page_table, norm_g, final_g, w_ffn_in, w_ffn_out, w_in_even, b_gate_even, w_pool, pool_scale, w_out_even, w_in_odd, cmp_pos_w, cmp_w, cmp_b, conv_w, w_out_odd):
    bp = x_prompt.shape[0]
    n_even, n_odd = state_pool.shape[0], state_conv.shape[0]
    past_len = page_table.shape[1] * PAGE_SIZE
    p = dict(norm_g=norm_g, final_g=final_g, w_ffn_in=w_ffn_in.astype(BF16), w_ffn_out=w_ffn_out.astype(BF16),
             w_in_even=w_in_even, b_gate_even=b_gate_even, w_pool=w_pool, pool_scale=pool_scale,
             w_out_even=w_out_even, w_in_odd=w_in_odd, cmp_pos_w=cmp_pos_w, cmp_w=cmp_w, cmp_b=cmp_b,
             conv_w=conv_w, w_out_odd=w_out_odd)
    pool0 = jnp.zeros((n_even, bp) + state_pool.shape[2:], F32)
    c0 = jnp.zeros((n_even, bp) + state_mlstm_c.shape[2:], F32)
    n0 = jnp.zeros((n_even, bp) + state_mlstm_n.shape[2:], F32)
    m0 = jnp.zeros((n_even, bp) + state_mlstm_m.shape[2:], F32)
    conv0 = jnp.zeros((n_odd, bp) + state_conv.shape[2:], F32)
    y_p, (pool_p, c_p, n_p, m_p, kv_p, win_p, conv_p) = _trunk(
        x_prompt, 0, pool0, c0, n0, m0, None, None, None, conv0, p)
    y_s, (pool_s, c_s, n_s, m_s, kv_s, win_s, conv_s) = _trunk(
        x_sample, past_len, state_pool, state_mlstm_c, state_mlstm_n, state_mlstm_m,
        cache_nsa_kv, page_table, state_win_kv, state_conv, p)
    return (y_p, y_s, pool_p, pool_s, c_p, c_s, n_p, n_s, m_p, m_s,
            kv_p, kv_s, win_p, win_s, conv_p, conv_s)
```

```python
import functools

import numpy as np
import jax
import jax.numpy as jnp
from jax import lax
from jax.experimental import pallas as pl
from jax.experimental.pallas import tpu as pltpu

F32 = jnp.float32
BF16 = jnp.bfloat16

EPS = 1e-6
POOL_WINDOWS = (2, 4, 8, 16)
POOL_GDIM = 64
POOL_WIDTH = 256
POOL_HALO = 16
ML_HEADS = 4
ML_DH = 192
ML_WIDTH = ML_HEADS * ML_DH
NSA_HEADS = 12
HEAD_DIM = 64
NSA_WIDTH = NSA_HEADS * HEAD_DIM
KV_HEADS = 2
GQ = NSA_HEADS // KV_HEADS
CMP_STRIDE = 16
SEL_BLOCK = 64
SEL_TOPN = 16
WINDOW = 512
Q_BLOCK = 64
PAGE_SIZE = 128
CONV_CH = 256
CONV_K = 3
CONV_HALO = 8

VMEM_LIMIT = 56 * 1024 * 1024
NEG = -1e30
LOG2E = 1.4426950408889634


def _params(sem, vmem=VMEM_LIMIT):
    return pltpu.CompilerParams(dimension_semantics=sem, vmem_limit_bytes=vmem)


def _const_spec(shape):
    nd = len(shape)
    return pl.BlockSpec(shape, lambda *_: (0,) * nd, pipeline_mode=pl.Buffered(1))


def _rms(x, g):
    return x * lax.rsqrt(jnp.mean(x * x, axis=-1, keepdims=True) + EPS) * g


def _dot(a, b):
    return jnp.dot(a, b, preferred_element_type=F32)


def _dot_nt(a, b):
    return lax.dot_general(a, b, (((1,), (1,)), ((), ())), preferred_element_type=F32)


def _split3(x, axis):
    hi = x.astype(BF16)
    r1 = x - hi.astype(F32)
    mid = r1.astype(BF16)
    lo = (r1 - mid.astype(F32)).astype(BF16)
    return jnp.concatenate([hi, mid, lo], axis=axis)


def _ffn_body(x_ref, g_ref, win_ref, wout_ref, *rest, d_ff, chunks, has_final):
    o_ref = rest[-1]
    x = x_ref[...]
    hn = _rms(x, g_ref[...]).astype(BF16)
    acc = jnp.zeros(x.shape, F32)
    off = 0
    for fc in chunks:
        a = _dot(hn, win_ref[:, off:off + fc])
        b = _dot(hn, win_ref[:, d_ff + off:d_ff + off + fc])
        act = (a * jax.nn.sigmoid(a) * b).astype(BF16)
        acc = acc + _dot(act, wout_ref[off:off + fc, :])
        off += fc
    y = x + 0.5 * acc
    if has_final:
        y = _rms(y, rest[0][...])
    o_ref[...] = y


def _ffn(x, g, w_in, w_out, final_g=None):
    n, d = x.shape
    d_ff = w_out.shape[0]
    tm = 512 if n % 512 == 0 else n
    chunks, left = [], d_ff
    while left:
        chunks.append(min(1024, left))
        left -= chunks[-1]
    row = pl.BlockSpec((tm, d), lambda i: (i, 0))
    in_specs = [row, _const_spec((1, d)), _const_spec(w_in.shape), _const_spec(w_out.shape)]
    args = [x, g.reshape(1, d), w_in, w_out]
    if final_g is not None:
        in_specs.append(_const_spec((1, d)))
        args.append(final_g.reshape(1, d))
    return pl.pallas_call(
        functools.partial(_ffn_body, d_ff=d_ff, chunks=tuple(chunks), has_final=final_g is not None),
        grid=(n // tm,), in_specs=in_specs, out_specs=row,
        out_shape=jax.ShapeDtypeStruct((n, d), F32),
        compiler_params=_params(("parallel",)), name="ffn")(*args)


def _even_in_body(x_ref, g_ref, wu_ref, wh_ref, wvt_ref, wg_ref, wgt_ref, bg_ref, bgt_ref,
                  u_ref, q_ref, k_ref, v_ref, og_ref, vt_ref, gc_ref, gr_ref):
    hn = _rms(x_ref[...], g_ref[...]).astype(BF16)
    u_ref[...] = _dot(hn, wu_ref[...])
    for seg, out_ref in enumerate((q_ref, k_ref, v_ref, og_ref)):
        z = _dot(hn, wh_ref[seg])
        for h in range(ML_HEADS):
            out_ref[h] = z[:, h * ML_DH:(h + 1) * ML_DH]
    for h in range(ML_HEADS):
        vt_ref[h] = _dot_nt(wvt_ref[h], hn)
    gc_ref[...] = _dot(hn, wg_ref[...]) + bg_ref[...]
    gr_ref[...] = _dot_nt(wgt_ref[...], hn) + bgt_ref[...]


def _even_in(x, g, w_in, b_gate):
    n, d = x.shape
    tm = 512 if n % 512 == 0 else n
    h4 = ML_HEADS
    wu = w_in[:, :POOL_WIDTH].astype(BF16)
    wh = w_in[:, POOL_WIDTH:POOL_WIDTH + 4 * ML_WIDTH].reshape(d, 4, ML_WIDTH).transpose(1, 0, 2).astype(BF16)
    wvt = wh[2].reshape(d, h4, ML_DH).transpose(1, 2, 0)
    wg = w_in[:, POOL_WIDTH + 4 * ML_WIDTH:].astype(BF16)
    row = lambda w: pl.BlockSpec((tm, w), lambda i: (i, 0))
    hrow = pl.BlockSpec((h4, tm, ML_DH), lambda i: (0, i, 0))
    hsh = jax.ShapeDtypeStruct((h4, n, ML_DH), F32)
    return pl.pallas_call(
        _even_in_body, grid=(n // tm,),
        in_specs=[row(d), _const_spec((1, d)), _const_spec(wu.shape), _const_spec(wh.shape),
                  _const_spec(wvt.shape), _const_spec(wg.shape), _const_spec((2 * h4, d)),
                  _const_spec((1, 2 * h4)), _const_spec((2 * h4, 1))],
        out_specs=[row(POOL_WIDTH), hrow, hrow, hrow, hrow,
                   pl.BlockSpec((h4, ML_DH, tm), lambda i: (0, 0, i)),
                   row(2 * h4), pl.BlockSpec((2 * h4, tm), lambda i: (0, i))],
        out_shape=[jax.ShapeDtypeStruct((n, POOL_WIDTH), F32), hsh, hsh, hsh, hsh,
                   jax.ShapeDtypeStruct((h4, ML_DH, n), F32),
                   jax.ShapeDtypeStruct((n, 2 * h4), F32), jax.ShapeDtypeStruct((2 * h4, n), F32)],
        compiler_params=_params(("parallel",)), name="even_in")(
            x, g.reshape(1, d), wu, wh, wvt, wg, wg.T, b_gate.reshape(1, -1), b_gate.reshape(-1, 1))


def _pool_body(u_ref, pre_ref, w_ref, sc_ref, y_ref, carry, full, *, tb, pos0):
    t = pl.program_id(1)

    @pl.when(t == 0)
    def _():
        carry[...] = pre_ref[...]

    u = u_ref[...]
    full[0:POOL_HALO] = carry[...]
    full[POOL_HALO:] = u
    acc = full[...]
    sums = []
    for sh in (1, 2, 4, 8):
        acc = acc + pltpu.roll(acc, sh, 0)
        sums.append(acc[POOL_HALO:])
    lane = lax.broadcasted_iota(jnp.int32, (tb, POOL_WIDTH), 1)
    grp = lane // POOL_GDIM
    win = jnp.where(grp == 0, sums[0], jnp.where(grp == 1, sums[1], jnp.where(grp == 2, sums[2], sums[3])))
    width = jnp.where(grp == 0, 2, jnp.where(grp == 1, 4, jnp.where(grp == 2, 8, 16)))
    pos = pos0 + t * tb + lax.broadcasted_iota(jnp.int32, (tb, POOL_WIDTH), 0)
    cnt = jnp.minimum(pos + 1, width).astype(F32)
    mixed = (win / cnt - u).astype(BF16)
    y_ref[...] = _dot(mixed, w_ref[...]) * sc_ref[...]
    carry[...] = full[tb:tb + POOL_HALO]


def _pool(u, prefix, w_pool, scale, pos0):
    b, t, c = u.shape
    tb = 512 if t % 512 == 0 else t
    pre = jnp.pad(prefix, ((0, 0), (POOL_HALO - prefix.shape[1], 0), (0, 0)))
    wbd = jax.scipy.linalg.block_diag(*[w_pool[i] for i in range(w_pool.shape[0])]).astype(BF16)
    return pl.pallas_call(
        functools.partial(_pool_body, tb=tb, pos0=pos0), grid=(b, t // tb),
        in_specs=[pl.BlockSpec((None, tb, c), lambda i, j: (i, j, 0)),
                  pl.BlockSpec((None, POOL_HALO, c), lambda i, j: (i, 0, 0)),
                  _const_spec((c, c)), _const_spec((1, c))],
        out_specs=pl.BlockSpec((None, tb, c), lambda i, j: (i, j, 0)),
        out_shape=jax.ShapeDtypeStruct((b, t, c), F32),
        scratch_shapes=[pltpu.VMEM((POOL_HALO, c), F32), pltpu.VMEM((tb + POOL_HALO, c), F32)],
        compiler_params=_params(("parallel", "arbitrary")), name="pool")(u, pre, wbd, scale.reshape(1, c))


def _log_sigmoid(x):
    return jnp.minimum(x, 0.0) - jnp.log(1.0 + jnp.exp(-jnp.abs(x)))


def _mlstm_body(q_ref, k_ref, v_ref, vt_ref, gc_ref, gr_ref, c0_ref, n0_ref, m0_ref,
                h_ref, c_ref, n_ref, m_ref, *, L, t_valid):
    @pl.when(pl.program_id(1) == 0)
    def _():
        c_ref[...] = c0_ref[...]
        n_ref[...] = n0_ref[...]
        m_ref[...] = m0_ref[...]

    gc = gc_ref[...]
    gr = gr_ref[...]
    lf_c = _log_sigmoid(gc)
    lf_r = _log_sigmoid(gr)
    row = lax.broadcasted_iota(jnp.int32, (L, L), 0)
    col = lax.broadcasted_iota(jnp.int32, (L, L), 1)
    tok_c = lax.broadcasted_iota(jnp.int32, (L, 1), 0) < t_valid
    tok_r = lax.broadcasted_iota(jnp.int32, (1, L), 1) < t_valid
    if t_valid < L:
        lf_c = jnp.where(tok_c, lf_c, 0.0)
        lf_r = jnp.where(tok_r, lf_r, 0.0)
    causal = row >= col
    tri3 = jnp.concatenate([causal.astype(BF16)] * 3, axis=1)
    cs_c = _dot(tri3, _split3(lf_c, 0))
    cs_r = _dot_nt(_split3(lf_r, 1), tri3)
    heads = range(ML_HEADS)
    q = [q_ref[hd] for hd in heads]
    kf = [k_ref[hd] * (ML_DH ** -0.5) for hd in heads]
    qb = [x.astype(BF16) for x in q]
    kb = [x.astype(BF16) for x in kf]
    c_old = [c_ref[hd] for hd in heads]
    n_old = [n_ref[hd] for hd in heads]
    m_prev = [m_ref[hd] for hd in heads]
    qk = [_dot_nt(qb[hd], kb[hd]) for hd in heads]
    qc = [_dot_nt(qb[hd], c_old[hd].astype(BF16)) for hd in heads]
    i_c, i_r, b_c, b_r, mt, a, s = [], [], [], [], [], [], []
    for hd in heads:
        ic = gc[:, hd:hd + 1]
        ir = gr[hd:hd + 1, :]
        if t_valid < L:
            ic = jnp.where(tok_c, ic, -jnp.inf)
            ir = jnp.where(tok_r, ir, -jnp.inf)
        bc = cs_c[:, ML_HEADS + hd:ML_HEADS + hd + 1]
        br = cs_r[ML_HEADS + hd:ML_HEADS + hd + 1, :]
        dmat = jnp.where(causal, bc - br + ir, -jnp.inf)
        inter = bc + m_prev[hd]
        mth = jnp.maximum(jnp.max(dmat, axis=1, keepdims=True), inter)
        s.append(qk[hd] * jnp.exp(dmat - mth))
        a.append(jnp.exp(inter - mth))
        i_c.append(ic), i_r.append(ir), b_c.append(bc), b_r.append(br), mt.append(mth)
    sv = [_dot(s[hd].astype(BF16), v_ref[hd].astype(BF16)) for hd in heads]
    wk_r, wk_c, decay, m_new = [], [], [], []
    for hd in heads:
        b_last = b_c[hd][L - 1:L, :]
        ge_r = b_last - b_r[hd] + i_r[hd]
        ge_c = b_last - b_c[hd] + i_c[hd]
        mn = jnp.maximum(b_last + m_prev[hd], jnp.max(ge_r, axis=1, keepdims=True))
        wk_r.append(jnp.exp(ge_r - mn))
        wk_c.append(jnp.exp(ge_c - mn))
        decay.append(jnp.exp(b_last + m_prev[hd] - mn))
        m_new.append(mn)
    kv = [_dot((vt_ref[hd] * wk_r[hd]).astype(BF16), kb[hd]) for hd in heads]
    for hd in heads:
        num = sv[hd] + a[hd] * qc[hd]
        den = jnp.sum(s[hd], axis=1, keepdims=True) + a[hd] * jnp.sum(q[hd] * n_old[hd], axis=1, keepdims=True)
        h_ref[hd] = num / jnp.maximum(jnp.abs(den), jnp.exp(-mt[hd]))
    for hd in heads:
        c_ref[hd] = decay[hd] * c_old[hd] + kv[hd]
        n_ref[hd] = decay[hd] * n_old[hd] + jnp.sum(kf[hd] * wk_c[hd], axis=0, keepdims=True)
        m_ref[hd] = m_new[hd]


def _mlstm(q, k, v, vt, gc, gr, c0, n0, m0, *, L, t_valid):
    h4, b, t, dh = q.shape
    nc = t // L
    tok = pl.BlockSpec((h4, None, L, dh), lambda i, c: (0, i, c, 0))
    if vt.ndim == 4:
        vt_spec = pl.BlockSpec((h4, None, dh, L), lambda i, c: (0, i, 0, c))
        gr_spec = pl.BlockSpec((None, 2 * h4, L), lambda i, c: (i, 0, c))
    else:
        vt_spec = pl.BlockSpec((h4, dh, L), lambda i, c: (0, 0, i * nc + c))
        gr_spec = pl.BlockSpec((2 * h4, L), lambda i, c: (0, i * nc + c))
    st = lambda r, w: pl.BlockSpec((None, h4, r, w), lambda i, c: (i, 0, 0, 0))
    return pl.pallas_call(
        functools.partial(_mlstm_body, L=L, t_valid=t_valid), grid=(b, nc),
        in_specs=[tok, tok, tok, vt_spec, pl.BlockSpec((None, L, 2 * h4), lambda i, c: (i, c, 0)), gr_spec,
                  st(dh, dh), st(1, dh), st(1, 1)],
        out_specs=[tok, st(dh, dh), st(1, dh), st(1, 1)],
        out_shape=[jax.ShapeDtypeStruct((h4, b, t, dh), F32), jax.ShapeDtypeStruct((b, h4, dh, dh), F32),
                   jax.ShapeDtypeStruct((b, h4, 1, dh), F32), jax.ShapeDtypeStruct((b, h4, 1, 1), F32)],
        compiler_params=_params(("parallel", "arbitrary")), name="mlstm")(
            q, k, v, vt, gc, gr, c0, n0.reshape(b, h4, 1, dh), m0.reshape(b, h4, 1, 1))


def _even_out_body(x_ref, yp_ref, og_ref, hm_ref, w0_ref, w1_ref, o_ref):
    acc = _dot(yp_ref[...].astype(BF16), w0_ref[...])
    for h in range(ML_HEADS):
        acc = acc + _dot((jax.nn.sigmoid(og_ref[h]) * hm_ref[h]).astype(BF16), w1_ref[h])
    o_ref[...] = x_ref[...] + acc


def _even_out(x, ypool, og, hm, w_out):
    n, d = x.shape
    tm = 512 if n % 512 == 0 else n
    w0 = w_out[:POOL_WIDTH].astype(BF16)
    w1 = w_out[POOL_WIDTH:].reshape(ML_HEADS, ML_DH, d).astype(BF16)
    row = lambda w: pl.BlockSpec((tm, w), lambda i: (i, 0))
    hrow = pl.BlockSpec((ML_HEADS, tm, ML_DH), lambda i: (0, i, 0))
    return pl.pallas_call(
        _even_out_body, grid=(n // tm,),
        in_specs=[row(d), row(POOL_WIDTH), hrow, hrow, _const_spec(w0.shape), _const_spec(w1.shape)],
        out_specs=row(d), out_shape=jax.ShapeDtypeStruct((n, d), F32),
        compiler_params=_params(("parallel",)), name="even_out")(x, ypool, og, hm, w0, w1)


def _odd_in_body(x_ref, g_ref, wq_ref, wkvt_ref, wgt_ref, wc_ref,
                 q_ref, rows_ref, win_ref, gates_ref, ucv_ref, bg_ref):
    hn = _rms(x_ref[...], g_ref[...]).astype(BF16)
    q_ref[...] = _dot(hn, wq_ref[...]) * (LOG2E * HEAD_DIM ** -0.5)
    nrow = rows_ref.shape[0]
    rows_ref[...] = _dot_nt(wkvt_ref[:nrow], hn)
    win_ref[...] = _dot_nt(wkvt_ref[nrow:], hn)
    gates_ref[...] = jax.nn.sigmoid(_dot(hn, wgt_ref[...]))
    bg_ref[...] = _dot(hn, wc_ref[:, :CONV_CH])
    ucv_ref[...] = _dot(hn, wc_ref[:, CONV_CH:2 * CONV_CH]) * _dot(hn, wc_ref[:, 2 * CONV_CH:])


def _odd_in(x, g, w_in, seq):
    n, d = x.shape
    tm = 512 if seq % 512 == 0 else seq
    nt = seq // tm
    kvw = 6 * KV_HEADS * HEAD_DIM
    ngt = 3 * NSA_HEADS
    wq = w_in[:, :NSA_WIDTH].astype(BF16)
    wkvt = w_in[:, NSA_WIDTH:NSA_WIDTH + kvw].T.astype(BF16)
    wgt = w_in[:, NSA_WIDTH + kvw:NSA_WIDTH + kvw + ngt].astype(BF16)
    wc = w_in[:, NSA_WIDTH + kvw + ngt:].astype(BF16)
    nrow = 4 * KV_HEADS * HEAD_DIM
    row = lambda w: pl.BlockSpec((tm, w), lambda i: (i, 0))
    slab = lambda r: pl.BlockSpec((None, r, tm), lambda i: (i // nt, 0, i % nt))
    sh = lambda w: jax.ShapeDtypeStruct((n, w), F32)
    return pl.pallas_call(
        _odd_in_body, grid=(n // tm,),
        in_specs=[row(d), _const_spec((1, d)), _const_spec(wq.shape), _const_spec(wkvt.shape),
                  _const_spec(wgt.shape), _const_spec(wc.shape)],
        out_specs=[row(NSA_WIDTH),
                   slab(nrow), slab(kvw - nrow), row(ngt), row(CONV_CH), row(CONV_CH)],
        out_shape=[sh(NSA_WIDTH),
                   jax.ShapeDtypeStruct((n // seq, nrow, seq), F32),
                   jax.ShapeDtypeStruct((n // seq, kvw - nrow, seq), F32),
                   sh(ngt), sh(CONV_CH), sh(CONV_CH)],
        compiler_params=_params(("parallel",)), name="odd_in")(x, g.reshape(1, d), wq, wkvt, wgt, wc)


def _conv_body(u_ref, bg_ref, pre_ref, w_ref, y_ref, carry, full, *, tb):
    @pl.when(pl.program_id(1) == 0)
    def _():
        carry[...] = pre_ref[...]

    full[0:CONV_HALO] = carry[...]
    full[CONV_HALO:] = u_ref[...]
    f = full[...]
    w = w_ref[...]
    conv = f * w[2:3] + pltpu.roll(f, 1, 0) * w[1:2] + pltpu.roll(f, 2, 0) * w[0:1]
    y_ref[...] = bg_ref[...] * conv[CONV_HALO:]
    carry[...] = full[tb:tb + CONV_HALO]


def _conv(u, bg, prefix, conv_w):
    b, t, c = u.shape
    tb = 512 if t % 512 == 0 else t
    pre = jnp.pad(prefix, ((0, 0), (CONV_HALO - prefix.shape[1], 0), (0, 0)))
    blk = pl.BlockSpec((None, tb, c), lambda i, j: (i, j, 0))
    return pl.pallas_call(
        functools.partial(_conv_body, tb=tb), grid=(b, t // tb),
        in_specs=[blk, blk, pl.BlockSpec((None, CONV_HALO, c), lambda i, j: (i, 0, 0)), _const_spec((CONV_K, c))],
        out_specs=blk, out_shape=jax.ShapeDtypeStruct((b, t, c), F32),
        scratch_shapes=[pltpu.VMEM((CONV_HALO, c), F32), pltpu.VMEM((tb + CONV_HALO, c), F32)],
        compiler_params=_params(("parallel", "arbitrary")), name="conv")(u, bg, pre, conv_w)


def _odd_out_body(x_ref, o_ref, yc_ref, w0_ref, w1_ref, out_ref):
    out_ref[...] = (x_ref[...] + _dot(o_ref[...].astype(BF16), w0_ref[...])
                    + _dot(yc_ref[...].astype(BF16), w1_ref[...]))


def _odd_out(x, o, yconv, w_out):
    n, d = x.shape
    tm = 512 if n % 512 == 0 else n
    w0 = w_out[:NSA_WIDTH].astype(BF16)
    w1 = w_out[NSA_WIDTH:].astype(BF16)
    row = lambda w: pl.BlockSpec((tm, w), lambda i: (i, 0))
    return pl.pallas_call(
        _odd_out_body, grid=(n // tm,),
        in_specs=[row(d), row(NSA_WIDTH), row(CONV_CH), _const_spec(w0.shape), _const_spec(w1.shape)],
        out_specs=row(d), out_shape=jax.ShapeDtypeStruct((n, d), F32),
        compiler_params=_params(("parallel",)), name="odd_out")(x, o, yconv, w0, w1)


CMP_TILE = 2048
CMP_ROWS = 4 * HEAD_DIM


def _cmp_weights(cmp_pos_w, cmp_w, cmp_b, tile):
    m = tile // CMP_STRIDE
    cols = -(-(m + 1) // 128) * 128
    pos = jnp.arange(tile)
    chunk = (pos // CMP_STRIDE)[:, None]
    col = jnp.arange(cols)[None, :]
    mats = []
    for kv in range(2):
        wa = cmp_pos_w[kv, :CMP_STRIDE][pos % CMP_STRIDE][:, None]
        wb = cmp_pos_w[kv, CMP_STRIDE:][pos % CMP_STRIDE][:, None]
        mats.append(jnp.where(chunk == col, wb, 0.0) + jnp.where(chunk == col - 1, wa, 0.0))
    w2 = jnp.stack(mats).astype(BF16)
    wbdt = jax.scipy.linalg.block_diag(cmp_w[0], cmp_w[0], cmp_w[1], cmp_w[1]).T.astype(BF16)
    biast = jnp.concatenate([cmp_b[0], cmp_b[0], cmp_b[1], cmp_b[1]]).reshape(-1, 1)
    return w2, wbdt, biast


def _compress_body(*refs, n_pages, n_prefetch):
    refs = refs[n_prefetch:]
    pages = refs[:n_pages]
    w2_ref, wbdt_ref, biast_ref, out_ref, carry = refs[n_pages:]
    m_out = out_ref.shape[1]

    @pl.when(pl.program_id(1) == 0)
    def _():
        carry[...] = jnp.zeros(carry.shape, F32)

    x = jnp.concatenate([pg[...] for pg in pages], axis=1).astype(BF16)
    half = CMP_ROWS // 2
    res = jnp.concatenate([_dot(x[:half], w2_ref[0]), _dot(x[half:], w2_ref[1])], axis=0)
    first = lax.broadcasted_iota(jnp.int32, (CMP_ROWS, m_out), 1) == 0
    pre = res[:, :m_out] + jnp.where(first, carry[...], 0.0)
    carry[...] = jnp.broadcast_to(res[:, m_out:m_out + 1], carry.shape)
    out_ref[...] = _dot(wbdt_ref[...], pre.astype(BF16)) + biast_ref[...]


def _compress_prompt(rows_t, cmp_pos_w, cmp_w, cmp_b):
    b, _, t = rows_t.shape
    tile = min(CMP_TILE, t)
    assert t % tile == 0
    w2, wbdt, biast = _cmp_weights(cmp_pos_w, cmp_w, cmp_b, tile)
    m_out = tile // CMP_STRIDE
    return pl.pallas_call(
        functools.partial(_compress_body, n_pages=1, n_prefetch=0), grid=(b, t // tile),
        in_specs=[pl.BlockSpec((None, CMP_ROWS, tile), lambda i, j: (i, 0, j)),
                  _const_spec(w2.shape), _const_spec(wbdt.shape), _const_spec(biast.shape)],
        out_specs=pl.BlockSpec((None, CMP_ROWS, m_out), lambda i, j: (i, 0, j)),
        out_shape=jax.ShapeDtypeStruct((b, CMP_ROWS, t // CMP_STRIDE), F32),
        scratch_shapes=[pltpu.VMEM((CMP_ROWS, m_out), F32)],
        compiler_params=_params(("parallel", "arbitrary")), name="compress_prompt")(rows_t, w2, wbdt, biast)


def _compress_paged(cache_t, page_table, cmp_pos_w, cmp_w, cmp_b):
    b, npg = page_table.shape
    pg = min(CMP_TILE // PAGE_SIZE, npg)
    assert npg % pg == 0
    tile = pg * PAGE_SIZE
    w2, wbdt, biast = _cmp_weights(cmp_pos_w, cmp_w, cmp_b, tile)
    m_out = tile // CMP_STRIDE
    specs = [pl.BlockSpec((None, CMP_ROWS, PAGE_SIZE),
                          functools.partial(lambda i, j, pt, p: (pt[i, j * pg + p], 0, 0), p=p))
             for p in range(pg)]
    return pl.pallas_call(
        functools.partial(_compress_body, n_pages=pg, n_prefetch=1),
        grid_spec=pltpu.PrefetchScalarGridSpec(
            num_scalar_prefetch=1, grid=(b, npg // pg),
            in_specs=specs + [_const_spec(w2.shape), _const_spec(wbdt.shape), _const_spec(biast.shape)],
            out_specs=pl.BlockSpec((None, CMP_ROWS, m_out), lambda i, j, pt: (i, 0, j)),
            scratch_shapes=[pltpu.VMEM((CMP_ROWS, m_out), F32)]),
        out_shape=jax.ShapeDtypeStruct((b, CMP_ROWS, npg * PAGE_SIZE // CMP_STRIDE), F32),
        compiler_params=_params(("parallel", "arbitrary")), name="compress_paged")(
            page_table, *([cache_t] * pg), w2, wbdt, biast)


def _alibi_slopes(rows_per_head):
    sl = (LOG2E * 2.0 ** (-8.0 * np.arange(1, NSA_HEADS + 1) / NSA_HEADS)).astype(np.float32).reshape(KV_HEADS, GQ)
    return jnp.asarray(np.repeat(sl, rows_per_head, axis=1)[:, :, None])


def _score_matrix(n_cmp_rows, n_blocks, lanes):
    m = np.arange(n_cmp_rows)[:, None]
    j = np.arange(lanes)[None, :]
    return jnp.asarray(((m >= 4 * j) & (m <= 4 * j + 4) & (m >= 1) & (j < n_blocks)).astype(np.float32))


def _group_queries(q_ref, g):
    heads = [q_ref[:, (g * GQ + i) * HEAD_DIM:(g * GQ + i + 1) * HEAD_DIM] for i in range(GQ)]
    return jnp.concatenate(heads, axis=0).astype(BF16)


def _softmax_rows(s, mask):
    s = jnp.where(mask, s, -jnp.inf)
    m = jnp.max(s, axis=-1, keepdims=True)
    m = jnp.where(m > -jnp.inf, m, 0.0)
    e = jnp.exp2(s - m)
    return e * (1.0 / jnp.maximum(jnp.sum(e, axis=-1, keepdims=True), 1e-30))


def _select_blocks(score, cur, n_blocks):
    r, lanes = score.shape
    blk = lax.broadcasted_iota(jnp.int32, (r, lanes), 1)
    forced = (blk == 0) | (blk == cur) | (blk == cur - 1)
    val = jnp.where(forced, jnp.inf, jnp.where(blk <= cur, score, -jnp.inf))
    rank = jnp.zeros((r, lanes), jnp.int32)
    for i in range(n_blocks):
        ci = val[:, i:i + 1]
        ahead = (ci > val) | ((ci == val) & (blk > i))
        rank = rank + ahead.astype(jnp.int32)
    return ((rank < min(SEL_TOPN, n_blocks)) & (blk < n_blocks)).astype(F32)


POS_ROWS = 16


def _slope_pieces(rows_per_head):
    sl = _alibi_slopes(rows_per_head)
    hi = sl.astype(BF16)
    mid = (sl - hi.astype(F32)).astype(BF16)
    lo = (sl - hi.astype(F32) - mid.astype(F32)).astype(BF16)
    pad = jnp.zeros(sl.shape[:2] + (POS_ROWS - 6,), BF16)
    return jnp.concatenate([hi, mid, lo, hi, mid, lo, pad], axis=2)


def _pos_rows(t_len, stride=1, offset=0):
    k = stride * np.arange(t_len) + offset
    hi = (SEL_BLOCK * (k // SEL_BLOCK)).astype(np.float32)
    lo = (k % SEL_BLOCK).astype(np.float32)
    return np.stack([hi, hi, hi, lo, lo, lo] + [np.zeros(t_len, np.float32)] * (POS_ROWS - 6))


def _block_rows(n_rows, n_keys):
    return (np.arange(n_rows)[:, None] == (np.arange(n_keys) // SEL_BLOCK)[None, :]).astype(np.float32)


def _key_constants(t_len):
    both = np.concatenate([_pos_rows(t_len), _block_rows(t_len // SEL_BLOCK, t_len)], axis=0)
    return jnp.asarray(both).astype(BF16)


def _select_bias_t(score_t, cur):
    n_blocks, r = score_t.shape
    blk = lax.broadcasted_iota(jnp.int32, (n_blocks, r), 0)
    forced = (blk == 0) | (blk == cur) | (blk == cur - 1)
    val = jnp.where(forced, jnp.inf, jnp.where(blk <= cur, score_t, -jnp.inf))
    rank = jnp.zeros((n_blocks, r), F32)
    for i in range(n_blocks):
        ci = val[i:i + 1, :]
        ahead = (ci > val) | ((ci == val) & (blk > i))
        rank = rank + jnp.where(ahead, 1.0, 0.0)
    return jnp.where((rank < min(SEL_TOPN, n_blocks)) & (blk <= cur), 0.0, NEG)


def _nsa_prompt_body(q_ref, gates_ref, kvc_ref, sel_ref, win_ref, saug_ref, kconst_ref, cconst_ref,
                     smat_ref, o_ref, need_sc, *, t_len, kc, wb):
    bi = pl.program_id(1)
    p0 = bi * Q_BLOCK
    nrow = GQ * Q_BLOCK
    n_cmp = kvc_ref.shape[1]
    n_blocks = t_len // SEL_BLOCK
    qpos = lax.broadcasted_iota(jnp.int32, (Q_BLOCK, 1), 0) + p0
    tpos = jnp.concatenate([qpos] * GQ, axis=0)
    gates = gates_ref[...]

    qgs = [_group_queries(q_ref, g) for g in range(KV_HEADS)]
    q_pos = [jnp.concatenate([qgs[g], saug_ref[g]], axis=1) for g in range(KV_HEADS)]

    cpos = CMP_STRIDE * lax.broadcasted_iota(jnp.int32, (1, n_cmp), 1) + (CMP_STRIDE - 1)
    dist = tpos - cpos
    cmask = (dist >= 0) & (cpos >= 2 * CMP_STRIDE - 1)

    def compressed_scores(g):
        ks = g * HEAD_DIM
        katc = jnp.concatenate([kvc_ref[ks:ks + HEAD_DIM, :].astype(BF16), cconst_ref[...]], axis=0)
        return _dot(q_pos[g], katc)

    wstart = pl.multiple_of(jnp.clip((p0 - WINDOW) // 128 * 128, 0, t_len - wb), 128)
    wpos = wstart + lax.broadcasted_iota(jnp.int32, (1, wb), 1)
    wmask = (wpos <= tpos) & (wpos > tpos - WINDOW)

    def window_scores(g):
        katw = jnp.concatenate([win_ref[g * HEAD_DIM:(g + 1) * HEAD_DIM, pl.ds(wstart, wb)].astype(BF16),
                                kconst_ref[0:POS_ROWS, pl.ds(wstart, wb)]], axis=0)
        return _dot(q_pos[g], katw)

    def window_output(g, e_win):
        vs = (KV_HEADS + g) * HEAD_DIM
        vwt = jnp.concatenate([win_ref[vs:vs + HEAD_DIM, pl.ds(wstart, wb)].astype(BF16),
                               jnp.ones((POS_ROWS, wb), BF16)], axis=0)
        acc_w = _dot_nt(e_win, vwt)
        return acc_w[:, :HEAD_DIM] / acc_w[:, HEAD_DIM:HEAD_DIM + 1]

    s_cmp = [compressed_scores(g) for g in range(KV_HEADS)]
    s_win = [window_scores(g) for g in range(KV_HEADS)]
    p_cmps = [_softmax_rows(s_cmp[g], cmask) for g in range(KV_HEADS)]
    e_wins = []
    for g in range(KV_HEADS):
        sw = jnp.where(wmask, s_win[g], -jnp.inf)
        e_wins.append(jnp.exp2(sw - jnp.max(sw, axis=1, keepdims=True)).astype(BF16))
    o_cmps = [_dot_nt(p_cmps[g].astype(BF16), kvc_ref[(KV_HEADS + g) * HEAD_DIM:(KV_HEADS + g + 1) * HEAD_DIM, :]
                      .astype(BF16)) for g in range(KV_HEADS)]
    imps = [jnp.sum(p_cmps[g].reshape(GQ, Q_BLOCK, n_cmp), axis=0) for g in range(KV_HEADS)]
    o_wins = [window_output(g, e_wins[g]) for g in range(KV_HEADS)]
    score_t = _dot_nt(smat_ref[...], _split3(jnp.concatenate(imps, axis=0), 1))
    bias_f = _select_bias_t(score_t, bi)
    bias_qb = bias_f.T
    biases = [bias_qb[g * Q_BLOCK:(g + 1) * Q_BLOCK] for g in range(KV_HEADS)]
    q_aug = [jnp.concatenate([q_pos[g], jnp.concatenate([biases[g]] * GQ, axis=0).astype(BF16)], axis=1)
             for g in range(KV_HEADS)]
    for hh in range(NSA_HEADS):
        g, r = hh // GQ, slice((hh % GQ) * Q_BLOCK, (hh % GQ + 1) * Q_BLOCK)
        o_ref[:, hh * HEAD_DIM:(hh + 1) * HEAD_DIM] = (
            gates[:, 3 * hh:3 * hh + 1] * o_cmps[g][r] + gates[:, 3 * hh + 2:3 * hh + 3] * o_wins[g][r])

    ones_rows = jnp.ones((POS_ROWS, kc), BF16)

    def chunk_scores(g, c):
        k0 = pl.multiple_of(c * kc, kc)
        kat = jnp.concatenate([sel_ref[g * HEAD_DIM:(g + 1) * HEAD_DIM, pl.ds(k0, kc)].astype(BF16),
                               kconst_ref[:, pl.ds(k0, kc)]], axis=0)
        return _dot(q_aug[g], kat), k0

    def probabilities(s, m_i):
        m_new = jnp.maximum(m_i, jnp.max(s, axis=1, keepdims=True))
        return m_new, jnp.exp2(s - m_new).astype(BF16), jnp.exp2(m_i - m_new)

    def weighted_values(g, pr, k0):
        vs = (KV_HEADS + g) * HEAD_DIM
        vt = jnp.concatenate([sel_ref[vs:vs + HEAD_DIM, pl.ds(k0, kc)].astype(BF16), ones_rows], axis=0)
        return _dot_nt(pr, vt)

    def chunk_update(c, carry, mask=None):
        scores = [chunk_scores(g, c) for g in range(KV_HEADS)]
        if mask is not None:
            scores = [(jnp.where(mask(k0), s, NEG), k0) for s, k0 in scores]
        probs = [probabilities(scores[g][0], carry[g][0]) for g in range(KV_HEADS)]
        out = []
        for g in range(KV_HEADS):
            m_new, pr, alpha = probs[g]
            out.append((m_new, alpha * carry[g][1] + weighted_values(g, pr, scores[g][1])))
        return tuple(out)

    last = (p0 + Q_BLOCK - 1) // kc
    blocks_per_chunk = kc // SEL_BLOCK
    n_need = jnp.int32(0)
    for c in range(t_len // kc - 1):
        picked = bias_f[c * blocks_per_chunk:(c + 1) * blocks_per_chunk, :] == 0.0
        need = jnp.where(c < last, jnp.max(jnp.where(picked, 1, 0)), 0)
        need_sc[n_need] = c
        n_need = n_need + need

    one = (jnp.full((nrow, 1), NEG, F32), jnp.zeros((nrow, HEAD_DIM + POS_ROWS), F32))
    carry = lax.fori_loop(0, n_need, lambda i, carry: chunk_update(need_sc[i], carry), (one,) * KV_HEADS)
    causal = lambda k0: k0 + lax.broadcasted_iota(jnp.int32, (1, kc), 1) <= tpos
    carry = chunk_update(last, carry, mask=causal)
    for g in range(KV_HEADS):
        acc_s = carry[g][1]
        o_sel = acc_s[:, :HEAD_DIM] / jnp.maximum(acc_s[:, HEAD_DIM:HEAD_DIM + 1], 1e-30)
        for i in range(GQ):
            hh = g * GQ + i
            r = slice(i * Q_BLOCK, (i + 1) * Q_BLOCK)
            o_ref[:, hh * HEAD_DIM:(hh + 1) * HEAD_DIM] += gates[:, 3 * hh + 1:3 * hh + 2] * o_sel[r]


def _nsa_prompt(q, gates, kvc_t, rows_t, win_t):
    b, t, _ = q.shape
    assert t % 128 == 0
    kc = 512 if t % 512 == 0 else t
    wb = min(WINDOW + 128, t)
    n_cmp = kvc_t.shape[2]
    n_blocks = t // SEL_BLOCK
    ncol = 4 * HEAD_DIM
    saug = _slope_pieces(Q_BLOCK)
    kconst = _key_constants(t)
    cconst = jnp.asarray(_pos_rows(n_cmp, CMP_STRIDE, CMP_STRIDE - 1)).astype(BF16)
    smat = jnp.tile(_score_matrix(n_cmp, n_blocks, n_blocks).T, (1, 3)).astype(BF16)
    return pl.pallas_call(
        functools.partial(_nsa_prompt_body, t_len=t, kc=kc, wb=wb), grid=(b, t // Q_BLOCK),
        in_specs=[pl.BlockSpec((None, Q_BLOCK, NSA_WIDTH), lambda i, j: (i, j, 0)),
                  pl.BlockSpec((None, Q_BLOCK, 3 * NSA_HEADS), lambda i, j: (i, j, 0)),
                  pl.BlockSpec((None, ncol, n_cmp), lambda i, j: (i, 0, 0)),
                  pl.BlockSpec((None, ncol, t), lambda i, j: (i, 1, 0)),
                  pl.BlockSpec((None, ncol, t), lambda i, j: (i, 0, 0)),
                  _const_spec(saug.shape), _const_spec(kconst.shape), _const_spec(cconst.shape),
                  _const_spec(smat.shape)],
        out_specs=pl.BlockSpec((None, Q_BLOCK, NSA_WIDTH), lambda i, j: (i, j, 0)),
        out_shape=jax.ShapeDtypeStruct((b, t, NSA_WIDTH), F32),
        scratch_shapes=[pltpu.SMEM((t // kc,), jnp.int32)],
        compiler_params=_params(("parallel", "arbitrary")), name="nsa_prompt")(
            q, gates, kvc_t, rows_t, win_t, saug, kconst, cconst, smat)


def _nsa_dec_select_body(q_ref, kvc_ref, slope_ref, smat_ref, ocmp_ref, sel_ref, *, past_len, tq, n_blocks,
                         step_blocks):
    n_cmp = kvc_ref.shape[1]
    n_steps = sel_ref.shape[1] - 1
    lanes_out = sel_ref.shape[3]
    nrow = GQ * tq
    qpos = lax.broadcasted_iota(jnp.int32, (tq, 1), 0) + past_len
    tpos = jnp.concatenate([qpos] * GQ, axis=0)
    cpos = CMP_STRIDE * lax.broadcasted_iota(jnp.int32, (1, n_cmp), 1) + (CMP_STRIDE - 1)
    dist = tpos - cpos
    for g in range(KV_HEADS):
        qg = _group_queries(q_ref, g)
        ks, vs = g * HEAD_DIM, (KV_HEADS + g) * HEAD_DIM
        s = _dot(qg, kvc_ref[ks:ks + HEAD_DIM, :].astype(BF16)) - slope_ref[g] * dist.astype(F32)
        p_cmp = _softmax_rows(s, (dist >= 0) & (cpos >= 2 * CMP_STRIDE - 1))
        ocmp_ref[g] = _dot_nt(p_cmp.astype(BF16), kvc_ref[vs:vs + HEAD_DIM, :].astype(BF16))
        imp = jnp.sum(p_cmp.reshape(GQ, tq, n_cmp), axis=0)
        score = _dot(_split3(imp, 1), smat_ref[...])
        bias = jnp.where(_select_blocks(score, qpos // SEL_BLOCK, n_blocks) > 0.5, 0.0, NEG)
        pad = jnp.zeros((tq, lanes_out - step_blocks), F32)
        for st in range(n_steps):
            sel_ref[g, st] = jnp.concatenate([bias[:, st * step_blocks:(st + 1) * step_blocks], pad], axis=1)
        last_blk = n_steps * step_blocks
        sel_ref[g, n_steps] = jnp.concatenate([bias[:, last_blk:last_blk + 1],
                                               jnp.zeros((tq, lanes_out - 1), F32)], axis=1)


def _nsa_dec_select(q, kvc, past_len, n_blocks, step_blocks):
    b, tq, _ = q.shape
    n_cmp = kvc.shape[2]
    lanes = -(-n_blocks // 128) * 128
    n_steps = (n_blocks - 1) // step_blocks
    assert n_steps * step_blocks == n_blocks - 1 and step_blocks <= 128
    slopes = _alibi_slopes(tq)
    smat = jnp.tile(_score_matrix(n_cmp, n_blocks, lanes), (3, 1)).astype(BF16)
    return pl.pallas_call(
        functools.partial(_nsa_dec_select_body, past_len=past_len, tq=tq, n_blocks=n_blocks,
                          step_blocks=step_blocks), grid=(b,),
        in_specs=[pl.BlockSpec((None, tq, NSA_WIDTH), lambda i: (i, 0, 0)),
                  pl.BlockSpec((None, 4 * HEAD_DIM, n_cmp), lambda i: (i, 0, 0)),
                  _const_spec(slopes.shape), _const_spec(smat.shape)],
        out_specs=[pl.BlockSpec((None, KV_HEADS, GQ * tq, HEAD_DIM), lambda i: (i, 0, 0, 0)),
                   pl.BlockSpec((None, KV_HEADS, n_steps + 1, tq, 128), lambda i: (i, 0, 0, 0, 0))],
        out_shape=[jax.ShapeDtypeStruct((b, KV_HEADS, GQ * tq, HEAD_DIM), F32),
                   jax.ShapeDtypeStruct((b, KV_HEADS, n_steps + 1, tq, 128), F32)],
        compiler_params=_params(("parallel",)), name="nsa_dec_select")(q, kvc, slopes, smat)


def _nsa_dec_attend_body(pt_ref, q_ref, selb_ref, ocmp_ref, gates_ref, newsel_ref, winbuf_ref, newwin_ref,
                         slope_ref, saug_ref, pos_ref, blk_ref, *rest, past_len, tq, t_valid, n_pages):
    pages = rest[:n_pages]
    o_ref, m_sc, acc_sc = rest[n_pages:]
    step = pl.program_id(1)
    nrow = GQ * tq
    qpos = lax.broadcasted_iota(jnp.int32, (tq, 1), 0) + past_len
    tpos = jnp.concatenate([qpos] * GQ, axis=0)

    @pl.when(step == 0)
    def _():
        m_sc[...] = jnp.full(m_sc.shape, NEG, F32)
        acc_sc[...] = jnp.zeros(acc_sc.shape, F32)

    def update(g, sc, weighted_values):
        m_i = m_sc[g]
        m_new = jnp.maximum(m_i, jnp.max(sc, axis=1, keepdims=True))
        pr = jnp.exp2(sc - m_new)
        acc_sc[g] = jnp.exp2(m_i - m_new) * acc_sc[g] + weighted_values(pr.astype(BF16))
        m_sc[g] = m_new

    nk = n_pages * PAGE_SIZE
    ones_rows = jnp.ones((POS_ROWS, nk), BF16)
    scores = []
    for g in range(KV_HEADS):
        qg = _group_queries(q_ref, g)
        ks = g * HEAD_DIM
        bias = jnp.concatenate([selb_ref[g, step]] * GQ, axis=0).astype(BF16)
        q_aug = jnp.concatenate([qg, saug_ref[g], bias], axis=1)
        kat = jnp.concatenate([jnp.concatenate([pg[ks:ks + HEAD_DIM, :] for pg in pages], axis=1).astype(BF16),
                               pos_ref[...], blk_ref[...]], axis=0)
        scores.append(_dot(q_aug, kat))
    for g in range(KV_HEADS):
        vs = (KV_HEADS + g) * HEAD_DIM
        vt = jnp.concatenate([jnp.concatenate([pg[vs:vs + HEAD_DIM, :] for pg in pages], axis=1).astype(BF16),
                              ones_rows], axis=0)
        update(g, scores[g], lambda p, vt=vt: _dot_nt(p, vt))

    @pl.when(step == pl.num_programs(1) - 1)
    def _():
        gates = gates_ref[...]
        tn = newsel_ref.shape[0]
        npos = past_len + lax.broadcasted_iota(jnp.int32, (1, tn), 1)
        nd = tpos - npos
        valid_new = (nd >= 0) & (npos < past_len + t_valid)
        wlen = winbuf_ref.shape[1]
        wpos = past_len - wlen + lax.broadcasted_iota(jnp.int32, (1, wlen), 1)
        wd = tpos - wpos
        for g in range(KV_HEADS):
            qg = _group_queries(q_ref, g)
            ks, vs = g * HEAD_DIM, (KV_HEADS + g) * HEAD_DIM
            slope = slope_ref[g]
            block_bias = jnp.concatenate([selb_ref[g, selb_ref.shape[1] - 1][:, 0:1]] * GQ, axis=0)
            sc = (_dot_nt(qg, newsel_ref[:, ks:ks + HEAD_DIM].astype(BF16)) + slope * npos.astype(F32)
                  + block_bias)
            new_v = jnp.concatenate([newsel_ref[:, vs:vs + HEAD_DIM].astype(BF16),
                                     jnp.ones((tn, POS_ROWS), BF16)], axis=1)
            update(g, jnp.where(valid_new, sc, NEG), lambda p, new_v=new_v: _dot(p, new_v))
            acc = acc_sc[g]
            o_sel = acc[:, :HEAD_DIM] / jnp.maximum(acc[:, HEAD_DIM:HEAD_DIM + 1], 1e-30)
            s1 = _dot(qg, winbuf_ref[ks:ks + HEAD_DIM, :].astype(BF16)) - slope * wd.astype(F32)
            s2 = _dot_nt(qg, newwin_ref[:, ks:ks + HEAD_DIM].astype(BF16)) - slope * nd.astype(F32)
            ok1 = (wd >= 0) & (wd < WINDOW) & (wpos >= 0)
            ok2 = valid_new & (nd < WINDOW)
            s1 = jnp.where(ok1, s1, -jnp.inf)
            s2 = jnp.where(ok2, s2, -jnp.inf)
            mx = jnp.maximum(jnp.max(s1, axis=1, keepdims=True), jnp.max(s2, axis=1, keepdims=True))
            mx = jnp.where(mx > -jnp.inf, mx, 0.0)
            e1 = jnp.exp2(s1 - mx)
            e2 = jnp.exp2(s2 - mx)
            den = jnp.maximum(jnp.sum(e1, axis=1, keepdims=True) + jnp.sum(e2, axis=1, keepdims=True), 1e-30)
            o_win = (_dot_nt(e1.astype(BF16), winbuf_ref[vs:vs + HEAD_DIM, :].astype(BF16))
                     + _dot(e2.astype(BF16), newwin_ref[:, vs:vs + HEAD_DIM].astype(BF16))) / den
            o_cmp = ocmp_ref[g]
            for i in range(GQ):
                hh = g * GQ + i
                r = slice(i * tq, (i + 1) * tq)
                o_ref[:, hh * HEAD_DIM:(hh + 1) * HEAD_DIM] = (
                    gates[:, 3 * hh:3 * hh + 1] * o_cmp[r] + gates[:, 3 * hh + 1:3 * hh + 2] * o_sel[r]
                    + gates[:, 3 * hh + 2:3 * hh + 3] * o_win[r])


DEC_PAGES = 16


def _nsa_dec_attend(q, selbias, o_cmp, gates, newsel, winbuf, newwin, cache, page_table, past_len, t_valid):
    b, tq, _ = q.shape
    npg = page_table.shape[1]
    pg = min(DEC_PAGES, npg)
    assert npg % pg == 0 and selbias.shape[2] == npg // pg + 1
    nk = pg * PAGE_SIZE
    ncol = 4 * HEAD_DIM
    slopes = _alibi_slopes(tq)
    saug = _slope_pieces(tq)
    pos_rows = jnp.asarray(_pos_rows(past_len)).astype(BF16)
    blk_rows = jnp.asarray(_block_rows(selbias.shape[-1], nk)).astype(BF16)
    wlen = winbuf.shape[2]
    bspec = lambda shape: pl.BlockSpec((None,) + shape, lambda i, j, pt: (i,) + (0,) * len(shape))
    page_specs = [pl.BlockSpec((None, ncol, PAGE_SIZE),
                               functools.partial(lambda i, j, pt, p: (pt[i, j * pg + p], 1, 0), p=p))
                  for p in range(pg)]
    return pl.pallas_call(
        functools.partial(_nsa_dec_attend_body, past_len=past_len, tq=tq, t_valid=t_valid, n_pages=pg),
        grid_spec=pltpu.PrefetchScalarGridSpec(
            num_scalar_prefetch=1, grid=(b, npg // pg),
            in_specs=[pl.BlockSpec((None, tq, NSA_WIDTH), lambda i, j, pt: (i, 0, 0)),
                      bspec(selbias.shape[1:]), bspec((KV_HEADS, GQ * tq, HEAD_DIM)),
                      bspec((tq, 3 * NSA_HEADS)), bspec((tq, ncol)), bspec((ncol, wlen)), bspec((tq, ncol)),
                      _const_spec(slopes.shape), _const_spec(saug.shape),
                      pl.BlockSpec((POS_ROWS, nk), lambda i, j, pt: (0, j)), _const_spec(blk_rows.shape)]
            + page_specs,
            out_specs=bspec((tq, NSA_WIDTH)),
            scratch_shapes=[pltpu.VMEM((KV_HEADS, GQ * tq, 1), F32),
                            pltpu.VMEM((KV_HEADS, GQ * tq, HEAD_DIM + POS_ROWS), F32)]),
        out_shape=jax.ShapeDtypeStruct((b, tq, NSA_WIDTH), F32),
        compiler_params=_params(("parallel", "arbitrary")), name="nsa_dec_attend")(
            page_table, q, selbias, o_cmp, gates, newsel, winbuf, newwin, slopes, saug, pos_rows, blk_rows,
            *([cache] * pg))


def _pad_t(a, axis, to):
    pad = [(0, 0)] * a.ndim
    pad[axis] = (0, to - a.shape[axis])
    return jnp.pad(a, pad)


def _even_layer(x, b, t, pos0, pool_buf, c0, n0, m0, g, w_in, b_gate, w_pool, pool_scale, w_out):
    n = b * t
    u, q, k, v, og, vt, gc, gr = _even_in(x, g, w_in, b_gate)
    u3 = u.reshape(b, t, POOL_WIDTH)
    new_pool = jnp.concatenate([pool_buf, u3], axis=1)[:, -pool_buf.shape[1]:]
    h4 = ML_HEADS
    if t % 256 == 0:
        tp, L = t, 256
        vt_in, gr_in = vt, gr
    else:
        tp = L = -(-t // 8) * 8
        vt_in = _pad_t(vt.reshape(h4, ML_DH, b, t), 3, tp).transpose(0, 2, 1, 3)
        gr_in = _pad_t(gr.reshape(2 * h4, b, t), 2, tp).transpose(1, 0, 2)
    seq = lambda a: _pad_t(a.reshape(h4, b, t, ML_DH), 2, tp)
    ypool = _pool(_pad_t(u3, 1, tp), pool_buf, w_pool, pool_scale, pos0)[:, :t].reshape(n, POOL_WIDTH)
    hm, c, nn, m = _mlstm(seq(q), seq(k), seq(v), vt_in, _pad_t(gc.reshape(b, t, 2 * h4), 1, tp), gr_in,
                          c0, n0, m0, L=L, t_valid=t)
    hm = hm[:, :, :t].reshape(h4, n, ML_DH)
    x = _even_out(x, ypool, og, hm, w_out)
    return x, new_pool, c, nn.reshape(b, h4, ML_DH), m.reshape(b, h4)


def _odd_layer(x, b, t, pos0, cache, page_table, win_buf, conv_buf, g, w_in, cmp_pos_w, cmp_w, cmp_b,
               conv_w, w_out):
    n = b * t
    ncol = 4 * HEAD_DIM
    prompt = cache is None
    q, rows_t, win_t, gates, ucv, bg = _odd_in(x, g, w_in, t if prompt else n)
    u3 = ucv.reshape(b, t, CONV_CH)
    new_conv = jnp.concatenate([conv_buf, u3], axis=1)[:, -(CONV_K - 1):]
    if prompt:
        kvc_t = _compress_prompt(rows_t, cmp_pos_w, cmp_w, cmp_b)
        o = _nsa_prompt(q.reshape(b, t, NSA_WIDTH), gates.reshape(b, t, 3 * NSA_HEADS), kvc_t, rows_t, win_t)
        new_win_t = win_t[:, :, t - min(WINDOW, t):]
        new_rows = rows_t.reshape(b, 4, KV_HEADS, HEAD_DIM, t).transpose(0, 4, 1, 2, 3)
        tp = t
    else:
        assert pos0 % SEL_BLOCK + t <= SEL_BLOCK
        tp = -(-t // 8) * 8
        rows_bt = rows_t[0].T.reshape(b, t, 2 * ncol)
        win_bt = win_t[0].T.reshape(b, t, ncol)
        n_pool = cache.shape[0]
        cache_t = cache.transpose(0, 2, 3, 4, 1).reshape(n_pool, 2 * ncol, PAGE_SIZE)
        kvc_t = _compress_paged(cache_t, page_table, cmp_pos_w, cmp_w, cmp_b)
        qp = _pad_t(q.reshape(b, t, NSA_WIDTH), 1, tp)
        n_blocks = -(-(pos0 + t) // SEL_BLOCK)
        step_blocks = min(DEC_PAGES, page_table.shape[1]) * PAGE_SIZE // SEL_BLOCK
        o_cmp, selmask = _nsa_dec_select(qp, kvc_t, pos0, n_blocks, step_blocks)
        wlen = win_buf.shape[1]
        winb_t = win_buf.transpose(0, 2, 3, 4, 1).reshape(b, ncol, wlen)
        o = _nsa_dec_attend(qp, selmask, o_cmp, _pad_t(gates.reshape(b, t, -1), 1, tp),
                            _pad_t(rows_bt[:, :, ncol:], 1, tp), winb_t, _pad_t(win_bt, 1, tp),
                            cache_t, page_table, pos0, t)[:, :t]
        new_win_t = jnp.concatenate([winb_t, win_bt.transpose(0, 2, 1)], axis=2)[:, :, -wlen:]
        new_rows = rows_bt.reshape(b, t, 4, KV_HEADS, HEAD_DIM)
    yconv = _conv(_pad_t(u3, 1, tp), _pad_t(bg.reshape(b, t, CONV_CH), 1, tp), conv_buf, conv_w)[:, :t]
    x = _odd_out(x, o.reshape(n, NSA_WIDTH), yconv.reshape(n, CONV_CH), w_out)
    new_win = new_win_t.reshape(b, 2, KV_HEADS, HEAD_DIM, new_win_t.shape[2]).transpose(0, 4, 1, 2, 3)
    return x, new_rows, new_win, new_conv


def _trunk(x3, pos0, pool_buf, ml_c, ml_n, ml_m, kv_cache, page_table, win_buf, conv_buf, p):
    b, t, d = x3.shape
    depth = p["norm_g"].shape[0]
    x = x3.reshape(b * t, d)
    pools, cs, ns, ms, rows, wins, convs = [], [], [], [], [], [], []
    for l in range(depth):
        j = l // 2
        x = _ffn(x, p["norm_g"][l, 0], p["w_ffn_in"][l][0], p["w_ffn_out"][l][0])
        if l % 2 == 0:
            x, pb, c, n, m = _even_layer(x, b, t, pos0, pool_buf[j], ml_c[j], ml_n[j], ml_m[j], p["norm_g"][l, 1],
                                         p["w_in_even"][j], p["b_gate_even"][j], p["w_pool"][j],
                                         p["pool_scale"][j], p["w_out_even"][j])
            pools.append(pb)
            cs.append(c)
            ns.append(n)
            ms.append(m)
        else:
            cache = None if kv_cache is None else kv_cache[j]
            wb = None if win_buf is None else win_buf[j]
            x, r, wn, cn = _odd_layer(x, b, t, pos0, cache, page_table, wb, conv_buf[j], p["norm_g"][l, 1],
                                      p["w_in_odd"][j], p["cmp_pos_w"][j], p["cmp_w"][j], p["cmp_b"][j],
                                      p["conv_w"][j], p["w_out_odd"][j])
            rows.append(r)
            wins.append(wn)
            convs.append(cn)
        x = _ffn(x, p["norm_g"][l, 2], p["w_ffn_in"][l][1], p["w_ffn_out"][l][1],
                 final_g=p["final_g"] if l == depth - 1 else None)
    states = (jnp.stack(pools), jnp.stack(cs), jnp.stack(ns), jnp.stack(ms),
              jnp.stack(rows), jnp.stack(wins), jnp.stack(convs))
    return x.reshape(b, t, d), states


def kernel(x_prompt, x_sample, state_pool, state_mlstm_c, state_mlstm_n, state_mlstm_m, cache_nsa_kv, state_win_kv, state_conv, page_table, norm_g, final_g, w_ffn_in, w_ffn_out, w_in_even, b_gate_even, w_pool, pool_scale, w_out_even, w_in_odd, cmp_pos_w, cmp_w, cmp_b, conv_w, w_out_odd):
    bp = x_prompt.shape[0]
    n_even, n_odd = state_pool.shape[0], state_conv.shape[0]
    past_len = page_table.shape[1] * PAGE_SIZE
    p = dict(norm_g=norm_g, final_g=final_g, w_ffn_in=w_ffn_in.astype(BF16), w_ffn_out=w_ffn_out.astype(BF16),
             w_in_even=w_in_even, b_gate_even=b_gate_even, w_pool=w_pool, pool_scale=pool_scale,
             w_out_even=w_out_even, w_in_odd=w_in_odd, cmp_pos_w=cmp_pos_w, cmp_w=cmp_w, cmp_b=cmp_b,
             conv_w=conv_w, w_out_odd=w_out_odd)
    pool0 = jnp.zeros((n_even, bp) + state_pool.shape[2:], F32)
    c0 = jnp.zeros((n_even, bp) + state_mlstm_c.shape[2:], F32)
    n0 = jnp.zeros((n_even, bp) + state_mlstm_n.shape[2:], F32)
    m0 = jnp.zeros((n_even, bp) + state_mlstm_m.shape[2:], F32)
    conv0 = jnp.zeros((n_odd, bp) + state_conv.shape[2:], F32)
    y_p, (pool_p, c_p, n_p, m_p, kv_p, win_p, conv_p) = _trunk(
        x_prompt, 0, pool0, c0, n0, m0, None, None, None, conv0, p)
    y_s, (pool_s, c_s, n_s, m_s, kv_s, win_s, conv_s) = _trunk(
        x_sample, past_len, state_pool, state_mlstm_c, state_mlstm_n, state_mlstm_m,
        cache_nsa_kv, page_table, state_win_kv, state_conv, p)
    return (y_p, y_s, pool_p, pool_s, c_p, c_s, n_p, n_s, m_p, m_s,
            kv_p, kv_s, win_p, win_s, conv_p, conv_s)
```

```python
import functools

import numpy as np
import jax
import jax.numpy as jnp
from jax import lax
from jax.experimental import pallas as pl
from jax.experimental.pallas import tpu as pltpu

F32 = jnp.float32
BF16 = jnp.bfloat16

EPS = 1e-6
POOL_WINDOWS = (2, 4, 8, 16)
POOL_GDIM = 64
POOL_WIDTH = 256
POOL_HALO = 16
ML_HEADS = 4
ML_DH = 192
ML_WIDTH = ML_HEADS * ML_DH
NSA_HEADS = 12
HEAD_DIM = 64
NSA_WIDTH = NSA_HEADS * HEAD_DIM
KV_HEADS = 2
GQ = NSA_HEADS // KV_HEADS
CMP_STRIDE = 16
SEL_BLOCK = 64
SEL_TOPN = 16
WINDOW = 512
Q_BLOCK = 64
PAGE_SIZE = 128
CONV_CH = 256
CONV_K = 3
CONV_HALO = 8

VMEM_LIMIT = 56 * 1024 * 1024
NEG = -1e30
LOG2E = 1.4426950408889634


def _params(sem, vmem=VMEM_LIMIT):
    return pltpu.CompilerParams(dimension_semantics=sem, vmem_limit_bytes=vmem)


def _const_spec(shape):
    nd = len(shape)
    return pl.BlockSpec(shape, lambda *_: (0,) * nd, pipeline_mode=pl.Buffered(1))


def _rms(x, g):
    return x * lax.rsqrt(jnp.mean(x * x, axis=-1, keepdims=True) + EPS) * g


def _dot(a, b):
    return jnp.dot(a, b, preferred_element_type=F32)


def _dot_nt(a, b):
    return lax.dot_general(a, b, (((1,), (1,)), ((), ())), preferred_element_type=F32)


def _split3(x, axis):
    hi = x.astype(BF16)
    r1 = x - hi.astype(F32)
    mid = r1.astype(BF16)
    lo = (r1 - mid.astype(F32)).astype(BF16)
    return jnp.concatenate([hi, mid, lo], axis=axis)


def _ffn_body(x_ref, g_ref, win_ref, wout_ref, *rest, d_ff, chunks, has_final):
    o_ref = rest[-1]
    x = x_ref[...]
    hn = _rms(x, g_ref[...]).astype(BF16)
    acc = jnp.zeros(x.shape, F32)
    off = 0
    for fc in chunks:
        a = _dot(hn, win_ref[:, off:off + fc])
        b = _dot(hn, win_ref[:, d_ff + off:d_ff + off + fc])
        act = (a * jax.nn.sigmoid(a) * b).astype(BF16)
        acc = acc + _dot(act, wout_ref[off:off + fc, :])
        off += fc
    y = x + 0.5 * acc
    if has_final:
        y = _rms(y, rest[0][...])
    o_ref[...] = y


def _ffn(x, g, w_in, w_out, final_g=None):
    n, d = x.shape
    d_ff = w_out.shape[0]
    tm = 512 if n % 512 == 0 else n
    chunks, left = [], d_ff
    while left:
        chunks.append(min(1024, left))
        left -= chunks[-1]
    row = pl.BlockSpec((tm, d), lambda i: (i, 0))
    in_specs = [row, _const_spec((1, d)), _const_spec(w_in.shape), _const_spec(w_out.shape)]
    args = [x, g.reshape(1, d), w_in, w_out]
    if final_g is not None:
        in_specs.append(_const_spec((1, d)))
        args.append(final_g.reshape(1, d))
    return pl.pallas_call(
        functools.partial(_ffn_body, d_ff=d_ff, chunks=tuple(chunks), has_final=final_g is not None),
        grid=(n // tm,), in_specs=in_specs, out_specs=row,
        out_shape=jax.ShapeDtypeStruct((n, d), F32),
        compiler_params=_params(("parallel",)), name="ffn")(*args)


def _even_in_body(x_ref, g_ref, wu_ref, wh_ref, wg_ref, wgt_ref, bg_ref, bgt_ref,
                  u_ref, q_ref, k_ref, v_ref, og_ref, vt_ref, gc_ref, gr_ref):
    hn = _rms(x_ref[...], g_ref[...]).astype(BF16)
    u_ref[...] = _dot(hn, wu_ref[...])
    for seg, out_ref in enumerate((q_ref, k_ref, v_ref, og_ref)):
        z = _dot(hn, wh_ref[seg])
        for h in range(ML_HEADS):
            out_ref[h] = z[:, h * ML_DH:(h + 1) * ML_DH]
            if out_ref is v_ref:
                vt_ref[h] = z[:, h * ML_DH:(h + 1) * ML_DH].T
    gc_ref[...] = _dot(hn, wg_ref[...]) + bg_ref[...]
    gr_ref[...] = _dot_nt(wgt_ref[...], hn) + bgt_ref[...]


def _even_in(x, g, w_in, b_gate):
    n, d = x.shape
    tm = 512 if n % 512 == 0 else n
    h4 = ML_HEADS
    wu = w_in[:, :POOL_WIDTH].astype(BF16)
    wh = w_in[:, POOL_WIDTH:POOL_WIDTH + 4 * ML_WIDTH].reshape(d, 4, ML_WIDTH).transpose(1, 0, 2).astype(BF16)
    wg = w_in[:, POOL_WIDTH + 4 * ML_WIDTH:].astype(BF16)
    row = lambda w: pl.BlockSpec((tm, w), lambda i: (i, 0))
    hrow = pl.BlockSpec((h4, tm, ML_DH), lambda i: (0, i, 0))
    hsh = jax.ShapeDtypeStruct((h4, n, ML_DH), F32)
    return pl.pallas_call(
        _even_in_body, grid=(n // tm,),
        in_specs=[row(d), _const_spec((1, d)), _const_spec(wu.shape), _const_spec(wh.shape),
                  _const_spec(wg.shape), _const_spec((2 * h4, d)),
                  _const_spec((1, 2 * h4)), _const_spec((2 * h4, 1))],
        out_specs=[row(POOL_WIDTH), hrow, hrow, hrow, hrow,
                   pl.BlockSpec((h4, ML_DH, tm), lambda i: (0, 0, i)),
                   row(2 * h4), pl.BlockSpec((2 * h4, tm), lambda i: (0, i))],
        out_shape=[jax.ShapeDtypeStruct((n, POOL_WIDTH), F32), hsh, hsh, hsh, hsh,
                   jax.ShapeDtypeStruct((h4, ML_DH, n), F32),
                   jax.ShapeDtypeStruct((n, 2 * h4), F32), jax.ShapeDtypeStruct((2 * h4, n), F32)],
        compiler_params=_params(("parallel",)), name="even_in")(
            x, g.reshape(1, d), wu, wh, wg, wg.T, b_gate.reshape(1, -1), b_gate.reshape(-1, 1))


def _pool_body(u_ref, pre_ref, w_ref, sc_ref, y_ref, carry, full, *, tb, pos0):
    t = pl.program_id(1)

    @pl.when(t == 0)
    def _():
        carry[...] = pre_ref[...]

    u = u_ref[...]
    full[0:POOL_HALO] = carry[...]
    full[POOL_HALO:] = u
    acc = full[...]
    sums = []
    for sh in (1, 2, 4, 8):
        acc = acc + pltpu.roll(acc, sh, 0)
        sums.append(acc[POOL_HALO:])
    lane = lax.broadcasted_iota(jnp.int32, (tb, POOL_WIDTH), 1)
    grp = lane // POOL_GDIM
    win = jnp.where(grp == 0, sums[0], jnp.where(grp == 1, sums[1], jnp.where(grp == 2, sums[2], sums[3])))
    width = jnp.where(grp == 0, 2, jnp.where(grp == 1, 4, jnp.where(grp == 2, 8, 16)))
    pos = pos0 + t * tb + lax.broadcasted_iota(jnp.int32, (tb, POOL_WIDTH), 0)
    cnt = jnp.minimum(pos + 1, width).astype(F32)
    mixed = (win / cnt - u).astype(BF16)
    y_ref[...] = _dot(mixed, w_ref[...]) * sc_ref[...]
    carry[...] = full[tb:tb + POOL_HALO]


def _pool(u, prefix, w_pool, scale, pos0):
    b, t, c = u.shape
    tb = 512 if t % 512 == 0 else t
    pre = jnp.pad(prefix, ((0, 0), (POOL_HALO - prefix.shape[1], 0), (0, 0)))
    wbd = jax.scipy.linalg.block_diag(*[w_pool[i] for i in range(w_pool.shape[0])]).astype(BF16)
    return pl.pallas_call(
        functools.partial(_pool_body, tb=tb, pos0=pos0), grid=(b, t // tb),
        in_specs=[pl.BlockSpec((None, tb, c), lambda i, j: (i, j, 0)),
                  pl.BlockSpec((None, POOL_HALO, c), lambda i, j: (i, 0, 0)),
                  _const_spec((c, c)), _const_spec((1, c))],
        out_specs=pl.BlockSpec((None, tb, c), lambda i, j: (i, j, 0)),
        out_shape=jax.ShapeDtypeStruct((b, t, c), F32),
        scratch_shapes=[pltpu.VMEM((POOL_HALO, c), F32), pltpu.VMEM((tb + POOL_HALO, c), F32)],
        compiler_params=_params(("parallel", "arbitrary")), name="pool")(u, pre, wbd, scale.reshape(1, c))


def _log_sigmoid(x):
    return jnp.minimum(x, 0.0) - jnp.log(1.0 + jnp.exp(-jnp.abs(x)))


def _mlstm_body(q_ref, k_ref, v_ref, vt_ref, gc_ref, gr_ref, c0_ref, n0_ref, m0_ref,
                h_ref, c_ref, n_ref, m_ref, *, L, t_valid):
    @pl.when(pl.program_id(1) == 0)
    def _():
        c_ref[...] = c0_ref[...]
        n_ref[...] = n0_ref[...]
        m_ref[...] = m0_ref[...]

    gc = gc_ref[...]
    gr = gr_ref[...]
    lf_c = _log_sigmoid(gc)
    lf_r = _log_sigmoid(gr)
    row = lax.broadcasted_iota(jnp.int32, (L, L), 0)
    col = lax.broadcasted_iota(jnp.int32, (L, L), 1)
    tok_c = lax.broadcasted_iota(jnp.int32, (L, 1), 0) < t_valid
    tok_r = lax.broadcasted_iota(jnp.int32, (1, L), 1) < t_valid
    if t_valid < L:
        lf_c = jnp.where(tok_c, lf_c, 0.0)
        lf_r = jnp.where(tok_r, lf_r, 0.0)
    causal = row >= col
    tri3 = jnp.concatenate([causal.astype(BF16)] * 3, axis=1)
    cs_c = _dot(tri3, _split3(lf_c, 0))
    cs_r = _dot_nt(_split3(lf_r, 1), tri3)
    heads = range(ML_HEADS)
    q = [q_ref[hd] for hd in heads]
    kf = [k_ref[hd] * (ML_DH ** -0.5) for hd in heads]
    qb = [x.astype(BF16) for x in q]
    kb = [x.astype(BF16) for x in kf]
    c_old = [c_ref[hd] for hd in heads]
    n_old = [n_ref[hd] for hd in heads]
    m_prev = [m_ref[hd] for hd in heads]
    qk = [_dot_nt(qb[hd], kb[hd]) for hd in heads]
    qc = [_dot_nt(qb[hd], c_old[hd].astype(BF16)) for hd in heads]
    i_c, i_r, b_c, b_r, mt, a, s = [], [], [], [], [], [], []
    for hd in heads:
        ic = gc[:, hd:hd + 1]
        ir = gr[hd:hd + 1, :]
        if t_valid < L:
            ic = jnp.where(tok_c, ic, -jnp.inf)
            ir = jnp.where(tok_r, ir, -jnp.inf)
        bc = cs_c[:, ML_HEADS + hd:ML_HEADS + hd + 1]
        br = cs_r[ML_HEADS + hd:ML_HEADS + hd + 1, :]
        dmat = jnp.where(causal, bc - br + ir, -jnp.inf)
        inter = bc + m_prev[hd]
        mth = jnp.maximum(jnp.max(dmat, axis=1, keepdims=True), inter)
        s.append(qk[hd] * jnp.exp(dmat - mth))
        a.append(jnp.exp(inter - mth))
        i_c.append(ic), i_r.append(ir), b_c.append(bc), b_r.append(br), mt.append(mth)
    sv = [_dot(s[hd].astype(BF16), v_ref[hd].astype(BF16)) for hd in heads]
    wk_r, wk_c, decay, m_new = [], [], [], []
    for hd in heads:
        b_last = b_c[hd][L - 1:L, :]
        ge_r = b_last - b_r[hd] + i_r[hd]
        ge_c = b_last - b_c[hd] + i_c[hd]
        mn = jnp.maximum(b_last + m_prev[hd], jnp.max(ge_r, axis=1, keepdims=True))
        wk_r.append(jnp.exp(ge_r - mn))
        wk_c.append(jnp.exp(ge_c - mn))
        decay.append(jnp.exp(b_last + m_prev[hd] - mn))
        m_new.append(mn)
    kv = [_dot((vt_ref[hd] * wk_r[hd]).astype(BF16), kb[hd]) for hd in heads]
    for hd in heads:
        num = sv[hd] + a[hd] * qc[hd]
        den = jnp.sum(s[hd], axis=1, keepdims=True) + a[hd] * jnp.sum(q[hd] * n_old[hd], axis=1, keepdims=True)
        h_ref[hd] = num / jnp.maximum(jnp.abs(den), jnp.exp(-mt[hd]))
    for hd in heads:
        c_ref[hd] = decay[hd] * c_old[hd] + kv[hd]
        n_ref[hd] = decay[hd] * n_old[hd] + jnp.sum(kf[hd] * wk_c[hd], axis=0, keepdims=True)
        m_ref[hd] = m_new[hd]


def _mlstm(q, k, v, vt, gc, gr, c0, n0, m0, *, L, t_valid):
    h4, b, t, dh = q.shape
    nc = t // L
    tok = pl.BlockSpec((h4, None, L, dh), lambda i, c: (0, i, c, 0))
    if vt.ndim == 4:
        vt_spec = pl.BlockSpec((h4, None, dh, L), lambda i, c: (0, i, 0, c))
        gr_spec = pl.BlockSpec((None, 2 * h4, L), lambda i, c: (i, 0, c))
    else:
        vt_spec = pl.BlockSpec((h4, dh, L), lambda i, c: (0, 0, i * nc + c))
        gr_spec = pl.BlockSpec((2 * h4, L), lambda i, c: (0, i * nc + c))
    st = lambda r, w: pl.BlockSpec((None, h4, r, w), lambda i, c: (i, 0, 0, 0))
    return pl.pallas_call(
        functools.partial(_mlstm_body, L=L, t_valid=t_valid), grid=(b, nc),
        in_specs=[tok, tok, tok, vt_spec, pl.BlockSpec((None, L, 2 * h4), lambda i, c: (i, c, 0)), gr_spec,
                  st(dh, dh), st(1, dh), st(1, 1)],
        out_specs=[tok, st(dh, dh), st(1, dh), st(1, 1)],
        out_shape=[jax.ShapeDtypeStruct((h4, b, t, dh), F32), jax.ShapeDtypeStruct((b, h4, dh, dh), F32),
                   jax.ShapeDtypeStruct((b, h4, 1, dh), F32), jax.ShapeDtypeStruct((b, h4, 1, 1), F32)],
        compiler_params=_params(("parallel", "arbitrary")), name="mlstm")(
            q, k, v, vt, gc, gr, c0, n0.reshape(b, h4, 1, dh), m0.reshape(b, h4, 1, 1))


def _even_out_body(x_ref, yp_ref, og_ref, hm_ref, w_ref, o_ref):
    gated = [(jax.nn.sigmoid(og_ref[h]) * hm_ref[h]).astype(BF16) for h in range(ML_HEADS)]
    mixed = jnp.concatenate([yp_ref[...].astype(BF16)] + gated, axis=1)
    o_ref[...] = x_ref[...] + _dot(mixed, w_ref[...])


def _even_out(x, ypool, og, hm, w_out):
    n, d = x.shape
    tm = 512 if n % 512 == 0 else n
    w = w_out.astype(BF16)
    row = lambda w: pl.BlockSpec((tm, w), lambda i: (i, 0))
    hrow = pl.BlockSpec((ML_HEADS, tm, ML_DH), lambda i: (0, i, 0))
    return pl.pallas_call(
        _even_out_body, grid=(n // tm,),
        in_specs=[row(d), row(POOL_WIDTH), hrow, hrow, _const_spec(w.shape)],
        out_specs=row(d), out_shape=jax.ShapeDtypeStruct((n, d), F32),
        compiler_params=_params(("parallel",)), name="even_out")(x, ypool, og, hm, w)


def _odd_in_body(x_ref, g_ref, wq_ref, wkvt_ref, wgt_ref, wc_ref,
                 q_ref, rows_ref, win_ref, gates_ref, ucv_ref, bg_ref):
    hn = _rms(x_ref[...], g_ref[...]).astype(BF16)
    q_ref[...] = _dot(hn, wq_ref[...]) * (LOG2E * HEAD_DIM ** -0.5)
    nrow = rows_ref.shape[0]
    rows_ref[...] = _dot_nt(wkvt_ref[:nrow], hn)
    win_ref[...] = _dot_nt(wkvt_ref[nrow:], hn)
    gates_ref[...] = jax.nn.sigmoid(_dot(hn, wgt_ref[...]))
    bg_ref[...] = _dot(hn, wc_ref[:, :CONV_CH])
    ucv_ref[...] = _dot(hn, wc_ref[:, CONV_CH:2 * CONV_CH]) * _dot(hn, wc_ref[:, 2 * CONV_CH:])


def _odd_in(x, g, w_in, seq):
    n, d = x.shape
    tm = 512 if seq % 512 == 0 else seq
    nt = seq // tm
    kvw = 6 * KV_HEADS * HEAD_DIM
    ngt = 3 * NSA_HEADS
    wq = w_in[:, :NSA_WIDTH].astype(BF16)
    wkvt = w_in[:, NSA_WIDTH:NSA_WIDTH + kvw].T.astype(BF16)
    wgt = w_in[:, NSA_WIDTH + kvw:NSA_WIDTH + kvw + ngt].astype(BF16)
    wc = w_in[:, NSA_WIDTH + kvw + ngt:].astype(BF16)
    nrow = 4 * KV_HEADS * HEAD_DIM
    row = lambda w: pl.BlockSpec((tm, w), lambda i: (i, 0))
    slab = lambda r: pl.BlockSpec((None, r, tm), lambda i: (i // nt, 0, i % nt))
    sh = lambda w: jax.ShapeDtypeStruct((n, w), F32)
    return pl.pallas_call(
        _odd_in_body, grid=(n // tm,),
        in_specs=[row(d), _const_spec((1, d)), _const_spec(wq.shape), _const_spec(wkvt.shape),
                  _const_spec(wgt.shape), _const_spec(wc.shape)],
        out_specs=[row(NSA_WIDTH),
                   slab(nrow), slab(kvw - nrow), row(ngt), row(CONV_CH), row(CONV_CH)],
        out_shape=[sh(NSA_WIDTH),
                   jax.ShapeDtypeStruct((n // seq, nrow, seq), F32),
                   jax.ShapeDtypeStruct((n // seq, kvw - nrow, seq), F32),
                   sh(ngt), sh(CONV_CH), sh(CONV_CH)],
        compiler_params=_params(("parallel",)), name="odd_in")(x, g.reshape(1, d), wq, wkvt, wgt, wc)


def _conv_body(u_ref, bg_ref, pre_ref, w_ref, y_ref, carry, full, *, tb):
    @pl.when(pl.program_id(1) == 0)
    def _():
        carry[...] = pre_ref[...]

    full[0:CONV_HALO] = carry[...]
    full[CONV_HALO:] = u_ref[...]
    f = full[...]
    w = w_ref[...]
    conv = f * w[2:3] + pltpu.roll(f, 1, 0) * w[1:2] + pltpu.roll(f, 2, 0) * w[0:1]
    y_ref[...] = bg_ref[...] * conv[CONV_HALO:]
    carry[...] = full[tb:tb + CONV_HALO]


def _conv(u, bg, prefix, conv_w):
    b, t, c = u.shape
    tb = 512 if t % 512 == 0 else t
    pre = jnp.pad(prefix, ((0, 0), (CONV_HALO - prefix.shape[1], 0), (0, 0)))
    blk = pl.BlockSpec((None, tb, c), lambda i, j: (i, j, 0))
    return pl.pallas_call(
        functools.partial(_conv_body, tb=tb), grid=(b, t // tb),
        in_specs=[blk, blk, pl.BlockSpec((None, CONV_HALO, c), lambda i, j: (i, 0, 0)), _const_spec((CONV_K, c))],
        out_specs=blk, out_shape=jax.ShapeDtypeStruct((b, t, c), F32),
        scratch_shapes=[pltpu.VMEM((CONV_HALO, c), F32), pltpu.VMEM((tb + CONV_HALO, c), F32)],
        compiler_params=_params(("parallel", "arbitrary")), name="conv")(u, bg, pre, conv_w)


def _odd_out_body(x_ref, o_ref, yc_ref, w0_ref, w1_ref, out_ref):
    out_ref[...] = (x_ref[...] + _dot(o_ref[...].astype(BF16), w0_ref[...])
                    + _dot(yc_ref[...].astype(BF16), w1_ref[...]))


def _odd_out(x, o, yconv, w_out):
    n, d = x.shape
    tm = 512 if n % 512 == 0 else n
    w0 = w_out[:NSA_WIDTH].astype(BF16)
    w1 = w_out[NSA_WIDTH:].astype(BF16)
    row = lambda w: pl.BlockSpec((tm, w), lambda i: (i, 0))
    return pl.pallas_call(
        _odd_out_body, grid=(n // tm,),
        in_specs=[row(d), row(NSA_WIDTH), row(CONV_CH), _const_spec(w0.shape), _const_spec(w1.shape)],
        out_specs=row(d), out_shape=jax.ShapeDtypeStruct((n, d), F32),
        compiler_params=_params(("parallel",)), name="odd_out")(x, o, yconv, w0, w1)


CMP_TILE = 2048
CMP_ROWS = 4 * HEAD_DIM


def _cmp_weights(cmp_pos_w, cmp_w, cmp_b, tile):
    m = tile // CMP_STRIDE
    cols = -(-(m + 1) // 128) * 128
    pos = jnp.arange(tile)
    chunk = (pos // CMP_STRIDE)[:, None]
    col = jnp.arange(cols)[None, :]
    mats = []
    for kv in range(2):
        wa = cmp_pos_w[kv, :CMP_STRIDE][pos % CMP_STRIDE][:, None]
        wb = cmp_pos_w[kv, CMP_STRIDE:][pos % CMP_STRIDE][:, None]
        mats.append(jnp.where(chunk == col, wb, 0.0) + jnp.where(chunk == col - 1, wa, 0.0))
    w2 = jnp.stack(mats).astype(BF16)
    wbdt = jax.scipy.linalg.block_diag(cmp_w[0], cmp_w[0], cmp_w[1], cmp_w[1]).T.astype(BF16)
    biast = jnp.concatenate([cmp_b[0], cmp_b[0], cmp_b[1], cmp_b[1]]).reshape(-1, 1)
    return w2, wbdt, biast


def _compress_body(*refs, n_pages, n_prefetch):
    refs = refs[n_prefetch:]
    pages = refs[:n_pages]
    w2_ref, wbdt_ref, biast_ref, out_ref, carry = refs[n_pages:]
    m_out = out_ref.shape[1]

    @pl.when(pl.program_id(1) == 0)
    def _():
        carry[...] = jnp.zeros(carry.shape, F32)

    x = jnp.concatenate([pg[...] for pg in pages], axis=1).astype(BF16)
    half = CMP_ROWS // 2
    res = jnp.concatenate([_dot(x[:half], w2_ref[0]), _dot(x[half:], w2_ref[1])], axis=0)
    first = lax.broadcasted_iota(jnp.int32, (CMP_ROWS, m_out), 1) == 0
    pre = res[:, :m_out] + jnp.where(first, carry[...], 0.0)
    carry[...] = jnp.broadcast_to(res[:, m_out:m_out + 1], carry.shape)
    out_ref[...] = _dot(wbdt_ref[...], pre.astype(BF16)) + biast_ref[...]


def _compress_prompt(rows_t, cmp_pos_w, cmp_w, cmp_b):
    b, _, t = rows_t.shape
    tile = min(CMP_TILE, t)
    assert t % tile == 0
    w2, wbdt, biast = _cmp_weights(cmp_pos_w, cmp_w, cmp_b, tile)
    m_out = tile // CMP_STRIDE
    return pl.pallas_call(
        functools.partial(_compress_body, n_pages=1, n_prefetch=0), grid=(b, t // tile),
        in_specs=[pl.BlockSpec((None, CMP_ROWS, tile), lambda i, j: (i, 0, j)),
                  _const_spec(w2.shape), _const_spec(wbdt.shape), _const_spec(biast.shape)],
        out_specs=pl.BlockSpec((None, CMP_ROWS, m_out), lambda i, j: (i, 0, j)),
        out_shape=jax.ShapeDtypeStruct((b, CMP_ROWS, t // CMP_STRIDE), F32),
        scratch_shapes=[pltpu.VMEM((CMP_ROWS, m_out), F32)],
        compiler_params=_params(("parallel", "arbitrary")), name="compress_prompt")(rows_t, w2, wbdt, biast)


def _compress_paged(cache_t, page_table, cmp_pos_w, cmp_w, cmp_b):
    b, npg = page_table.shape
    pg = min(CMP_TILE // PAGE_SIZE, npg)
    assert npg % pg == 0
    tile = pg * PAGE_SIZE
    w2, wbdt, biast = _cmp_weights(cmp_pos_w, cmp_w, cmp_b, tile)
    m_out = tile // CMP_STRIDE
    specs = [pl.BlockSpec((None, CMP_ROWS, PAGE_SIZE),
                          functools.partial(lambda i, j, pt, p: (pt[i, j * pg + p], 0, 0), p=p))
             for p in range(pg)]
    return pl.pallas_call(
        functools.partial(_compress_body, n_pages=pg, n_prefetch=1),
        grid_spec=pltpu.PrefetchScalarGridSpec(
            num_scalar_prefetch=1, grid=(b, npg // pg),
            in_specs=specs + [_const_spec(w2.shape), _const_spec(wbdt.shape), _const_spec(biast.shape)],
            out_specs=pl.BlockSpec((None, CMP_ROWS, m_out), lambda i, j, pt: (i, 0, j)),
            scratch_shapes=[pltpu.VMEM((CMP_ROWS, m_out), F32)]),
        out_shape=jax.ShapeDtypeStruct((b, CMP_ROWS, npg * PAGE_SIZE // CMP_STRIDE), F32),
        compiler_params=_params(("parallel", "arbitrary")), name="compress_paged")(
            page_table, *([cache_t] * pg), w2, wbdt, biast)


def _alibi_slopes(rows_per_head):
    sl = (LOG2E * 2.0 ** (-8.0 * np.arange(1, NSA_HEADS + 1) / NSA_HEADS)).astype(np.float32).reshape(KV_HEADS, GQ)
    return jnp.asarray(np.repeat(sl, rows_per_head, axis=1)[:, :, None])


def _score_matrix(n_cmp_rows, n_blocks, lanes):
    m = np.arange(n_cmp_rows)[:, None]
    j = np.arange(lanes)[None, :]
    return jnp.asarray(((m >= 4 * j) & (m <= 4 * j + 4) & (m >= 1) & (j < n_blocks)).astype(np.float32))


def _group_queries(q_ref, g):
    heads = [q_ref[:, (g * GQ + i) * HEAD_DIM:(g * GQ + i + 1) * HEAD_DIM] for i in range(GQ)]
    return jnp.concatenate(heads, axis=0).astype(BF16)


def _softmax_rows(s, mask):
    s = jnp.where(mask, s, -jnp.inf)
    m = jnp.max(s, axis=-1, keepdims=True)
    m = jnp.where(m > -jnp.inf, m, 0.0)
    e = jnp.exp2(s - m)
    return e * (1.0 / jnp.maximum(jnp.sum(e, axis=-1, keepdims=True), 1e-30))


def _select_blocks(score, cur, n_blocks):
    r, lanes = score.shape
    blk = lax.broadcasted_iota(jnp.int32, (r, lanes), 1)
    forced = (blk == 0) | (blk == cur) | (blk == cur - 1)
    val = jnp.where(forced, jnp.inf, jnp.where(blk <= cur, score, -jnp.inf))
    rank = jnp.zeros((r, lanes), jnp.int32)
    for i in range(n_blocks):
        ci = val[:, i:i + 1]
        ahead = (ci > val) | ((ci == val) & (blk > i))
        rank = rank + ahead.astype(jnp.int32)
    return ((rank < min(SEL_TOPN, n_blocks)) & (blk < n_blocks)).astype(F32)


POS_ROWS = 16


def _slope_pieces(rows_per_head):
    sl = _alibi_slopes(rows_per_head)
    hi = sl.astype(BF16)
    mid = (sl - hi.astype(F32)).astype(BF16)
    lo = (sl - hi.astype(F32) - mid.astype(F32)).astype(BF16)
    pad = jnp.zeros(sl.shape[:2] + (POS_ROWS - 6,), BF16)
    return jnp.concatenate([hi, mid, lo, hi, mid, lo, pad], axis=2)


def _pos_rows(t_len, stride=1, offset=0):
    k = stride * np.arange(t_len) + offset
    hi = (SEL_BLOCK * (k // SEL_BLOCK)).astype(np.float32)
    lo = (k % SEL_BLOCK).astype(np.float32)
    return np.stack([hi, hi, hi, lo, lo, lo] + [np.zeros(t_len, np.float32)] * (POS_ROWS - 6))


def _block_rows(n_rows, n_keys):
    return (np.arange(n_rows)[:, None] == (np.arange(n_keys) // SEL_BLOCK)[None, :]).astype(np.float32)


def _key_constants(t_len):
    both = np.concatenate([_pos_rows(t_len), _block_rows(t_len // SEL_BLOCK, t_len)], axis=0)
    return jnp.asarray(both).astype(BF16)


def _select_bias_t(score_t, cur):
    n_blocks, r = score_t.shape
    blk = lax.broadcasted_iota(jnp.int32, (n_blocks, r), 0)
    forced = (blk == 0) | (blk == cur) | (blk == cur - 1)
    val = jnp.where(forced, jnp.inf, jnp.where(blk <= cur, score_t, -jnp.inf))

    rank = jnp.zeros((n_blocks, r), F32)
    for i in range(n_blocks):
        ci = val[i:i + 1, :]
        ahead = (ci > val) | ((ci == val) & (blk > i))
        rank = rank + jnp.where(ahead, 1.0, 0.0)
    return jnp.where((rank < min(SEL_TOPN, n_blocks)) & (blk <= cur), 0.0, NEG)


def _nsa_prompt_body(q_ref, gates_ref, kvc_ref, sel_ref, win_ref, saug_ref, kconst_ref, cconst_ref,
                     smat_ref, o_ref, need_sc, *, t_len, kc, wb):
    bi = pl.program_id(1)
    p0 = bi * Q_BLOCK
    nrow = GQ * Q_BLOCK
    n_cmp = kvc_ref.shape[1]
    n_blocks = t_len // SEL_BLOCK
    qpos = lax.broadcasted_iota(jnp.int32, (Q_BLOCK, 1), 0) + p0
    tpos = jnp.concatenate([qpos] * GQ, axis=0)
    gates = gates_ref[...]

    qgs = [_group_queries(q_ref, g) for g in range(KV_HEADS)]
    q_pos = [jnp.concatenate([qgs[g], saug_ref[g]], axis=1) for g in range(KV_HEADS)]

    cpos = CMP_STRIDE * lax.broadcasted_iota(jnp.int32, (1, n_cmp), 1) + (CMP_STRIDE - 1)
    dist = tpos - cpos
    cmask = (dist >= 0) & (cpos >= 2 * CMP_STRIDE - 1)

    def compressed_scores(g):
        ks = g * HEAD_DIM
        katc = jnp.concatenate([kvc_ref[ks:ks + HEAD_DIM, :].astype(BF16), cconst_ref[...]], axis=0)
        return _dot(q_pos[g], katc)

    wstart = pl.multiple_of(jnp.clip((p0 - WINDOW) // 128 * 128, 0, t_len - wb), 128)
    wpos = wstart + lax.broadcasted_iota(jnp.int32, (1, wb), 1)
    wmask = (wpos <= tpos) & (wpos > tpos - WINDOW)

    def window_scores(g):
        katw = jnp.concatenate([win_ref[g * HEAD_DIM:(g + 1) * HEAD_DIM, pl.ds(wstart, wb)].astype(BF16),
                                kconst_ref[0:POS_ROWS, pl.ds(wstart, wb)]], axis=0)
        return _dot(q_pos[g], katw)

    def window_output(g, e_win):
        vs = (KV_HEADS + g) * HEAD_DIM
        vwt = jnp.concatenate([win_ref[vs:vs + HEAD_DIM, pl.ds(wstart, wb)].astype(BF16),
                               jnp.ones((POS_ROWS, wb), BF16)], axis=0)
        acc_w = _dot_nt(e_win, vwt)
        return acc_w[:, :HEAD_DIM] / acc_w[:, HEAD_DIM:HEAD_DIM + 1]

    s_cmp = [compressed_scores(g) for g in range(KV_HEADS)]
    s_win = [window_scores(g) for g in range(KV_HEADS)]
    p_cmps = [_softmax_rows(s_cmp[g], cmask) for g in range(KV_HEADS)]
    e_wins = []
    for g in range(KV_HEADS):
        sw = jnp.where(wmask, s_win[g], -jnp.inf)
        e_wins.append(jnp.exp2(sw - jnp.max(sw, axis=1, keepdims=True)).astype(BF16))
    o_cmps = [_dot_nt(p_cmps[g].astype(BF16), kvc_ref[(KV_HEADS + g) * HEAD_DIM:(KV_HEADS + g + 1) * HEAD_DIM, :]
                      .astype(BF16)) for g in range(KV_HEADS)]
    imps = [jnp.sum(p_cmps[g].reshape(GQ, Q_BLOCK, n_cmp), axis=0) for g in range(KV_HEADS)]
    o_wins = [window_output(g, e_wins[g]) for g in range(KV_HEADS)]
    score_t = _dot_nt(smat_ref[...], _split3(jnp.concatenate(imps, axis=0), 1))
    bias_f = _select_bias_t(score_t, bi)
    bias_qb = bias_f.T
    biases = [bias_qb[g * Q_BLOCK:(g + 1) * Q_BLOCK] for g in range(KV_HEADS)]
    q_aug = [jnp.concatenate([q_pos[g], jnp.concatenate([biases[g]] * GQ, axis=0).astype(BF16)], axis=1)
             for g in range(KV_HEADS)]
    for hh in range(NSA_HEADS):
        g, r = hh // GQ, slice((hh % GQ) * Q_BLOCK, (hh % GQ + 1) * Q_BLOCK)
        o_ref[:, hh * HEAD_DIM:(hh + 1) * HEAD_DIM] = (
            gates[:, 3 * hh:3 * hh + 1] * o_cmps[g][r] + gates[:, 3 * hh + 2:3 * hh + 3] * o_wins[g][r])

    ones_rows = jnp.ones((POS_ROWS, kc), BF16)

    def chunk_scores(g, c):
        k0 = pl.multiple_of(c * kc, kc)
        kat = jnp.concatenate([sel_ref[g * HEAD_DIM:(g + 1) * HEAD_DIM, pl.ds(k0, kc)].astype(BF16),
                               kconst_ref[:, pl.ds(k0, kc)]], axis=0)
        return _dot(q_aug[g], kat), k0

    def probabilities(s, m_i):
        m_new = jnp.maximum(m_i, jnp.max(s, axis=1, keepdims=True))
        return m_new, jnp.exp2(s - m_new).astype(BF16), jnp.exp2(m_i - m_new)

    def weighted_values(g, pr, k0):
        vs = (KV_HEADS + g) * HEAD_DIM
        vt = jnp.concatenate([sel_ref[vs:vs + HEAD_DIM, pl.ds(k0, kc)].astype(BF16), ones_rows], axis=0)
        return _dot_nt(pr, vt)

    def chunk_update(c, carry, mask=None):
        scores = [chunk_scores(g, c) for g in range(KV_HEADS)]
        if mask is not None:
            scores = [(jnp.where(mask(k0), s, NEG), k0) for s, k0 in scores]
        probs = [probabilities(scores[g][0], carry[g][0]) for g in range(KV_HEADS)]
        out = []
        for g in range(KV_HEADS):
            m_new, pr, alpha = probs[g]
            out.append((m_new, alpha * carry[g][1] + weighted_values(g, pr, scores[g][1])))
        return tuple(out)

    last = (p0 + Q_BLOCK - 1) // kc
    blocks_per_chunk = kc // SEL_BLOCK
    n_need = jnp.int32(0)
    for c in range(t_len // kc - 1):
        picked = bias_f[c * blocks_per_chunk:(c + 1) * blocks_per_chunk, :] == 0.0
        need = jnp.where(c < last, jnp.max(jnp.where(picked, 1, 0)), 0)
        need_sc[n_need] = c
        n_need = n_need + need

    one = (jnp.full((nrow, 1), NEG, F32), jnp.zeros((nrow, HEAD_DIM + POS_ROWS), F32))
    carry = lax.fori_loop(0, n_need, lambda i, carry: chunk_update(need_sc[i], carry), (one,) * KV_HEADS)
    causal = lambda k0: k0 + lax.broadcasted_iota(jnp.int32, (1, kc), 1) <= tpos
    carry = chunk_update(last, carry, mask=causal)
    for g in range(KV_HEADS):
        acc_s = carry[g][1]
        o_sel = acc_s[:, :HEAD_DIM] / jnp.maximum(acc_s[:, HEAD_DIM:HEAD_DIM + 1], 1e-30)
        for i in range(GQ):
            hh = g * GQ + i
            r = slice(i * Q_BLOCK, (i + 1) * Q_BLOCK)
            o_ref[:, hh * HEAD_DIM:(hh + 1) * HEAD_DIM] += gates[:, 3 * hh + 1:3 * hh + 2] * o_sel[r]


def _nsa_prompt(q, gates, kvc_t, rows_t, win_t):
    b, t, _ = q.shape
    assert t % 128 == 0
    kc = 512 if t % 512 == 0 else t
    wb = min(WINDOW + 128, t)
    n_cmp = kvc_t.shape[2]
    n_blocks = t // SEL_BLOCK
    ncol = 4 * HEAD_DIM
    saug = _slope_pieces(Q_BLOCK)
    kconst = _key_constants(t)
    cconst = jnp.asarray(_pos_rows(n_cmp, CMP_STRIDE, CMP_STRIDE - 1)).astype(BF16)
    smat = jnp.tile(_score_matrix(n_cmp, n_blocks, n_blocks).T, (1, 3)).astype(BF16)
    return pl.pallas_call(
        functools.partial(_nsa_prompt_body, t_len=t, kc=kc, wb=wb), grid=(b, t // Q_BLOCK),
        in_specs=[pl.BlockSpec((None, Q_BLOCK, NSA_WIDTH), lambda i, j: (i, j, 0)),
                  pl.BlockSpec((None, Q_BLOCK, 3 * NSA_HEADS), lambda i, j: (i, j, 0)),
                  pl.BlockSpec((None, ncol, n_cmp), lambda i, j: (i, 0, 0)),
                  pl.BlockSpec((None, ncol, t), lambda i, j: (i, 1, 0)),
                  pl.BlockSpec((None, ncol, t), lambda i, j: (i, 0, 0)),
                  _const_spec(saug.shape), _const_spec(kconst.shape), _const_spec(cconst.shape),
                  _const_spec(smat.shape)],
        out_specs=pl.BlockSpec((None, Q_BLOCK, NSA_WIDTH), lambda i, j: (i, j, 0)),
        out_shape=jax.ShapeDtypeStruct((b, t, NSA_WIDTH), F32),
        scratch_shapes=[pltpu.SMEM((t // kc,), jnp.int32)],
        compiler_params=_params(("parallel", "arbitrary")), name="nsa_prompt")(
            q, gates, kvc_t, rows_t, win_t, saug, kconst, cconst, smat)


def _nsa_dec_select_body(q_ref, kvc_ref, slope_ref, smat_ref, ocmp_ref, sel_ref, *, past_len, tq, n_blocks,
                         step_blocks):
    n_cmp = kvc_ref.shape[1]
    n_steps = sel_ref.shape[1] - 1
    lanes_out = sel_ref.shape[3]
    nrow = GQ * tq
    qpos = lax.broadcasted_iota(jnp.int32, (tq, 1), 0) + past_len
    tpos = jnp.concatenate([qpos] * GQ, axis=0)
    cpos = CMP_STRIDE * lax.broadcasted_iota(jnp.int32, (1, n_cmp), 1) + (CMP_STRIDE - 1)
    dist = tpos - cpos
    for g in range(KV_HEADS):
        qg = _group_queries(q_ref, g)
        ks, vs = g * HEAD_DIM, (KV_HEADS + g) * HEAD_DIM
        s = _dot(qg, kvc_ref[ks:ks + HEAD_DIM, :].astype(BF16)) - slope_ref[g] * dist.astype(F32)
        p_cmp = _softmax_rows(s, (dist >= 0) & (cpos >= 2 * CMP_STRIDE - 1))
        ocmp_ref[g] = _dot_nt(p_cmp.astype(BF16), kvc_ref[vs:vs + HEAD_DIM, :].astype(BF16))
        imp = jnp.sum(p_cmp.reshape(GQ, tq, n_cmp), axis=0)
        score = _dot(_split3(imp, 1), smat_ref[...])
        bias = jnp.where(_select_blocks(score, qpos // SEL_BLOCK, n_blocks) > 0.5, 0.0, NEG)
        pad = jnp.zeros((tq, lanes_out - step_blocks), F32)
        for st in range(n_steps):
            sel_ref[g, st] = jnp.concatenate([bias[:, st * step_blocks:(st + 1) * step_blocks], pad], axis=1)
        last_blk = n_steps * step_blocks
        sel_ref[g, n_steps] = jnp.concatenate([bias[:, last_blk:last_blk + 1],
                                               jnp.zeros((tq, lanes_out - 1), F32)], axis=1)


def _nsa_dec_select(q, kvc, past_len, n_blocks, step_blocks):
    b, tq, _ = q.shape
    n_cmp = kvc.shape[2]
    lanes = -(-n_blocks // 128) * 128
    n_steps = (n_blocks - 1) // step_blocks
    assert n_steps * step_blocks == n_blocks - 1 and step_blocks <= 128
    slopes = _alibi_slopes(tq)
    smat = jnp.tile(_score_matrix(n_cmp, n_blocks, lanes), (3, 1)).astype(BF16)
    return pl.pallas_call(
        functools.partial(_nsa_dec_select_body, past_len=past_len, tq=tq, n_blocks=n_blocks,
                          step_blocks=step_blocks), grid=(b,),
        in_specs=[pl.BlockSpec((None, tq, NSA_WIDTH), lambda i: (i, 0, 0)),
                  pl.BlockSpec((None, 4 * HEAD_DIM, n_cmp), lambda i: (i, 0, 0)),
                  _const_spec(slopes.shape), _const_spec(smat.shape)],
        out_specs=[pl.BlockSpec((None, KV_HEADS, GQ * tq, HEAD_DIM), lambda i: (i, 0, 0, 0)),
                   pl.BlockSpec((None, KV_HEADS, n_steps + 1, tq, 128), lambda i: (i, 0, 0, 0, 0))],
        out_shape=[jax.ShapeDtypeStruct((b, KV_HEADS, GQ * tq, HEAD_DIM), F32),
                   jax.ShapeDtypeStruct((b, KV_HEADS, n_steps + 1, tq, 128), F32)],
        compiler_params=_params(("parallel",)), name="nsa_dec_select")(q, kvc, slopes, smat)


def _nsa_dec_attend_body(pt_ref, q_ref, selb_ref, ocmp_ref, gates_ref, newsel_ref, winbuf_ref, newwin_ref,
                         slope_ref, saug_ref, pos_ref, blk_ref, *rest, past_len, tq, t_valid, n_pages):
    pages = rest[:n_pages]
    o_ref, m_sc, acc_sc = rest[n_pages:]
    step = pl.program_id(1)
    nrow = GQ * tq
    qpos = lax.broadcasted_iota(jnp.int32, (tq, 1), 0) + past_len
    tpos = jnp.concatenate([qpos] * GQ, axis=0)

    @pl.when(step == 0)
    def _():
        m_sc[...] = jnp.full(m_sc.shape, NEG, F32)
        acc_sc[...] = jnp.zeros(acc_sc.shape, F32)

    def update(g, sc, weighted_values):
        m_i = m_sc[g]
        m_new = jnp.maximum(m_i, jnp.max(sc, axis=1, keepdims=True))
        pr = jnp.exp2(sc - m_new)
        acc_sc[g] = jnp.exp2(m_i - m_new) * acc_sc[g] + weighted_values(pr.astype(BF16))
        m_sc[g] = m_new

    nk = n_pages * PAGE_SIZE
    ones_rows = jnp.ones((POS_ROWS, nk), BF16)
    scores = []
    for g in range(KV_HEADS):
        qg = _group_queries(q_ref, g)
        ks = g * HEAD_DIM
        bias = jnp.concatenate([selb_ref[g, step]] * GQ, axis=0).astype(BF16)
        q_aug = jnp.concatenate([qg, saug_ref[g], bias], axis=1)
        kat = jnp.concatenate([jnp.concatenate([pg[ks:ks + HEAD_DIM, :] for pg in pages], axis=1).astype(BF16),
                               pos_ref[...], blk_ref[...]], axis=0)
        scores.append(_dot(q_aug, kat))
    for g in range(KV_HEADS):
        vs = (KV_HEADS + g) * HEAD_DIM
        vt = jnp.concatenate([jnp.concatenate([pg[vs:vs + HEAD_DIM, :] for pg in pages], axis=1).astype(BF16),
                              ones_rows], axis=0)
        update(g, scores[g], lambda p, vt=vt: _dot_nt(p, vt))

    @pl.when(step == pl.num_programs(1) - 1)
    def _():
        gates = gates_ref[...]
        tn = newsel_ref.shape[0]
        npos = past_len + lax.broadcasted_iota(jnp.int32, (1, tn), 1)
        nd = tpos - npos
        valid_new = (nd >= 0) & (npos < past_len + t_valid)
        wlen = winbuf_ref.shape[1]
        wpos = past_len - wlen + lax.broadcasted_iota(jnp.int32, (1, wlen), 1)
        wd = tpos - wpos
        for g in range(KV_HEADS):
            qg = _group_queries(q_ref, g)
            ks, vs = g * HEAD_DIM, (KV_HEADS + g) * HEAD_DIM
            slope = slope_ref[g]
            block_bias = jnp.concatenate([selb_ref[g, selb_ref.shape[1] - 1][:, 0:1]] * GQ, axis=0)
            sc = (_dot_nt(qg, newsel_ref[:, ks:ks + HEAD_DIM].astype(BF16)) + slope * npos.astype(F32)
                  + block_bias)
            new_v = jnp.concatenate([newsel_ref[:, vs:vs + HEAD_DIM].astype(BF16),
                                     jnp.ones((tn, POS_ROWS), BF16)], axis=1)
            update(g, jnp.where(valid_new, sc, NEG), lambda p, new_v=new_v: _dot(p, new_v))
            acc = acc_sc[g]
            o_sel = acc[:, :HEAD_DIM] / jnp.maximum(acc[:, HEAD_DIM:HEAD_DIM + 1], 1e-30)
            s1 = _dot(qg, winbuf_ref[ks:ks + HEAD_DIM, :].astype(BF16)) - slope * wd.astype(F32)
            s2 = _dot_nt(qg, newwin_ref[:, ks:ks + HEAD_DIM].astype(BF16)) - slope * nd.astype(F32)
            ok1 = (wd >= 0) & (wd < WINDOW) & (wpos >= 0)
            ok2 = valid_new & (nd < WINDOW)
            s1 = jnp.where(ok1, s1, -jnp.inf)
            s2 = jnp.where(ok2, s2, -jnp.inf)
            mx = jnp.maximum(jnp.max(s1, axis=1, keepdims=True), jnp.max(s2, axis=1, keepdims=True))
            mx = jnp.where(mx > -jnp.inf, mx, 0.0)
            e1 = jnp.exp2(s1 - mx)
            e2 = jnp.exp2(s2 - mx)
            den = jnp.maximum(jnp.sum(e1, axis=1, keepdims=True) + jnp.sum(e2, axis=1, keepdims=True), 1e-30)
            o_win = (_dot_nt(e1.astype(BF16), winbuf_ref[vs:vs + HEAD_DIM, :].astype(BF16))
                     + _dot(e2.astype(BF16), newwin_ref[:, vs:vs + HEAD_DIM].astype(BF16))) / den
            o_cmp = ocmp_ref[g]
            for i in range(GQ):
                hh = g * GQ + i
                r = slice(i * tq, (i + 1) * tq)
                o_ref[:, hh * HEAD_DIM:(hh + 1) * HEAD_DIM] = (
                    gates[:, 3 * hh:3 * hh + 1] * o_cmp[r] + gates[:, 3 * hh + 1:3 * hh + 2] * o_sel[r]
                    + gates[:, 3 * hh + 2:3 * hh + 3] * o_win[r])


DEC_PAGES = 16


def _nsa_dec_attend(q, selbias, o_cmp, gates, newsel, winbuf, newwin, cache, page_table, past_len, t_valid):
    b, tq, _ = q.shape
    npg = page_table.shape[1]
    pg = min(DEC_PAGES, npg)
    assert npg % pg == 0 and selbias.shape[2] == npg // pg + 1
    nk = pg * PAGE_SIZE
    ncol = 4 * HEAD_DIM
    slopes = _alibi_slopes(tq)
    saug = _slope_pieces(tq)
    pos_rows = jnp.asarray(_pos_rows(past_len)).astype(BF16)
    blk_rows = jnp.asarray(_block_rows(selbias.shape[-1], nk)).astype(BF16)
    wlen = winbuf.shape[2]
    bspec = lambda shape: pl.BlockSpec((None,) + shape, lambda i, j, pt: (i,) + (0,) * len(shape))
    page_specs = [pl.BlockSpec((None, ncol, PAGE_SIZE),
                               functools.partial(lambda i, j, pt, p: (pt[i, j * pg + p], 1, 0), p=p))
                  for p in range(pg)]
    return pl.pallas_call(
        functools.partial(_nsa_dec_attend_body, past_len=past_len, tq=tq, t_valid=t_valid, n_pages=pg),
        grid_spec=pltpu.PrefetchScalarGridSpec(
            num_scalar_prefetch=1, grid=(b, npg // pg),
            in_specs=[pl.BlockSpec((None, tq, NSA_WIDTH), lambda i, j, pt: (i, 0, 0)),
                      bspec(selbias.shape[1:]), bspec((KV_HEADS, GQ * tq, HEAD_DIM)),
                      bspec((tq, 3 * NSA_HEADS)), bspec((tq, ncol)), bspec((ncol, wlen)), bspec((tq, ncol)),
                      _const_spec(slopes.shape), _const_spec(saug.shape),
                      pl.BlockSpec((POS_ROWS, nk), lambda i, j, pt: (0, j)), _const_spec(blk_rows.shape)]
            + page_specs,
            out_specs=bspec((tq, NSA_WIDTH)),
            scratch_shapes=[pltpu.VMEM((KV_HEADS, GQ * tq, 1), F32),
                            pltpu.VMEM((KV_HEADS, GQ * tq, HEAD_DIM + POS_ROWS), F32)]),
        out_shape=jax.ShapeDtypeStruct((b, tq, NSA_WIDTH), F32),
        compiler_params=_params(("parallel", "arbitrary")), name="nsa_dec_attend")(
            page_table, q, selbias, o_cmp, gates, newsel, winbuf, newwin, slopes, saug, pos_rows, blk_rows,
            *([cache] * pg))


def _pad_t(a, axis, to):
    pad = [(0, 0)] * a.ndim
    pad[axis] = (0, to - a.shape[axis])
    return jnp.pad(a, pad)


def _even_layer(x, b, t, pos0, pool_buf, c0, n0, m0, g, w_in, b_gate, w_pool, pool_scale, w_out):
    n = b * t
    u, q, k, v, og, vt, gc, gr = _even_in(x, g, w_in, b_gate)
    u3 = u.reshape(b, t, POOL_WIDTH)
    new_pool = jnp.concatenate([pool_buf, u3], axis=1)[:, -pool_buf.shape[1]:]
    h4 = ML_HEADS
    if t % 256 == 0:
        tp, L = t, 256
        vt_in, gr_in = vt, gr
    else:
        tp = L = -(-t // 8) * 8
        vt_in = _pad_t(vt.reshape(h4, ML_DH, b, t), 3, tp).transpose(0, 2, 1, 3)
        gr_in = _pad_t(gr.reshape(2 * h4, b, t), 2, tp).transpose(1, 0, 2)
    seq = lambda a: _pad_t(a.reshape(h4, b, t, ML_DH), 2, tp)
    ypool = _pool(_pad_t(u3, 1, tp), pool_buf, w_pool, pool_scale, pos0)[:, :t].reshape(n, POOL_WIDTH)
    hm, c, nn, m = _mlstm(seq(q), seq(k), seq(v), vt_in, _pad_t(gc.reshape(b, t, 2 * h4), 1, tp), gr_in,
                          c0, n0, m0, L=L, t_valid=t)
    hm = hm[:, :, :t].reshape(h4, n, ML_DH)
    x = _even_out(x, ypool, og, hm, w_out)
    return x, new_pool, c, nn.reshape(b, h4, ML_DH), m.reshape(b, h4)


def _odd_layer(x, b, t, pos0, cache, page_table, win_buf, conv_buf, g, w_in, cmp_pos_w, cmp_w, cmp_b,
               conv_w, w_out):
    n = b * t
    ncol = 4 * HEAD_DIM
    prompt = cache is None
    q, rows_t, win_t, gates, ucv, bg = _odd_in(x, g, w_in, t if prompt else n)
    u3 = ucv.reshape(b, t, CONV_CH)
    new_conv = jnp.concatenate([conv_buf, u3], axis=1)[:, -(CONV_K - 1):]
    if prompt:
        kvc_t = _compress_prompt(rows_t, cmp_pos_w, cmp_w, cmp_b)
        o = _nsa_prompt(q.reshape(b, t, NSA_WIDTH), gates.reshape(b, t, 3 * NSA_HEADS), kvc_t, rows_t, win_t)
        new_win_t = win_t[:, :, t - min(WINDOW, t):]
        new_rows = rows_t.reshape(b, 4, KV_HEADS, HEAD_DIM, t).transpose(0, 4, 1, 2, 3)
        tp = t
    else:
        assert pos0 % SEL_BLOCK + t <= SEL_BLOCK
        tp = -(-t // 8) * 8
        rows_bt = rows_t[0].T.reshape(b, t, 2 * ncol)
        win_bt = win_t[0].T.reshape(b, t, ncol)
        n_pool = cache.shape[0]
        cache_t = cache.transpose(0, 2, 3, 4, 1).reshape(n_pool, 2 * ncol, PAGE_SIZE)
        kvc_t = _compress_paged(cache_t, page_table, cmp_pos_w, cmp_w, cmp_b)
        qp = _pad_t(q.reshape(b, t, NSA_WIDTH), 1, tp)
        n_blocks = -(-(pos0 + t) // SEL_BLOCK)
        step_blocks = min(DEC_PAGES, page_table.shape[1]) * PAGE_SIZE // SEL_BLOCK
        o_cmp, selmask = _nsa_dec_select(qp, kvc_t, pos0, n_blocks, step_blocks)
        wlen = win_buf.shape[1]
        winb_t = win_buf.transpose(0, 2, 3, 4, 1).reshape(b, ncol, wlen)
        o = _nsa_dec_attend(qp, selmask, o_cmp, _pad_t(gates.reshape(b, t, -1), 1, tp),
                            _pad_t(rows_bt[:, :, ncol:], 1, tp), winb_t, _pad_t(win_bt, 1, tp),
                            cache_t, page_table, pos0, t)[:, :t]
        new_win_t = jnp.concatenate([winb_t, win_bt.transpose(0, 2, 1)], axis=2)[:, :, -wlen:]
        new_rows = rows_bt.reshape(b, t, 4, KV_HEADS, HEAD_DIM)
    yconv = _conv(_pad_t(u3, 1, tp), _pad_t(bg.reshape(b, t, CONV_CH), 1, tp), conv_buf, conv_w)[:, :t]
    x = _odd_out(x, o.reshape(n, NSA_WIDTH), yconv.reshape(n, CONV_CH), w_out)
    new_win = new_win_t.reshape(b, 2, KV_HEADS, HEAD_DIM, new_win_t.shape[2]).transpose(0, 4, 1, 2, 3)
    return x, new_rows, new_win, new_conv


def _trunk(x3, pos0, pool_buf, ml_c, ml_n, ml_m, kv_cache, page_table, win_buf, conv_buf, p):
    b, t, d = x3.shape
    depth = p["norm_g"].shape[0]
    x = x3.reshape(b * t, d)
    pools, cs, ns, ms, rows, wins, convs = [], [], [], [], [], [], []
    for l in range(depth):
        j = l // 2
        x = _ffn(x, p["norm_g"][l, 0], p["w_ffn_in"][l][0], p["w_ffn_out"][l][0])
        if l % 2 == 0:
            x, pb, c, n, m = _even_layer(x, b, t, pos0, pool_buf[j], ml_c[j], ml_n[j], ml_m[j], p["norm_g"][l, 1],
                                         p["w_in_even"][j], p["b_gate_even"][j], p["w_pool"][j],
                                         p["pool_scale"][j], p["w_out_even"][j])
            pools.append(pb)
            cs.append(c)
            ns.append(n)
            ms.append(m)
        else:
            cache = None if kv_cache is None else kv_cache[j]
            wb = None if win_buf is None else win_buf[j]
            x, r, wn, cn = _odd_layer(x, b, t, pos0, cache, page_table, wb, conv_buf[j], p["norm_g"][l, 1],
                                      p["w_in_odd"][j], p["cmp_pos_w"][j], p["cmp_w"][j], p["cmp_b"][j],
                                      p["conv_w"][j], p["w_out_odd"][j])
            rows.append(r)
            wins.append(wn)
            convs.append(cn)
        x = _ffn(x, p["norm_g"][l, 2], p["w_ffn_in"][l][1], p["w_ffn_out"][l][1],
                 final_g=p["final_g"] if l == depth - 1 else None)
    states = (jnp.stack(pools), jnp.stack(cs), jnp.stack(ns), jnp.stack(ms),
              jnp.stack(rows), jnp.stack(wins), jnp.stack(convs))
    return x.reshape(b, t, d), states


def kernel(x_prompt, x_sample, state_pool, state_mlstm_c, state_mlstm_n, state_mlstm_m, cache_nsa_kv, state_win_kv, state_conv, page_table, norm_g, final_g, w_ffn_in, w_ffn_out, w_in_even, b_gate_even, w_pool, pool_scale, w_out_even, w_in_odd, cmp_pos_w, cmp_w, cmp_b, conv_w, w_out_odd):
    bp = x_prompt.shape[0]
    n_even, n_odd = state_pool.shape[0], state_conv.shape[0]
    past_len = page_table.shape[1] * PAGE_SIZE
    p = dict(norm_g=norm_g, final_g=final_g, w_ffn_in=w_ffn_in.astype(BF16), w_ffn_out=w_ffn_out.astype(BF16),
             w_in_even=w_in_even, b_gate_even=b_gate_even, w_pool=w_pool, pool_scale=pool_scale,
             w_out_even=w_out_even, w_in_odd=w_in_odd, cmp_pos_w=cmp_pos_w, cmp_w=cmp_w, cmp_b=cmp_b,
             conv_w=conv_w, w_out_odd=w_out_odd)
    pool0 = jnp.zeros((n_even, bp) + state_pool.shape[2:], F32)
    c0 = jnp.zeros((n_even, bp) + state_mlstm_c.shape[2:], F32)
    n0 = jnp.zeros((n_even, bp) + state_mlstm_n.shape[2:], F32)
    m0 = jnp.zeros((n_even, bp) + state_mlstm_m.shape[2:], F32)
    conv0 = jnp.zeros((n_odd, bp) + state_conv.shape[2:], F32)
    y_p, (pool_p, c_p, n_p, m_p, kv_p, win_p, conv_p) = _trunk(
        x_prompt, 0, pool0, c0, n0, m0, None, None, None, conv0, p)
    y_s, (pool_s, c_s, n_s, m_s, kv_s, win_s, conv_s) = _trunk(
        x_sample, past_len, state_pool, state_mlstm_c, state_mlstm_n, state_mlstm_m,
        cache_nsa_kv, page_table, state_win_kv, state_conv, p)
    return (y_p, y_s, pool_p, pool_s, c_p, c_s, n_p, n_s, m_p, m_s,
            kv_p, kv_s, win_p, win_s, conv_p, conv_s)
```

```python
import functools

import numpy as np
import jax
import jax.numpy as jnp
from jax import lax
from jax.experimental import pallas as pl
from jax.experimental.pallas import tpu as pltpu

F32 = jnp.float32
BF16 = jnp.bfloat16

EPS = 1e-6
POOL_WINDOWS = (2, 4, 8, 16)
POOL_GDIM = 64
POOL_WIDTH = 256
POOL_HALO = 16
ML_HEADS = 4
ML_DH = 192
ML_WIDTH = ML_HEADS * ML_DH
NSA_HEADS = 12
HEAD_DIM = 64
NSA_WIDTH = NSA_HEADS * HEAD_DIM
KV_HEADS = 2
GQ = NSA_HEADS // KV_HEADS
CMP_STRIDE = 16
SEL_BLOCK = 64
SEL_TOPN = 16
WINDOW = 512
Q_BLOCK = 64
PAGE_SIZE = 128
CONV_CH = 256
CONV_K = 3
CONV_HALO = 8

VMEM_LIMIT = 56 * 1024 * 1024
NEG = -1e30
LOG2E = 1.4426950408889634


def _params(sem, vmem=VMEM_LIMIT):
    return pltpu.CompilerParams(dimension_semantics=sem, vmem_limit_bytes=vmem)


def _const_spec(shape):
    nd = len(shape)
    return pl.BlockSpec(shape, lambda *_: (0,) * nd, pipeline_mode=pl.Buffered(1))


def _rms(x, g):
    return x * lax.rsqrt(jnp.mean(x * x, axis=-1, keepdims=True) + EPS) * g


def _dot(a, b):
    return jnp.dot(a, b, preferred_element_type=F32)


def _dot_nt(a, b):
    return lax.dot_general(a, b, (((1,), (1,)), ((), ())), preferred_element_type=F32)


def _split3(x, axis):
    hi = x.astype(BF16)
    r1 = x - hi.astype(F32)
    mid = r1.astype(BF16)
    lo = (r1 - mid.astype(F32)).astype(BF16)
    return jnp.concatenate([hi, mid, lo], axis=axis)


def _ffn_body(x_ref, g_ref, win_ref, wout_ref, *rest, d_ff, chunks, has_final):
    o_ref = rest[-1]
    x = x_ref[...]
    hn = _rms(x, g_ref[...]).astype(BF16)
    acc = jnp.zeros(x.shape, F32)
    off = 0
    for fc in chunks:
        a = _dot(hn, win_ref[:, off:off + fc])
        b = _dot(hn, win_ref[:, d_ff + off:d_ff + off + fc])
        act = (a * jax.nn.sigmoid(a) * b).astype(BF16)
        acc = acc + _dot(act, wout_ref[off:off + fc, :])
        off += fc
    y = x + 0.5 * acc
    if has_final:
        y = _rms(y, rest[0][...])
    o_ref[...] = y


def _ffn(x, g, w_in, w_out, final_g=None):
    n, d = x.shape
    d_ff = w_out.shape[0]
    tm = 512 if n % 512 == 0 else n
    chunks, left = [], d_ff
    while left:
        chunks.append(min(1024, left))
        left -= chunks[-1]
    row = pl.BlockSpec((tm, d), lambda i: (i, 0))
    in_specs = [row, _const_spec((1, d)), _const_spec(w_in.shape), _const_spec(w_out.shape)]
    args = [x, g.reshape(1, d), w_in, w_out]
    if final_g is not None:
        in_specs.append(_const_spec((1, d)))
        args.append(final_g.reshape(1, d))
    return pl.pallas_call(
        functools.partial(_ffn_body, d_ff=d_ff, chunks=tuple(chunks), has_final=final_g is not None),
        grid=(n // tm,), in_specs=in_specs, out_specs=row,
        out_shape=jax.ShapeDtypeStruct((n, d), F32),
        compiler_params=_params(("parallel",)), name="ffn")(*args)


def _even_in_body(x_ref, g_ref, wu_ref, wh_ref, wg_ref, wgt_ref, bg_ref, bgt_ref,
                  u_ref, q_ref, k_ref, v_ref, og_ref, vt_ref, gc_ref, gr_ref):
    hn = _rms(x_ref[...], g_ref[...]).astype(BF16)
    u_ref[...] = _dot(hn, wu_ref[...])
    for seg, out_ref in enumerate((q_ref, k_ref, v_ref, og_ref)):
        z = _dot(hn, wh_ref[seg])
        for h in range(ML_HEADS):
            out_ref[h] = z[:, h * ML_DH:(h + 1) * ML_DH]
            if out_ref is v_ref:
                vt_ref[h] = z[:, h * ML_DH:(h + 1) * ML_DH].T
    gc_ref[...] = _dot(hn, wg_ref[...]) + bg_ref[...]
    gr_ref[...] = _dot_nt(wgt_ref[...], hn) + bgt_ref[...]


def _even_in(x, g, w_in, b_gate):
    n, d = x.shape
    tm = 512 if n % 512 == 0 else n
    h4 = ML_HEADS
    wu = w_in[:, :POOL_WIDTH].astype(BF16)
    wh = w_in[:, POOL_WIDTH:POOL_WIDTH + 4 * ML_WIDTH].reshape(d, 4, ML_WIDTH).transpose(1, 0, 2).astype(BF16)
    wg = w_in[:, POOL_WIDTH + 4 * ML_WIDTH:].astype(BF16)
    row = lambda w: pl.BlockSpec((tm, w), lambda i: (i, 0))
    hrow = pl.BlockSpec((h4, tm, ML_DH), lambda i: (0, i, 0))
    hsh = jax.ShapeDtypeStruct((h4, n, ML_DH), F32)
    return pl.pallas_call(
        _even_in_body, grid=(n // tm,),
        in_specs=[row(d), _const_spec((1, d)), _const_spec(wu.shape), _const_spec(wh.shape),
                  _const_spec(wg.shape), _const_spec((2 * h4, d)),
                  _const_spec((1, 2 * h4)), _const_spec((2 * h4, 1))],
        out_specs=[row(POOL_WIDTH), hrow, hrow, hrow, hrow,
                   pl.BlockSpec((h4, ML_DH, tm), lambda i: (0, 0, i)),
                   row(2 * h4), pl.BlockSpec((2 * h4, tm), lambda i: (0, i))],
        out_shape=[jax.ShapeDtypeStruct((n, POOL_WIDTH), F32), hsh, hsh, hsh, hsh,
                   jax.ShapeDtypeStruct((h4, ML_DH, n), F32),
                   jax.ShapeDtypeStruct((n, 2 * h4), F32), jax.ShapeDtypeStruct((2 * h4, n), F32)],
        compiler_params=_params(("parallel",)), name="even_in")(
            x, g.reshape(1, d), wu, wh, wg, wg.T, b_gate.reshape(1, -1), b_gate.reshape(-1, 1))


def _pool_body(u_ref, pre_ref, w_ref, sc_ref, y_ref, carry, full, *, tb, pos0):
    t = pl.program_id(1)

    @pl.when(t == 0)
    def _():
        carry[...] = pre_ref[...]

    u = u_ref[...]
    full[0:POOL_HALO] = carry[...]
    full[POOL_HALO:] = u
    acc = full[...]
    sums = []
    for sh in (1, 2, 4, 8):
        acc = acc + pltpu.roll(acc, sh, 0)
        sums.append(acc[POOL_HALO:])
    lane = lax.broadcasted_iota(jnp.int32, (tb, POOL_WIDTH), 1)
    grp = lane // POOL_GDIM
    win = jnp.where(grp == 0, sums[0], jnp.where(grp == 1, sums[1], jnp.where(grp == 2, sums[2], sums[3])))
    width = jnp.where(grp == 0, 2, jnp.where(grp == 1, 4, jnp.where(grp == 2, 8, 16)))
    pos = pos0 + t * tb + lax.broadcasted_iota(jnp.int32, (tb, POOL_WIDTH), 0)
    cnt = jnp.minimum(pos + 1, width).astype(F32)
    mixed = (win / cnt - u).astype(BF16)
    y_ref[...] = _dot(mixed, w_ref[...]) * sc_ref[...]
    carry[...] = full[tb:tb + POOL_HALO]


def _pool(u, prefix, w_pool, scale, pos0):
    b, t, c = u.shape
    tb = 512 if t % 512 == 0 else t
    pre = jnp.pad(prefix, ((0, 0), (POOL_HALO - prefix.shape[1], 0), (0, 0)))
    wbd = jax.scipy.linalg.block_diag(*[w_pool[i] for i in range(w_pool.shape[0])]).astype(BF16)
    return pl.pallas_call(
        functools.partial(_pool_body, tb=tb, pos0=pos0), grid=(b, t // tb),
        in_specs=[pl.BlockSpec((None, tb, c), lambda i, j: (i, j, 0)),
                  pl.BlockSpec((None, POOL_HALO, c), lambda i, j: (i, 0, 0)),
                  _const_spec((c, c)), _const_spec((1, c))],
        out_specs=pl.BlockSpec((None, tb, c), lambda i, j: (i, j, 0)),
        out_shape=jax.ShapeDtypeStruct((b, t, c), F32),
        scratch_shapes=[pltpu.VMEM((POOL_HALO, c), F32), pltpu.VMEM((tb + POOL_HALO, c), F32)],
        compiler_params=_params(("parallel", "arbitrary")), name="pool")(u, pre, wbd, scale.reshape(1, c))


def _log_sigmoid(x):
    return jnp.minimum(x, 0.0) - jnp.log(1.0 + jnp.exp(-jnp.abs(x)))


def _mlstm_body(q_ref, k_ref, v_ref, vt_ref, gc_ref, gr_ref, c0_ref, n0_ref, m0_ref,
                h_ref, c_ref, n_ref, m_ref, *, L, t_valid):
    @pl.when(pl.program_id(1) == 0)
    def _():
        c_ref[...] = c0_ref[...]
        n_ref[...] = n0_ref[...]
        m_ref[...] = m0_ref[...]

    gc = gc_ref[...]
    gr = gr_ref[...]
    lf_c = _log_sigmoid(gc)
    lf_r = _log_sigmoid(gr)
    row = lax.broadcasted_iota(jnp.int32, (L, L), 0)
    col = lax.broadcasted_iota(jnp.int32, (L, L), 1)
    tok_c = lax.broadcasted_iota(jnp.int32, (L, 1), 0) < t_valid
    tok_r = lax.broadcasted_iota(jnp.int32, (1, L), 1) < t_valid
    if t_valid < L:
        lf_c = jnp.where(tok_c, lf_c, 0.0)
        lf_r = jnp.where(tok_r, lf_r, 0.0)
    causal = row >= col
    tri3 = jnp.concatenate([causal.astype(BF16)] * 3, axis=1)
    cs_c = _dot(tri3, _split3(lf_c, 0))
    cs_r = _dot_nt(_split3(lf_r, 1), tri3)
    heads = range(ML_HEADS)
    q = [q_ref[hd] for hd in heads]
    kf = [k_ref[hd] * (ML_DH ** -0.5) for hd in heads]
    qb = [x.astype(BF16) for x in q]
    kb = [x.astype(BF16) for x in kf]
    c_old = [c_ref[hd] for hd in heads]
    n_old = [n_ref[hd] for hd in heads]
    m_prev = [m_ref[hd] for hd in heads]
    qk = [_dot_nt(qb[hd], kb[hd]) for hd in heads]
    qc = [_dot_nt(qb[hd], c_old[hd].astype(BF16)) for hd in heads]
    i_c, i_r, b_c, b_r, mt, a, s = [], [], [], [], [], [], []
    for hd in heads:
        ic = gc[:, hd:hd + 1]
        ir = gr[hd:hd + 1, :]
        if t_valid < L:
            ic = jnp.where(tok_c, ic, -jnp.inf)
            ir = jnp.where(tok_r, ir, -jnp.inf)
        bc = cs_c[:, ML_HEADS + hd:ML_HEADS + hd + 1]
        br = cs_r[ML_HEADS + hd:ML_HEADS + hd + 1, :]
        dmat = jnp.where(causal, bc - br + ir, -jnp.inf)
        inter = bc + m_prev[hd]
        mth = jnp.maximum(jnp.max(dmat, axis=1, keepdims=True), inter)
        s.append(qk[hd] * jnp.exp(dmat - mth))
        a.append(jnp.exp(inter - mth))
        i_c.append(ic), i_r.append(ir), b_c.append(bc), b_r.append(br), mt.append(mth)
    sv = [_dot(s[hd].astype(BF16), v_ref[hd].astype(BF16)) for hd in heads]
    wk_r, wk_c, decay, m_new = [], [], [], []
    for hd in heads:
        b_last = b_c[hd][L - 1:L, :]
        ge_r = b_last - b_r[hd] + i_r[hd]
        ge_c = b_last - b_c[hd] + i_c[hd]
        mn = jnp.maximum(b_last + m_prev[hd], jnp.max(ge_r, axis=1, keepdims=True))
        wk_r.append(jnp.exp(ge_r - mn))
        wk_c.append(jnp.exp(ge_c - mn))
        decay.append(jnp.exp(b_last + m_prev[hd] - mn))
        m_new.append(mn)
    kv = [_dot((vt_ref[hd] * wk_r[hd]).astype(BF16), kb[hd]) for hd in heads]
    for hd in heads:
        num = sv[hd] + a[hd] * qc[hd]
        den = jnp.sum(s[hd], axis=1, keepdims=True) + a[hd] * jnp.sum(q[hd] * n_old[hd], axis=1, keepdims=True)
        h_ref[hd] = num / jnp.maximum(jnp.abs(den), jnp.exp(-mt[hd]))
    for hd in heads:
        c_ref[hd] = decay[hd] * c_old[hd] + kv[hd]
        n_ref[hd] = decay[hd] * n_old[hd] + jnp.sum(kf[hd] * wk_c[hd], axis=0, keepdims=True)
        m_ref[hd] = m_new[hd]


def _mlstm(q, k, v, vt, gc, gr, c0, n0, m0, *, L, t_valid):
    h4, b, t, dh = q.shape
    nc = t // L
    tok = pl.BlockSpec((h4, None, L, dh), lambda i, c: (0, i, c, 0))
    if vt.ndim == 4:
        vt_spec = pl.BlockSpec((h4, None, dh, L), lambda i, c: (0, i, 0, c))
        gr_spec = pl.BlockSpec((None, 2 * h4, L), lambda i, c: (i, 0, c))
    else:
        vt_spec = pl.BlockSpec((h4, dh, L), lambda i, c: (0, 0, i * nc + c))
        gr_spec = pl.BlockSpec((2 * h4, L), lambda i, c: (0, i * nc + c))
    st = lambda r, w: pl.BlockSpec((None, h4, r, w), lambda i, c: (i, 0, 0, 0))
    return pl.pallas_call(
        functools.partial(_mlstm_body, L=L, t_valid=t_valid), grid=(b, nc),
        in_specs=[tok, tok, tok, vt_spec, pl.BlockSpec((None, L, 2 * h4), lambda i, c: (i, c, 0)), gr_spec,
                  st(dh, dh), st(1, dh), st(1, 1)],
        out_specs=[tok, st(dh, dh), st(1, dh), st(1, 1)],
        out_shape=[jax.ShapeDtypeStruct((h4, b, t, dh), F32), jax.ShapeDtypeStruct((b, h4, dh, dh), F32),
                   jax.ShapeDtypeStruct((b, h4, 1, dh), F32), jax.ShapeDtypeStruct((b, h4, 1, 1), F32)],
        compiler_params=_params(("parallel", "arbitrary")), name="mlstm")(
            q, k, v, vt, gc, gr, c0, n0.reshape(b, h4, 1, dh), m0.reshape(b, h4, 1, 1))


def _even_out_body(x_ref, yp_ref, og_ref, hm_ref, w_ref, o_ref):
    gated = [(jax.nn.sigmoid(og_ref[h]) * hm_ref[h]).astype(BF16) for h in range(ML_HEADS)]
    mixed = jnp.concatenate([yp_ref[...].astype(BF16)] + gated, axis=1)
    o_ref[...] = x_ref[...] + _dot(mixed, w_ref[...])


def _even_out(x, ypool, og, hm, w_out):
    n, d = x.shape
    tm = 512 if n % 512 == 0 else n
    w = w_out.astype(BF16)
    row = lambda w: pl.BlockSpec((tm, w), lambda i: (i, 0))
    hrow = pl.BlockSpec((ML_HEADS, tm, ML_DH), lambda i: (0, i, 0))
    return pl.pallas_call(
        _even_out_body, grid=(n // tm,),
        in_specs=[row(d), row(POOL_WIDTH), hrow, hrow, _const_spec(w.shape)],
        out_specs=row(d), out_shape=jax.ShapeDtypeStruct((n, d), F32),
        compiler_params=_params(("parallel",)), name="even_out")(x, ypool, og, hm, w)


def _odd_in_body(x_ref, g_ref, wq_ref, wkvt_ref, wgt_ref, wc_ref,
                 q_ref, rows_ref, win_ref, gates_ref, ucv_ref, bg_ref):
    hn = _rms(x_ref[...], g_ref[...]).astype(BF16)
    q_ref[...] = _dot(hn, wq_ref[...]) * (LOG2E * HEAD_DIM ** -0.5)
    nrow = rows_ref.shape[0]
    rows_ref[...] = _dot_nt(wkvt_ref[:nrow], hn)
    win_ref[...] = _dot_nt(wkvt_ref[nrow:], hn)
    gates_ref[...] = jax.nn.sigmoid(_dot(hn, wgt_ref[...]))
    bg_ref[...] = _dot(hn, wc_ref[:, :CONV_CH])
    ucv_ref[...] = _dot(hn, wc_ref[:, CONV_CH:2 * CONV_CH]) * _dot(hn, wc_ref[:, 2 * CONV_CH:])


def _odd_in(x, g, w_in, seq):
    n, d = x.shape
    tm = 512 if seq % 512 == 0 else seq
    nt = seq // tm
    kvw = 6 * KV_HEADS * HEAD_DIM
    ngt = 3 * NSA_HEADS
    wq = w_in[:, :NSA_WIDTH].astype(BF16)
    wkvt = w_in[:, NSA_WIDTH:NSA_WIDTH + kvw].T.astype(BF16)
    wgt = w_in[:, NSA_WIDTH + kvw:NSA_WIDTH + kvw + ngt].astype(BF16)
    wc = w_in[:, NSA_WIDTH + kvw + ngt:].astype(BF16)
    nrow = 4 * KV_HEADS * HEAD_DIM
    row = lambda w: pl.BlockSpec((tm, w), lambda i: (i, 0))
    slab = lambda r: pl.BlockSpec((None, r, tm), lambda i: (i // nt, 0, i % nt))
    sh = lambda w: jax.ShapeDtypeStruct((n, w), F32)
    return pl.pallas_call(
        _odd_in_body, grid=(n // tm,),
        in_specs=[row(d), _const_spec((1, d)), _const_spec(wq.shape), _const_spec(wkvt.shape),
                  _const_spec(wgt.shape), _const_spec(wc.shape)],
        out_specs=[row(NSA_WIDTH),
                   slab(nrow), slab(kvw - nrow), row(ngt), row(CONV_CH), row(CONV_CH)],
        out_shape=[sh(NSA_WIDTH),
                   jax.ShapeDtypeStruct((n // seq, nrow, seq), F32),
                   jax.ShapeDtypeStruct((n // seq, kvw - nrow, seq), F32),
                   sh(ngt), sh(CONV_CH), sh(CONV_CH)],
        compiler_params=_params(("parallel",)), name="odd_in")(x, g.reshape(1, d), wq, wkvt, wgt, wc)


def _conv_body(u_ref, bg_ref, pre_ref, w_ref, y_ref, carry, full, *, tb):
    @pl.when(pl.program_id(1) == 0)
    def _():
        carry[...] = pre_ref[...]

    full[0:CONV_HALO] = carry[...]
    full[CONV_HALO:] = u_ref[...]
    f = full[...]
    w = w_ref[...]
    conv = f * w[2:3] + pltpu.roll(f, 1, 0) * w[1:2] + pltpu.roll(f, 2, 0) * w[0:1]
    y_ref[...] = bg_ref[...] * conv[CONV_HALO:]
    carry[...] = full[tb:tb + CONV_HALO]


def _conv(u, bg, prefix, conv_w):
    b, t, c = u.shape
    tb = 512 if t % 512 == 0 else t
    pre = jnp.pad(prefix, ((0, 0), (CONV_HALO - prefix.shape[1], 0), (0, 0)))
    blk = pl.BlockSpec((None, tb, c), lambda i, j: (i, j, 0))
    return pl.pallas_call(
        functools.partial(_conv_body, tb=tb), grid=(b, t // tb),
        in_specs=[blk, blk, pl.BlockSpec((None, CONV_HALO, c), lambda i, j: (i, 0, 0)), _const_spec((CONV_K, c))],
        out_specs=blk, out_shape=jax.ShapeDtypeStruct((b, t, c), F32),
        scratch_shapes=[pltpu.VMEM((CONV_HALO, c), F32), pltpu.VMEM((tb + CONV_HALO, c), F32)],
        compiler_params=_params(("parallel", "arbitrary")), name="conv")(u, bg, pre, conv_w)


def _odd_out_body(x_ref, o_ref, yc_ref, w0_ref, w1_ref, out_ref):
    out_ref[...] = (x_ref[...] + _dot(o_ref[...].astype(BF16), w0_ref[...])
                    + _dot(yc_ref[...].astype(BF16), w1_ref[...]))


def _odd_out(x, o, yconv, w_out):
    n, d = x.shape
    tm = 512 if n % 512 == 0 else n
    w0 = w_out[:NSA_WIDTH].astype(BF16)
    w1 = w_out[NSA_WIDTH:].astype(BF16)
    row = lambda w: pl.BlockSpec((tm, w), lambda i: (i, 0))
    return pl.pallas_call(
        _odd_out_body, grid=(n // tm,),
        in_specs=[row(d), row(NSA_WIDTH), row(CONV_CH), _const_spec(w0.shape), _const_spec(w1.shape)],
        out_specs=row(d), out_shape=jax.ShapeDtypeStruct((n, d), F32),
        compiler_params=_params(("parallel",)), name="odd_out")(x, o, yconv, w0, w1)


CMP_TILE = 2048
CMP_ROWS = 4 * HEAD_DIM


def _cmp_weights(cmp_pos_w, cmp_w, cmp_b, tile):
    m = tile // CMP_STRIDE
    cols = -(-(m + 1) // 128) * 128
    pos = jnp.arange(tile)
    chunk = (pos // CMP_STRIDE)[:, None]
    col = jnp.arange(cols)[None, :]
    mats = []
    for kv in range(2):
        wa = cmp_pos_w[kv, :CMP_STRIDE][pos % CMP_STRIDE][:, None]
        wb = cmp_pos_w[kv, CMP_STRIDE:][pos % CMP_STRIDE][:, None]
        mats.append(jnp.where(chunk == col, wb, 0.0) + jnp.where(chunk == col - 1, wa, 0.0))
    w2 = jnp.stack(mats).astype(BF16)
    wbdt = jax.scipy.linalg.block_diag(cmp_w[0], cmp_w[0], cmp_w[1], cmp_w[1]).T.astype(BF16)
    biast = jnp.concatenate([cmp_b[0], cmp_b[0], cmp_b[1], cmp_b[1]]).reshape(-1, 1)
    return w2, wbdt, biast


def _compress_body(*refs, n_pages, n_prefetch):
    refs = refs[n_prefetch:]
    pages = refs[:n_pages]
    w2_ref, wbdt_ref, biast_ref, out_ref, carry = refs[n_pages:]
    m_out = out_ref.shape[1]

    @pl.when(pl.program_id(1) == 0)
    def _():
        carry[...] = jnp.zeros(carry.shape, F32)

    x = jnp.concatenate([pg[...] for pg in pages], axis=1).astype(BF16)
    half = CMP_ROWS // 2
    res = jnp.concatenate([_dot(x[:half], w2_ref[0]), _dot(x[half:], w2_ref[1])], axis=0)
    first = lax.broadcasted_iota(jnp.int32, (CMP_ROWS, m_out), 1) == 0
    pre = res[:, :m_out] + jnp.where(first, carry[...], 0.0)
    carry[...] = jnp.broadcast_to(res[:, m_out:m_out + 1], carry.shape)
    out_ref[...] = _dot(wbdt_ref[...], pre.astype(BF16)) + biast_ref[...]


def _compress_prompt(rows_t, cmp_pos_w, cmp_w, cmp_b):
    b, _, t = rows_t.shape
    tile = min(CMP_TILE, t)
    assert t % tile == 0
    w2, wbdt, biast = _cmp_weights(cmp_pos_w, cmp_w, cmp_b, tile)
    m_out = tile // CMP_STRIDE
    return pl.pallas_call(
        functools.partial(_compress_body, n_pages=1, n_prefetch=0), grid=(b, t // tile),
        in_specs=[pl.BlockSpec((None, CMP_ROWS, tile), lambda i, j: (i, 0, j)),
                  _const_spec(w2.shape), _const_spec(wbdt.shape), _const_spec(biast.shape)],
        out_specs=pl.BlockSpec((None, CMP_ROWS, m_out), lambda i, j: (i, 0, j)),
        out_shape=jax.ShapeDtypeStruct((b, CMP_ROWS, t // CMP_STRIDE), F32),
        scratch_shapes=[pltpu.VMEM((CMP_ROWS, m_out), F32)],
        compiler_params=_params(("parallel", "arbitrary")), name="compress_prompt")(rows_t, w2, wbdt, biast)


def _compress_paged(cache_t, page_table, cmp_pos_w, cmp_w, cmp_b):
    b, npg = page_table.shape
    pg = min(CMP_TILE // PAGE_SIZE, npg)
    assert npg % pg == 0
    tile = pg * PAGE_SIZE
    w2, wbdt, biast = _cmp_weights(cmp_pos_w, cmp_w, cmp_b, tile)
    m_out = tile // CMP_STRIDE
    specs = [pl.BlockSpec((None, CMP_ROWS, PAGE_SIZE),
                          functools.partial(lambda i, j, pt, p: (pt[i, j * pg + p], 0, 0), p=p))
             for p in range(pg)]
    return pl.pallas_call(
        functools.partial(_compress_body, n_pages=pg, n_prefetch=1),
        grid_spec=pltpu.PrefetchScalarGridSpec(
            num_scalar_prefetch=1, grid=(b, npg // pg),
            in_specs=specs + [_const_spec(w2.shape), _const_spec(wbdt.shape), _const_spec(biast.shape)],
            out_specs=pl.BlockSpec((None, CMP_ROWS, m_out), lambda i, j, pt: (i, 0, j)),
            scratch_shapes=[pltpu.VMEM((CMP_ROWS, m_out), F32)]),
        out_shape=jax.ShapeDtypeStruct((b, CMP_ROWS, npg * PAGE_SIZE // CMP_STRIDE), F32),
        compiler_params=_params(("parallel", "arbitrary")), name="compress_paged")(
            page_table, *([cache_t] * pg), w2, wbdt, biast)


def _alibi_slopes(rows_per_head):
    sl = (LOG2E * 2.0 ** (-8.0 * np.arange(1, NSA_HEADS + 1) / NSA_HEADS)).astype(np.float32).reshape(KV_HEADS, GQ)
    return jnp.asarray(np.repeat(sl, rows_per_head, axis=1)[:, :, None])


def _score_matrix(n_cmp_rows, n_blocks, lanes):
    m = np.arange(n_cmp_rows)[:, None]
    j = np.arange(lanes)[None, :]
    return jnp.asarray(((m >= 4 * j) & (m <= 4 * j + 4) & (m >= 1) & (j < n_blocks)).astype(np.float32))


def _group_queries(q_ref, g):
    heads = [q_ref[:, (g * GQ + i) * HEAD_DIM:(g * GQ + i + 1) * HEAD_DIM] for i in range(GQ)]
    return jnp.concatenate(heads, axis=0).astype(BF16)


def _softmax_rows(s, mask):
    s = jnp.where(mask, s, -jnp.inf)
    m = jnp.max(s, axis=-1, keepdims=True)
    m = jnp.where(m > -jnp.inf, m, 0.0)
    e = jnp.exp2(s - m)
    return e * (1.0 / jnp.maximum(jnp.sum(e, axis=-1, keepdims=True), 1e-30))


def _select_blocks(score, cur, n_blocks):
    r, lanes = score.shape
    blk = lax.broadcasted_iota(jnp.int32, (r, lanes), 1)
    forced = (blk == 0) | (blk == cur) | (blk == cur - 1)
    val = jnp.where(forced, jnp.inf, jnp.where(blk <= cur, score, -jnp.inf))
    rank = jnp.zeros((r, lanes), jnp.int32)
    for i in range(n_blocks):
        ci = val[:, i:i + 1]
        ahead = (ci > val) | ((ci == val) & (blk > i))
        rank = rank + ahead.astype(jnp.int32)
    return ((rank < min(SEL_TOPN, n_blocks)) & (blk < n_blocks)).astype(F32)


POS_ROWS = 16


def _slope_pieces(rows_per_head):
    sl = _alibi_slopes(rows_per_head)
    hi = sl.astype(BF16)
    mid = (sl - hi.astype(F32)).astype(BF16)
    lo = (sl - hi.astype(F32) - mid.astype(F32)).astype(BF16)
    pad = jnp.zeros(sl.shape[:2] + (POS_ROWS - 6,), BF16)
    return jnp.concatenate([hi, mid, lo, hi, mid, lo, pad], axis=2)


def _pos_rows(t_len, stride=1, offset=0):
    k = stride * np.arange(t_len) + offset
    hi = (SEL_BLOCK * (k // SEL_BLOCK)).astype(np.float32)
    lo = (k % SEL_BLOCK).astype(np.float32)
    return np.stack([hi, hi, hi, lo, lo, lo] + [np.zeros(t_len, np.float32)] * (POS_ROWS - 6))


def _block_rows(n_rows, n_keys):
    return (np.arange(n_rows)[:, None] == (np.arange(n_keys) // SEL_BLOCK)[None, :]).astype(np.float32)


def _key_constants(t_len):
    both = np.concatenate([_pos_rows(t_len), _block_rows(t_len // SEL_BLOCK, t_len)], axis=0)
    return jnp.asarray(both).astype(BF16)


def _select_bias_t(score_t, cur):
    n_blocks, r = score_t.shape
    blk = lax.broadcasted_iota(jnp.int32, (n_blocks, r), 0)
    forced = (blk == 0) | (blk == cur) | (blk == cur - 1)
    val = jnp.where(forced, jnp.inf, jnp.where(blk <= cur, score_t, -jnp.inf))

    rank = jnp.zeros((n_blocks, r), F32)
    for i in range(n_blocks):
        ci = val[i:i + 1, :]
        ahead = (ci > val) | ((ci == val) & (blk > i))
        rank = rank + jnp.where(ahead, 1.0, 0.0)
    return jnp.where((rank < min(SEL_TOPN, n_blocks)) & (blk <= cur), 0.0, NEG)


def _nsa_prompt_body(q_ref, gates_ref, kvc_ref, sel_ref, win_ref, saug_ref, kconst_ref, cconst_ref,
                     smat_ref, o_ref, need_sc, *, t_len, kc, wb):
    bi = pl.program_id(1)
    p0 = bi * Q_BLOCK
    nrow = GQ * Q_BLOCK
    n_cmp = kvc_ref.shape[1]
    n_blocks = t_len // SEL_BLOCK
    qpos = lax.broadcasted_iota(jnp.int32, (Q_BLOCK, 1), 0) + p0
    tpos = jnp.concatenate([qpos] * GQ, axis=0)
    gates = gates_ref[...]

    qgs = [_group_queries(q_ref, g) for g in range(KV_HEADS)]
    q_pos = [jnp.concatenate([qgs[g], saug_ref[g]], axis=1) for g in range(KV_HEADS)]

    cpos = CMP_STRIDE * lax.broadcasted_iota(jnp.int32, (1, n_cmp), 1) + (CMP_STRIDE - 1)
    dist = tpos - cpos
    cmask = (dist >= 0) & (cpos >= 2 * CMP_STRIDE - 1)

    def compressed_scores(g):
        ks = g * HEAD_DIM
        katc = jnp.concatenate([kvc_ref[ks:ks + HEAD_DIM, :].astype(BF16), cconst_ref[...]], axis=0)
        return _dot(q_pos[g], katc)

    wstart = pl.multiple_of(jnp.clip((p0 - WINDOW) // 128 * 128, 0, t_len - wb), 128)
    wpos = wstart + lax.broadcasted_iota(jnp.int32, (1, wb), 1)
    wmask = (wpos <= tpos) & (wpos > tpos - WINDOW)

    def window_scores(g):
        katw = jnp.concatenate([win_ref[g * HEAD_DIM:(g + 1) * HEAD_DIM, pl.ds(wstart, wb)].astype(BF16),
                                kconst_ref[0:POS_ROWS, pl.ds(wstart, wb)]], axis=0)
        return _dot(q_pos[g], katw)

    def window_output(g, e_win):
        vs = (KV_HEADS + g) * HEAD_DIM
        vwt = jnp.concatenate([win_ref[vs:vs + HEAD_DIM, pl.ds(wstart, wb)].astype(BF16),
                               jnp.ones((POS_ROWS, wb), BF16)], axis=0)
        acc_w = _dot_nt(e_win, vwt)
        return acc_w[:, :HEAD_DIM] / acc_w[:, HEAD_DIM:HEAD_DIM + 1]

    s_cmp = [compressed_scores(g) for g in range(KV_HEADS)]
    s_win = [window_scores(g) for g in range(KV_HEADS)]
    p_cmps = [_softmax_rows(s_cmp[g], cmask) for g in range(KV_HEADS)]
    e_wins = []
    for g in range(KV_HEADS):
        sw = jnp.where(wmask, s_win[g], -jnp.inf)
        e_wins.append(jnp.exp2(sw - jnp.max(sw, axis=1, keepdims=True)).astype(BF16))
    o_cmps = [_dot_nt(p_cmps[g].astype(BF16), kvc_ref[(KV_HEADS + g) * HEAD_DIM:(KV_HEADS + g + 1) * HEAD_DIM, :]
                      .astype(BF16)) for g in range(KV_HEADS)]
    imps = [jnp.sum(p_cmps[g].reshape(GQ, Q_BLOCK, n_cmp), axis=0) for g in range(KV_HEADS)]
    o_wins = [window_output(g, e_wins[g]) for g in range(KV_HEADS)]
    score_t = _dot_nt(smat_ref[...], _split3(jnp.concatenate(imps, axis=0), 1))
    bias_f = _select_bias_t(score_t, bi)
    bias_qb = bias_f.T
    biases = [bias_qb[g * Q_BLOCK:(g + 1) * Q_BLOCK] for g in range(KV_HEADS)]
    q_aug = [jnp.concatenate([q_pos[g], jnp.concatenate([biases[g]] * GQ, axis=0).astype(BF16)], axis=1)
             for g in range(KV_HEADS)]
    for hh in range(NSA_HEADS):
        g, r = hh // GQ, slice((hh % GQ) * Q_BLOCK, (hh % GQ + 1) * Q_BLOCK)
        o_ref[:, hh * HEAD_DIM:(hh + 1) * HEAD_DIM] = (
            gates[:, 3 * hh:3 * hh + 1] * o_cmps[g][r] + gates[:, 3 * hh + 2:3 * hh + 3] * o_wins[g][r])

    ones_rows = jnp.ones((POS_ROWS, kc), BF16)

    def chunk_scores(g, c):
        k0 = pl.multiple_of(c * kc, kc)
        kat = jnp.concatenate([sel_ref[g * HEAD_DIM:(g + 1) * HEAD_DIM, pl.ds(k0, kc)].astype(BF16),
                               kconst_ref[:, pl.ds(k0, kc)]], axis=0)
        return _dot(q_aug[g], kat), k0

    def probabilities(s, m_i):
        m_new = jnp.maximum(m_i, jnp.max(s, axis=1, keepdims=True))
        return m_new, jnp.exp2(s - m_new).astype(BF16), jnp.exp2(m_i - m_new)

    def weighted_values(g, pr, k0):
        vs = (KV_HEADS + g) * HEAD_DIM
        vt = jnp.concatenate([sel_ref[vs:vs + HEAD_DIM, pl.ds(k0, kc)].astype(BF16), ones_rows], axis=0)
        return _dot_nt(pr, vt)

    def chunk_update(c, carry, mask=None):
        scores = [chunk_scores(g, c) for g in range(KV_HEADS)]
        if mask is not None:
            scores = [(jnp.where(mask(k0), s, NEG), k0) for s, k0 in scores]
        probs = [probabilities(scores[g][0], carry[g][0]) for g in range(KV_HEADS)]
        out = []
        for g in range(KV_HEADS):
            m_new, pr, alpha = probs[g]
            out.append((m_new, alpha * carry[g][1] + weighted_values(g, pr, scores[g][1])))
        return tuple(out)

    last = (p0 + Q_BLOCK - 1) // kc
    blocks_per_chunk = kc // SEL_BLOCK
    n_need = jnp.int32(0)
    for c in range(t_len // kc - 1):
        picked = bias_f[c * blocks_per_chunk:(c + 1) * blocks_per_chunk, :] == 0.0
        need = jnp.where(c < last, jnp.max(jnp.where(picked, 1, 0)), 0)
        need_sc[n_need] = c
        n_need = n_need + need

    one = (jnp.full((nrow, 1), NEG, F32), jnp.zeros((nrow, HEAD_DIM + POS_ROWS), F32))
    carry = lax.fori_loop(0, n_need, lambda i, carry: chunk_update(need_sc[i], carry), (one,) * KV_HEADS)
    causal = lambda k0: k0 + lax.broadcasted_iota(jnp.int32, (1, kc), 1) <= tpos
    carry = chunk_update(last, carry, mask=causal)
    for g in range(KV_HEADS):
        acc_s = carry[g][1]
        o_sel = acc_s[:, :HEAD_DIM] / jnp.maximum(acc_s[:, HEAD_DIM:HEAD_DIM + 1], 1e-30)
        for i in range(GQ):
            hh = g * GQ + i
            r = slice(i * Q_BLOCK, (i + 1) * Q_BLOCK)
            o_ref[:, hh * HEAD_DIM:(hh + 1) * HEAD_DIM] += gates[:, 3 * hh + 1:3 * hh + 2] * o_sel[r]


def _nsa_prompt(q, gates, kvc_t, rows_t, win_t):
    b, t, _ = q.shape
    assert t % 128 == 0
    kc = 512 if t % 512 == 0 else t
    wb = min(WINDOW + 128, t)
    n_cmp = kvc_t.shape[2]
    n_blocks = t // SEL_BLOCK
    ncol = 4 * HEAD_DIM
    saug = _slope_pieces(Q_BLOCK)
    kconst = _key_constants(t)
    cconst = jnp.asarray(_pos_rows(n_cmp, CMP_STRIDE, CMP_STRIDE - 1)).astype(BF16)
    smat = jnp.tile(_score_matrix(n_cmp, n_blocks, n_blocks).T, (1, 3)).astype(BF16)
    return pl.pallas_call(
        functools.partial(_nsa_prompt_body, t_len=t, kc=kc, wb=wb), grid=(b, t // Q_BLOCK),
        in_specs=[pl.BlockSpec((None, Q_BLOCK, NSA_WIDTH), lambda i, j: (i, j, 0)),
                  pl.BlockSpec((None, Q_BLOCK, 3 * NSA_HEADS), lambda i, j: (i, j, 0)),
                  pl.BlockSpec((None, ncol, n_cmp), lambda i, j: (i, 0, 0)),
                  pl.BlockSpec((None, ncol, t), lambda i, j: (i, 1, 0)),
                  pl.BlockSpec((None, ncol, t), lambda i, j: (i, 0, 0)),
                  _const_spec(saug.shape), _const_spec(kconst.shape), _const_spec(cconst.shape),
                  _const_spec(smat.shape)],
        out_specs=pl.BlockSpec((None, Q_BLOCK, NSA_WIDTH), lambda i, j: (i, j, 0)),
        out_shape=jax.ShapeDtypeStruct((b, t, NSA_WIDTH), F32),
        scratch_shapes=[pltpu.SMEM((t // kc,), jnp.int32)],
        compiler_params=_params(("parallel", "arbitrary")), name="nsa_prompt")(
            q, gates, kvc_t, rows_t, win_t, saug, kconst, cconst, smat)


def _nsa_dec_select_body(q_ref, kvc_ref, slope_ref, smat_ref, ocmp_ref, sel_ref, *, past_len, tq, n_blocks,
                         step_blocks):
    n_cmp = kvc_ref.shape[1]
    n_steps = sel_ref.shape[1] - 1
    lanes_out = sel_ref.shape[3]
    nrow = GQ * tq
    qpos = lax.broadcasted_iota(jnp.int32, (tq, 1), 0) + past_len
    tpos = jnp.concatenate([qpos] * GQ, axis=0)
    cpos = CMP_STRIDE * lax.broadcasted_iota(jnp.int32, (1, n_cmp), 1) + (CMP_STRIDE - 1)
    dist = tpos - cpos
    for g in range(KV_HEADS):
        qg = _group_queries(q_ref, g)
        ks, vs = g * HEAD_DIM, (KV_HEADS + g) * HEAD_DIM
        s = _dot(qg, kvc_ref[ks:ks + HEAD_DIM, :].astype(BF16)) - slope_ref[g] * dist.astype(F32)
        p_cmp = _softmax_rows(s, (dist >= 0) & (cpos >= 2 * CMP_STRIDE - 1))
        ocmp_ref[g] = _dot_nt(p_cmp.astype(BF16), kvc_ref[vs:vs + HEAD_DIM, :].astype(BF16))
        imp = jnp.sum(p_cmp.reshape(GQ, tq, n_cmp), axis=0)
        score = _dot(_split3(imp, 1), smat_ref[...])
        bias = jnp.where(_select_blocks(score, qpos // SEL_BLOCK, n_blocks) > 0.5, 0.0, NEG)
        pad = jnp.zeros((tq, lanes_out - step_blocks), F32)
        for st in range(n_steps):
            sel_ref[g, st] = jnp.concatenate([bias[:, st * step_blocks:(st + 1) * step_blocks], pad], axis=1)
        last_blk = n_steps * step_blocks
        sel_ref[g, n_steps] = jnp.concatenate([bias[:, last_blk:last_blk + 1],
                                               jnp.zeros((tq, lanes_out - 1), F32)], axis=1)


def _nsa_dec_select(q, kvc, past_len, n_blocks, step_blocks):
    b, tq, _ = q.shape
    n_cmp = kvc.shape[2]
    lanes = -(-n_blocks // 128) * 128
    n_steps = (n_blocks - 1) // step_blocks
    assert n_steps * step_blocks == n_blocks - 1 and step_blocks <= 128
    slopes = _alibi_slopes(tq)
    smat = jnp.tile(_score_matrix(n_cmp, n_blocks, lanes), (3, 1)).astype(BF16)
    return pl.pallas_call(
        functools.partial(_nsa_dec_select_body, past_len=past_len, tq=tq, n_blocks=n_blocks,
                          step_blocks=step_blocks), grid=(b,),
        in_specs=[pl.BlockSpec((None, tq, NSA_WIDTH), lambda i: (i, 0, 0)),
                  pl.BlockSpec((None, 4 * HEAD_DIM, n_cmp), lambda i: (i, 0, 0)),
                  _const_spec(slopes.shape), _const_spec(smat.shape)],
        out_specs=[pl.BlockSpec((None, KV_HEADS, GQ * tq, HEAD_DIM), lambda i: (i, 0, 0, 0)),
                   pl.BlockSpec((None, KV_HEADS, n_steps + 1, tq, 128), lambda i: (i, 0, 0, 0, 0))],
        out_shape=[jax.ShapeDtypeStruct((b, KV_HEADS, GQ * tq, HEAD_DIM), F32),
                   jax.ShapeDtypeStruct((b, KV_HEADS, n_steps + 1, tq, 128), F32)],
        compiler_params=_params(("parallel",)), name="nsa_dec_select")(q, kvc, slopes, smat)


def _nsa_dec_attend_body(pt_ref, q_ref, selb_ref, ocmp_ref, gates_ref, newsel_ref, winbuf_ref, newwin_ref,
                         slope_ref, saug_ref, pos_ref, blk_ref, *rest, past_len, tq, t_valid, n_pages):
    pages = rest[:n_pages]
    o_ref, m_sc, acc_sc = rest[n_pages:]
    step = pl.program_id(1)
    nrow = GQ * tq
    qpos = lax.broadcasted_iota(jnp.int32, (tq, 1), 0) + past_len
    tpos = jnp.concatenate([qpos] * GQ, axis=0)

    @pl.when(step == 0)
    def _():
        m_sc[...] = jnp.full(m_sc.shape, NEG, F32)
        acc_sc[...] = jnp.zeros(acc_sc.shape, F32)

    def update(g, sc, weighted_values):
        m_i = m_sc[g]
        m_new = jnp.maximum(m_i, jnp.max(sc, axis=1, keepdims=True))
        pr = jnp.exp2(sc - m_new)
        acc_sc[g] = jnp.exp2(m_i - m_new) * acc_sc[g] + weighted_values(pr.astype(BF16))
        m_sc[g] = m_new

    nk = n_pages * PAGE_SIZE
    ones_rows = jnp.ones((POS_ROWS, nk), BF16)
    scores = []
    for g in range(KV_HEADS):
        qg = _group_queries(q_ref, g)
        ks = g * HEAD_DIM
        bias = jnp.concatenate([selb_ref[g, step]] * GQ, axis=0).astype(BF16)
        q_aug = jnp.concatenate([qg, saug_ref[g], bias], axis=1)
        kat = jnp.concatenate([jnp.concatenate([pg[ks:ks + HEAD_DIM, :] for pg in pages], axis=1).astype(BF16),
                               pos_ref[...], blk_ref[...]], axis=0)
        scores.append(_dot(q_aug, kat))
    for g in range(KV_HEADS):
        vs = (KV_HEADS + g) * HEAD_DIM
        vt = jnp.concatenate([jnp.concatenate([pg[vs:vs + HEAD_DIM, :] for pg in pages], axis=1).astype(BF16),
                              ones_rows], axis=0)
        update(g, scores[g], lambda p, vt=vt: _dot_nt(p, vt))

    @pl.when(step == pl.num_programs(1) - 1)
    def _():
        gates = gates_ref[...]
        tn = newsel_ref.shape[0]
        npos = past_len + lax.broadcasted_iota(jnp.int32, (1, tn), 1)
        nd = tpos - npos
        valid_new = (nd >= 0) & (npos < past_len + t_valid)
        wlen = winbuf_ref.shape[1]
        wpos = past_len - wlen + lax.broadcasted_iota(jnp.int32, (1, wlen), 1)
        wd = tpos - wpos
        for g in range(KV_HEADS):
            qg = _group_queries(q_ref, g)
            ks, vs = g * HEAD_DIM, (KV_HEADS + g) * HEAD_DIM
            slope = slope_ref[g]
            block_bias = jnp.concatenate([selb_ref[g, selb_ref.shape[1] - 1][:, 0:1]] * GQ, axis=0)
            sc = (_dot_nt(qg, newsel_ref[:, ks:ks + HEAD_DIM].astype(BF16)) + slope * npos.astype(F32)
                  + block_bias)
            new_v = jnp.concatenate([newsel_ref[:, vs:vs + HEAD_DIM].astype(BF16),
                                     jnp.ones((tn, POS_ROWS), BF16)], axis=1)
            update(g, jnp.where(valid_new, sc, NEG), lambda p, new_v=new_v: _dot(p, new_v))
            acc = acc_sc[g]
            o_sel = acc[:, :HEAD_DIM] / jnp.maximum(acc[:, HEAD_DIM:HEAD_DIM + 1], 1e-30)
            s1 = _dot(qg, winbuf_ref[ks:ks + HEAD_DIM, :].astype(BF16)) - slope * wd.astype(F32)
            s2 = _dot_nt(qg, newwin_ref[:, ks:ks + HEAD_DIM].astype(BF16)) - slope * nd.astype(F32)
            ok1 = (wd >= 0) & (wd < WINDOW) & (wpos >= 0)
            ok2 = valid_new & (nd < WINDOW)
            s1 = jnp.where(ok1, s1, -jnp.inf)
            s2 = jnp.where(ok2, s2, -jnp.inf)
            mx = jnp.maximum(jnp.max(s1, axis=1, keepdims=True), jnp.max(s2, axis=1, keepdims=True))
            mx = jnp.where(mx > -jnp.inf, mx, 0.0)
            e1 = jnp.exp2(s1 - mx)
            e2 = jnp.exp2(s2 - mx)
            den = jnp.maximum(jnp.sum(e1, axis=1, keepdims=True) + jnp.sum(e2, axis=1, keepdims=True), 1e-30)
            o_win = (_dot_nt(e1.astype(BF16), winbuf_ref[vs:vs + HEAD_DIM, :].astype(BF16))
                     + _dot(e2.astype(BF16), newwin_ref[:, vs:vs + HEAD_DIM].astype(BF16))) / den
            o_cmp = ocmp_ref[g]
            for i in range(GQ):
                hh = g * GQ + i
                r = slice(i * tq, (i + 1) * tq)
                o_ref[:, hh * HEAD_DIM:(hh + 1) * HEAD_DIM] = (
                    gates[:, 3 * hh:3 * hh + 1] * o_cmp[r] + gates[:, 3 * hh + 1:3 * hh + 2] * o_sel[r]
                    + gates[:, 3 * hh + 2:3 * hh + 3] * o_win[r])


DEC_PAGES = 32


def _nsa_dec_attend(q, selbias, o_cmp, gates, newsel, winbuf, newwin, cache, page_table, past_len, t_valid):
    b, tq, _ = q.shape
    npg = page_table.shape[1]
    pg = min(DEC_PAGES, npg)
    assert npg % pg == 0 and selbias.shape[2] == npg // pg + 1
    nk = pg * PAGE_SIZE
    ncol = 4 * HEAD_DIM
    slopes = _alibi_slopes(tq)
    saug = _slope_pieces(tq)
    pos_rows = jnp.asarray(_pos_rows(past_len)).astype(BF16)
    blk_rows = jnp.asarray(_block_rows(selbias.shape[-1], nk)).astype(BF16)
    wlen = winbuf.shape[2]
    bspec = lambda shape: pl.BlockSpec((None,) + shape, lambda i, j, pt: (i,) + (0,) * len(shape))
    page_specs = [pl.BlockSpec((None, ncol, PAGE_SIZE),
                               functools.partial(lambda i, j, pt, p: (pt[i, j * pg + p], 1, 0), p=p))
                  for p in range(pg)]
    return pl.pallas_call(
        functools.partial(_nsa_dec_attend_body, past_len=past_len, tq=tq, t_valid=t_valid, n_pages=pg),
        grid_spec=pltpu.PrefetchScalarGridSpec(
            num_scalar_prefetch=1, grid=(b, npg // pg),
            in_specs=[pl.BlockSpec((None, tq, NSA_WIDTH), lambda i, j, pt: (i, 0, 0)),
                      bspec(selbias.shape[1:]), bspec((KV_HEADS, GQ * tq, HEAD_DIM)),
                      bspec((tq, 3 * NSA_HEADS)), bspec((tq, ncol)), bspec((ncol, wlen)), bspec((tq, ncol)),
                      _const_spec(slopes.shape), _const_spec(saug.shape),
                      pl.BlockSpec((POS_ROWS, nk), lambda i, j, pt: (0, j)), _const_spec(blk_rows.shape)]
            + page_specs,
            out_specs=bspec((tq, NSA_WIDTH)),
            scratch_shapes=[pltpu.VMEM((KV_HEADS, GQ * tq, 1), F32),
                            pltpu.VMEM((KV_HEADS, GQ * tq, HEAD_DIM + POS_ROWS), F32)]),
        out_shape=jax.ShapeDtypeStruct((b, tq, NSA_WIDTH), F32),
        compiler_params=_params(("parallel", "arbitrary")), name="nsa_dec_attend")(
            page_table, q, selbias, o_cmp, gates, newsel, winbuf, newwin, slopes, saug, pos_rows, blk_rows,
            *([cache] * pg))


def _pad_t(a, axis, to):
    pad = [(0, 0)] * a.ndim
    pad[axis] = (0, to - a.shape[axis])
    return jnp.pad(a, pad)


def _even_layer(x, b, t, pos0, pool_buf, c0, n0, m0, g, w_in, b_gate, w_pool, pool_scale, w_out):
    n = b * t
    u, q, k, v, og, vt, gc, gr = _even_in(x, g, w_in, b_gate)
    u3 = u.reshape(b, t, POOL_WIDTH)
    new_pool = jnp.concatenate([pool_buf, u3], axis=1)[:, -pool_buf.shape[1]:]
    h4 = ML_HEADS
    if t % 256 == 0:
        tp, L = t, 256
        vt_in, gr_in = vt, gr
    else:
        tp = L = -(-t // 8) * 8
        vt_in = _pad_t(vt.reshape(h4, ML_DH, b, t), 3, tp).transpose(0, 2, 1, 3)
        gr_in = _pad_t(gr.reshape(2 * h4, b, t), 2, tp).transpose(1, 0, 2)
    seq = lambda a: _pad_t(a.reshape(h4, b, t, ML_DH), 2, tp)
    ypool = _pool(_pad_t(u3, 1, tp), pool_buf, w_pool, pool_scale, pos0)[:, :t].reshape(n, POOL_WIDTH)
    hm, c, nn, m = _mlstm(seq(q), seq(k), seq(v), vt_in, _pad_t(gc.reshape(b, t, 2 * h4), 1, tp), gr_in,
                          c0, n0, m0, L=L, t_valid=t)
    hm = hm[:, :, :t].reshape(h4, n, ML_DH)
    x = _even_out(x, ypool, og, hm, w_out)
    return x, new_pool, c, nn.reshape(b, h4, ML_DH), m.reshape(b, h4)


def _odd_layer(x, b, t, pos0, cache, page_table, win_buf, conv_buf, g, w_in, cmp_pos_w, cmp_w, cmp_b,
               conv_w, w_out):
    n = b * t
    ncol = 4 * HEAD_DIM
    prompt = cache is None
    q, rows_t, win_t, gates, ucv, bg = _odd_in(x, g, w_in, t if prompt else n)
    u3 = ucv.reshape(b, t, CONV_CH)
    new_conv = jnp.concatenate([conv_buf, u3], axis=1)[:, -(CONV_K - 1):]
    if prompt:
        kvc_t = _compress_prompt(rows_t, cmp_pos_w, cmp_w, cmp_b)
        o = _nsa_prompt(q.reshape(b, t, NSA_WIDTH), gates.reshape(b, t, 3 * NSA_HEADS), kvc_t, rows_t, win_t)
        new_win_t = win_t[:, :, t - min(WINDOW, t):]
        new_rows = rows_t.reshape(b, 4, KV_HEADS, HEAD_DIM, t).transpose(0, 4, 1, 2, 3)
        tp = t
    else:
        assert pos0 % SEL_BLOCK + t <= SEL_BLOCK
        tp = -(-t // 8) * 8
        rows_bt = rows_t[0].T.reshape(b, t, 2 * ncol)
        win_bt = win_t[0].T.reshape(b, t, ncol)
        n_pool = cache.shape[0]
        cache_t = cache.transpose(0, 2, 3, 4, 1).reshape(n_pool, 2 * ncol, PAGE_SIZE)
        kvc_t = _compress_paged(cache_t, page_table, cmp_pos_w, cmp_w, cmp_b)
        qp = _pad_t(q.reshape(b, t, NSA_WIDTH), 1, tp)
        n_blocks = -(-(pos0 + t) // SEL_BLOCK)
        step_blocks = min(DEC_PAGES, page_table.shape[1]) * PAGE_SIZE // SEL_BLOCK
        o_cmp, selmask = _nsa_dec_select(qp, kvc_t, pos0, n_blocks, step_blocks)
        wlen = win_buf.shape[1]
        winb_t = win_buf.transpose(0, 2, 3, 4, 1).reshape(b, ncol, wlen)
        o = _nsa_dec_attend(qp, selmask, o_cmp, _pad_t(gates.reshape(b, t, -1), 1, tp),
                            _pad_t(rows_bt[:, :, ncol:], 1, tp), winb_t, _pad_t(win_bt, 1, tp),
                            cache_t, page_table, pos0, t)[:, :t]
        new_win_t = jnp.concatenate([winb_t, win_bt.transpose(0, 2, 1)], axis=2)[:, :, -wlen:]
        new_rows = rows_bt.reshape(b, t, 4, KV_HEADS, HEAD_DIM)
    yconv = _conv(_pad_t(u3, 1, tp), _pad_t(bg.reshape(b, t, CONV_CH), 1, tp), conv_buf, conv_w)[:, :t]
    x = _odd_out(x, o.reshape(n, NSA_WIDTH), yconv.reshape(n, CONV_CH), w_out)
    new_win = new_win_t.reshape(b, 2, KV_HEADS, HEAD_DIM, new_win_t.shape[2]).transpose(0, 4, 1, 2, 3)
    return x, new_rows, new_win, new_conv


def _trunk(x3, pos0, pool_buf, ml_c, ml_n, ml_m, kv_cache, page_table, win_buf, conv_buf, p):
    b, t, d = x3.shape
    depth = p["norm_g"].shape[0]
    x = x3.reshape(b * t, d)
    pools, cs, ns, ms, rows, wins, convs = [], [], [], [], [], [], []
    for l in range(depth):
        j = l // 2
        x = _ffn(x, p["norm_g"][l, 0], p["w_ffn_in"][l][0], p["w_ffn_out"][l][0])
        if l % 2 == 0:
            x, pb, c, n, m = _even_layer(x, b, t, pos0, pool_buf[j], ml_c[j], ml_n[j], ml_m[j], p["norm_g"][l, 1],
                                         p["w_in_even"][j], p["b_gate_even"][j], p["w_pool"][j],
                                         p["pool_scale"][j], p["w_out_even"][j])
            pools.append(pb)
            cs.append(c)
            ns.append(n)
            ms.append(m)
        else:
            cache = None if kv_cache is None else kv_cache[j]
            wb = None if win_buf is None else win_buf[j]
            x, r, wn, cn = _odd_layer(x, b, t, pos0, cache, page_table, wb, conv_buf[j], p["norm_g"][l, 1],
                                      p["w_in_odd"][j], p["cmp_pos_w"][j], p["cmp_w"][j], p["cmp_b"][j],
                                      p["conv_w"][j], p["w_out_odd"][j])
            rows.append(r)
            wins.append(wn)
            convs.append(cn)
        x = _ffn(x, p["norm_g"][l, 2], p["w_ffn_in"][l][1], p["w_ffn_out"][l][1],
                 final_g=p["final_g"] if l == depth - 1 else None)
    states = (jnp.stack(pools), jnp.stack(cs), jnp.stack(ns), jnp.stack(ms),
              jnp.stack(rows), jnp.stack(wins), jnp.stack(convs))
    return x.reshape(b, t, d), states


def kernel(x_prompt, x_sample, state_pool, state_mlstm_c, state_mlstm_n, state_mlstm_m, cache_nsa_kv, state_win_kv, state_conv, page_table, norm_g, final_g, w_ffn_in, w_ffn_out, w_in_even, b_gate_even, w_pool, pool_scale, w_out_even, w_in_odd, cmp_pos_w, cmp_w, cmp_b, conv_w, w_out_odd):
    bp = x_prompt.shape[0]
    n_even, n_odd = state_pool.shape[0], state_conv.shape[0]
    past_len = page_table.shape[1] * PAGE_SIZE
    p = dict(norm_g=norm_g, final_g=final_g, w_ffn_in=w_ffn_in.astype(BF16), w_ffn_out=w_ffn_out.astype(BF16),
             w_in_even=w_in_even, b_gate_even=b_gate_even, w_pool=w_pool, pool_scale=pool_scale,
             w_out_even=w_out_even, w_in_odd=w_in_odd, cmp_pos_w=cmp_pos_w, cmp_w=cmp_w, cmp_b=cmp_b,
             conv_w=conv_w, w_out_odd=w_out_odd)
    pool0 = jnp.zeros((n_even, bp) + state_pool.shape[2:], F32)
    c0 = jnp.zeros((n_even, bp) + state_mlstm_c.shape[2:], F32)
    n0 = jnp.zeros((n_even, bp) + state_mlstm_n.shape[2:], F32)
    m0 = jnp.zeros((n_even, bp) + state_mlstm_m.shape[2:], F32)
    conv0 = jnp.zeros((n_odd, bp) + state_conv.shape[2:], F32)
    y_p, (pool_p, c_p, n_p, m_p, kv_p, win_p, conv_p) = _trunk(
        x_prompt, 0, pool0, c0, n0, m0, None, None, None, conv0, p)
    y_s, (pool_s, c_s, n_s, m_s, kv_s, win_s, conv_s) = _trunk(
        x_sample, past_len, state_pool, state_mlstm_c, state_mlstm_n, state_mlstm_m,
        cache_nsa_kv, page_table, state_win_kv, state_conv, p)
    return (y_p, y_s, pool_p, pool_s, c_p, c_s, n_p, n_s, m_p, m_s,
            kv_p, kv_s, win_p, win_s, conv_p, conv_s)
```

```python
import functools

import numpy as np
import jax
import jax.numpy as jnp
from jax import lax
from jax.experimental import pallas as pl
from jax.experimental.pallas import tpu as pltpu

F32 = jnp.float32
BF16 = jnp.bfloat16

EPS = 1e-6
POOL_WINDOWS = (2, 4, 8, 16)
POOL_GDIM = 64
POOL_WIDTH = 256
POOL_HALO = 16
ML_HEADS = 4
ML_DH = 192
ML_WIDTH = ML_HEADS * ML_DH
NSA_HEADS = 12
HEAD_DIM = 64
NSA_WIDTH = NSA_HEADS * HEAD_DIM
KV_HEADS = 2
GQ = NSA_HEADS // KV_HEADS
CMP_STRIDE = 16
SEL_BLOCK = 64
SEL_TOPN = 16
WINDOW = 512
Q_BLOCK = 64
PAGE_SIZE = 128
CONV_CH = 256
CONV_K = 3
CONV_HALO = 8

VMEM_LIMIT = 56 * 1024 * 1024
NEG = -1e30
LOG2E = 1.4426950408889634


def _params(sem, vmem=VMEM_LIMIT):
    return pltpu.CompilerParams(dimension_semantics=sem, vmem_limit_bytes=vmem)


def _const_spec(shape):
    nd = len(shape)
    return pl.BlockSpec(shape, lambda *_: (0,) * nd, pipeline_mode=pl.Buffered(1))


def _rms(x, g):
    return x * lax.rsqrt(jnp.mean(x * x, axis=-1, keepdims=True) + EPS) * g


def _dot(a, b):
    return jnp.dot(a, b, preferred_element_type=F32)


def _dot_nt(a, b):
    return lax.dot_general(a, b, (((1,), (1,)), ((), ())), preferred_element_type=F32)


def _split3(x, axis):
    hi = x.astype(BF16)
    r1 = x - hi.astype(F32)
    mid = r1.astype(BF16)
    lo = (r1 - mid.astype(F32)).astype(BF16)
    return jnp.concatenate([hi, mid, lo], axis=axis)


def _ffn_body(x_ref, g_ref, win_ref, wout_ref, *rest, d_ff, chunks, has_final):
    o_ref = rest[-1]
    x = x_ref[...]
    hn = _rms(x, g_ref[...]).astype(BF16)
    acc = jnp.zeros(x.shape, F32)
    off = 0
    for fc in chunks:
        a = _dot(hn, win_ref[:, off:off + fc])
        b = _dot(hn, win_ref[:, d_ff + off:d_ff + off + fc])
        act = (a * jax.nn.sigmoid(a) * b).astype(BF16)
        acc = acc + _dot(act, wout_ref[off:off + fc, :])
        off += fc
    y = x + 0.5 * acc
    if has_final:
        y = _rms(y, rest[0][...])
    o_ref[...] = y


def _ffn(x, g, w_in, w_out, final_g=None):
    n, d = x.shape
    d_ff = w_out.shape[0]
    tm = 512 if n % 512 == 0 else n
    chunks, left = [], d_ff
    while left:
        chunks.append(min(1024, left))
        left -= chunks[-1]
    row = pl.BlockSpec((tm, d), lambda i: (i, 0))
    in_specs = [row, _const_spec((1, d)), _const_spec(w_in.shape), _const_spec(w_out.shape)]
    args = [x, g.reshape(1, d), w_in, w_out]
    if final_g is not None:
        in_specs.append(_const_spec((1, d)))
        args.append(final_g.reshape(1, d))
    return pl.pallas_call(
        functools.partial(_ffn_body, d_ff=d_ff, chunks=tuple(chunks), has_final=final_g is not None),
        grid=(n // tm,), in_specs=in_specs, out_specs=row,
        out_shape=jax.ShapeDtypeStruct((n, d), F32),
        compiler_params=_params(("parallel",)), name="ffn")(*args)


def _even_in_body(x_ref, g_ref, wu_ref, wh_ref, wg_ref, wgt_ref, bg_ref, bgt_ref,
                  u_ref, q_ref, k_ref, v_ref, og_ref, vt_ref, gc_ref, gr_ref):
    hn = _rms(x_ref[...], g_ref[...]).astype(BF16)
    u_ref[...] = _dot(hn, wu_ref[...])
    for seg, out_ref in enumerate((q_ref, k_ref, v_ref, og_ref)):
        z = _dot(hn, wh_ref[seg])
        for h in range(ML_HEADS):
            out_ref[h] = z[:, h * ML_DH:(h + 1) * ML_DH]
            if out_ref is v_ref:
                vt_ref[h] = z[:, h * ML_DH:(h + 1) * ML_DH].T
    gc_ref[...] = _dot(hn, wg_ref[...]) + bg_ref[...]
    gr_ref[...] = _dot_nt(wgt_ref[...], hn) + bgt_ref[...]


def _even_in(x, g, w_in, b_gate):
    n, d = x.shape
    tm = 512 if n % 512 == 0 else n
    h4 = ML_HEADS
    wu = w_in[:, :POOL_WIDTH].astype(BF16)
    wh = w_in[:, POOL_WIDTH:POOL_WIDTH + 4 * ML_WIDTH].reshape(d, 4, ML_WIDTH).transpose(1, 0, 2).astype(BF16)
    wg = w_in[:, POOL_WIDTH + 4 * ML_WIDTH:].astype(BF16)
    row = lambda w: pl.BlockSpec((tm, w), lambda i: (i, 0))
    hrow = pl.BlockSpec((h4, tm, ML_DH), lambda i: (0, i, 0))
    hsh = jax.ShapeDtypeStruct((h4, n, ML_DH), F32)
    return pl.pallas_call(
        _even_in_body, grid=(n // tm,),
        in_specs=[row(d), _const_spec((1, d)), _const_spec(wu.shape), _const_spec(wh.shape),
                  _const_spec(wg.shape), _const_spec((2 * h4, d)),
                  _const_spec((1, 2 * h4)), _const_spec((2 * h4, 1))],
        out_specs=[row(POOL_WIDTH), hrow, hrow, hrow, hrow,
                   pl.BlockSpec((h4, ML_DH, tm), lambda i: (0, 0, i)),
                   row(2 * h4), pl.BlockSpec((2 * h4, tm), lambda i: (0, i))],
        out_shape=[jax.ShapeDtypeStruct((n, POOL_WIDTH), F32), hsh, hsh, hsh, hsh,
                   jax.ShapeDtypeStruct((h4, ML_DH, n), F32),
                   jax.ShapeDtypeStruct((n, 2 * h4), F32), jax.ShapeDtypeStruct((2 * h4, n), F32)],
        compiler_params=_params(("parallel",)), name="even_in")(
            x, g.reshape(1, d), wu, wh, wg, wg.T, b_gate.reshape(1, -1), b_gate.reshape(-1, 1))


def _pool_body(u_ref, pre_ref, w_ref, sc_ref, y_ref, carry, full, *, tb, pos0):
    t = pl.program_id(1)

    @pl.when(t == 0)
    def _():
        carry[...] = pre_ref[...]

    u = u_ref[...]
    full[0:POOL_HALO] = carry[...]
    full[POOL_HALO:] = u
    acc = full[...]
    sums = []
    for sh in (1, 2, 4, 8):
        acc = acc + pltpu.roll(acc, sh, 0)
        sums.append(acc[POOL_HALO:])
    lane = lax.broadcasted_iota(jnp.int32, (tb, POOL_WIDTH), 1)
    grp = lane // POOL_GDIM
    win = jnp.where(grp == 0, sums[0], jnp.where(grp == 1, sums[1], jnp.where(grp == 2, sums[2], sums[3])))
    width = jnp.where(grp == 0, 2, jnp.where(grp == 1, 4, jnp.where(grp == 2, 8, 16)))
    pos = pos0 + t * tb + lax.broadcasted_iota(jnp.int32, (tb, POOL_WIDTH), 0)
    cnt = jnp.minimum(pos + 1, width).astype(F32)
    mixed = (win / cnt - u).astype(BF16)
    y_ref[...] = (_dot(mixed, w_ref[...]) * sc_ref[...]).astype(y_ref.dtype)
    carry[...] = full[tb:tb + POOL_HALO]


def _pool(u, prefix, w_pool, scale, pos0):
    b, t, c = u.shape
    tb = 512 if t % 512 == 0 else t
    pre = jnp.pad(prefix, ((0, 0), (POOL_HALO - prefix.shape[1], 0), (0, 0)))
    wbd = jax.scipy.linalg.block_diag(*[w_pool[i] for i in range(w_pool.shape[0])]).astype(BF16)
    return pl.pallas_call(
        functools.partial(_pool_body, tb=tb, pos0=pos0), grid=(b, t // tb),
        in_specs=[pl.BlockSpec((None, tb, c), lambda i, j: (i, j, 0)),
                  pl.BlockSpec((None, POOL_HALO, c), lambda i, j: (i, 0, 0)),
                  _const_spec((c, c)), _const_spec((1, c))],
        out_specs=pl.BlockSpec((None, tb, c), lambda i, j: (i, j, 0)),
        out_shape=jax.ShapeDtypeStruct((b, t, c), BF16),
        scratch_shapes=[pltpu.VMEM((POOL_HALO, c), F32), pltpu.VMEM((tb + POOL_HALO, c), F32)],
        compiler_params=_params(("parallel", "arbitrary")), name="pool")(u, pre, wbd, scale.reshape(1, c))


def _log_sigmoid(x):
    return jnp.minimum(x, 0.0) - jnp.log(1.0 + jnp.exp(-jnp.abs(x)))


def _mlstm_body(q_ref, k_ref, v_ref, vt_ref, gc_ref, gr_ref, c0_ref, n0_ref, m0_ref,
                h_ref, c_ref, n_ref, m_ref, *, L, t_valid):
    @pl.when(pl.program_id(1) == 0)
    def _():
        c_ref[...] = c0_ref[...]
        n_ref[...] = n0_ref[...]
        m_ref[...] = m0_ref[...]

    gc = gc_ref[...]
    gr = gr_ref[...]
    lf_c = _log_sigmoid(gc)
    lf_r = _log_sigmoid(gr)
    row = lax.broadcasted_iota(jnp.int32, (L, L), 0)
    col = lax.broadcasted_iota(jnp.int32, (L, L), 1)
    tok_c = lax.broadcasted_iota(jnp.int32, (L, 1), 0) < t_valid
    tok_r = lax.broadcasted_iota(jnp.int32, (1, L), 1) < t_valid
    if t_valid < L:
        lf_c = jnp.where(tok_c, lf_c, 0.0)
        lf_r = jnp.where(tok_r, lf_r, 0.0)
    causal = row >= col
    tri3 = jnp.concatenate([causal.astype(BF16)] * 3, axis=1)
    cs_c = _dot(tri3, _split3(lf_c, 0))
    cs_r = _dot_nt(_split3(lf_r, 1), tri3)
    heads = range(ML_HEADS)
    q = [q_ref[hd] for hd in heads]
    kf = [k_ref[hd] * (ML_DH ** -0.5) for hd in heads]
    qb = [x.astype(BF16) for x in q]
    kb = [x.astype(BF16) for x in kf]
    c_old = [c_ref[hd] for hd in heads]
    n_old = [n_ref[hd] for hd in heads]
    m_prev = [m_ref[hd] for hd in heads]
    qk = [_dot_nt(qb[hd], kb[hd]) for hd in heads]
    qc = [_dot_nt(qb[hd], c_old[hd].astype(BF16)) for hd in heads]
    i_c, i_r, b_c, b_r, mt, a, s = [], [], [], [], [], [], []
    for hd in heads:
        ic = gc[:, hd:hd + 1]
        ir = gr[hd:hd + 1, :]
        if t_valid < L:
            ic = jnp.where(tok_c, ic, -jnp.inf)
            ir = jnp.where(tok_r, ir, -jnp.inf)
        bc = cs_c[:, ML_HEADS + hd:ML_HEADS + hd + 1]
        br = cs_r[ML_HEADS + hd:ML_HEADS + hd + 1, :]
        dmat = jnp.where(causal, bc - br + ir, -jnp.inf)
        inter = bc + m_prev[hd]
        mth = jnp.maximum(jnp.max(dmat, axis=1, keepdims=True), inter)
        s.append(qk[hd] * jnp.exp(dmat - mth))
        a.append(jnp.exp(inter - mth))
        i_c.append(ic), i_r.append(ir), b_c.append(bc), b_r.append(br), mt.append(mth)
    sv = [_dot(s[hd].astype(BF16), v_ref[hd].astype(BF16)) for hd in heads]
    wk_r, wk_c, decay, m_new = [], [], [], []
    for hd in heads:
        b_last = b_c[hd][L - 1:L, :]
        ge_r = b_last - b_r[hd] + i_r[hd]
        ge_c = b_last - b_c[hd] + i_c[hd]
        mn = jnp.maximum(b_last + m_prev[hd], jnp.max(ge_r, axis=1, keepdims=True))
        wk_r.append(jnp.exp(ge_r - mn))
        wk_c.append(jnp.exp(ge_c - mn))
        decay.append(jnp.exp(b_last + m_prev[hd] - mn))
        m_new.append(mn)
    kv = [_dot((vt_ref[hd] * wk_r[hd]).astype(BF16), kb[hd]) for hd in heads]
    for hd in heads:
        num = sv[hd] + a[hd] * qc[hd]
        den = jnp.sum(s[hd], axis=1, keepdims=True) + a[hd] * jnp.sum(q[hd] * n_old[hd], axis=1, keepdims=True)
        h_ref[hd] = num / jnp.maximum(jnp.abs(den), jnp.exp(-mt[hd]))
    for hd in heads:
        c_ref[hd] = decay[hd] * c_old[hd] + kv[hd]
        n_ref[hd] = decay[hd] * n_old[hd] + jnp.sum(kf[hd] * wk_c[hd], axis=0, keepdims=True)
        m_ref[hd] = m_new[hd]


def _mlstm(q, k, v, vt, gc, gr, c0, n0, m0, *, L, t_valid):
    h4, b, t, dh = q.shape
    nc = t // L
    tok = pl.BlockSpec((h4, None, L, dh), lambda i, c: (0, i, c, 0))
    if vt.ndim == 4:
        vt_spec = pl.BlockSpec((h4, None, dh, L), lambda i, c: (0, i, 0, c))
        gr_spec = pl.BlockSpec((None, 2 * h4, L), lambda i, c: (i, 0, c))
    else:
        vt_spec = pl.BlockSpec((h4, dh, L), lambda i, c: (0, 0, i * nc + c))
        gr_spec = pl.BlockSpec((2 * h4, L), lambda i, c: (0, i * nc + c))
    st = lambda r, w: pl.BlockSpec((None, h4, r, w), lambda i, c: (i, 0, 0, 0))
    return pl.pallas_call(
        functools.partial(_mlstm_body, L=L, t_valid=t_valid), grid=(b, nc),
        in_specs=[tok, tok, tok, vt_spec, pl.BlockSpec((None, L, 2 * h4), lambda i, c: (i, c, 0)), gr_spec,
                  st(dh, dh), st(1, dh), st(1, 1)],
        out_specs=[tok, st(dh, dh), st(1, dh), st(1, 1)],
        out_shape=[jax.ShapeDtypeStruct((h4, b, t, dh), F32), jax.ShapeDtypeStruct((b, h4, dh, dh), F32),
                   jax.ShapeDtypeStruct((b, h4, 1, dh), F32), jax.ShapeDtypeStruct((b, h4, 1, 1), F32)],
        compiler_params=_params(("parallel", "arbitrary")), name="mlstm")(
            q, k, v, vt, gc, gr, c0, n0.reshape(b, h4, 1, dh), m0.reshape(b, h4, 1, 1))


def _even_out_body(x_ref, yp_ref, og_ref, hm_ref, w_ref, o_ref):
    gated = [(jax.nn.sigmoid(og_ref[h]) * hm_ref[h]).astype(BF16) for h in range(ML_HEADS)]
    mixed = jnp.concatenate([yp_ref[...].astype(BF16)] + gated, axis=1)
    o_ref[...] = x_ref[...] + _dot(mixed, w_ref[...])


def _even_out(x, ypool, og, hm, w_out):
    n, d = x.shape
    tm = 512 if n % 512 == 0 else n
    w = w_out.astype(BF16)
    row = lambda w: pl.BlockSpec((tm, w), lambda i: (i, 0))
    hrow = pl.BlockSpec((ML_HEADS, tm, ML_DH), lambda i: (0, i, 0))
    return pl.pallas_call(
        _even_out_body, grid=(n // tm,),
        in_specs=[row(d), row(POOL_WIDTH), hrow, hrow, _const_spec(w.shape)],
        out_specs=row(d), out_shape=jax.ShapeDtypeStruct((n, d), F32),
        compiler_params=_params(("parallel",)), name="even_out")(x, ypool, og, hm, w)


def _odd_in_body(x_ref, g_ref, wq_ref, wkvt_ref, wgt_ref, wc_ref,
                 q_ref, rows_ref, win_ref, gates_ref, ucv_ref, bg_ref):
    hn = _rms(x_ref[...], g_ref[...]).astype(BF16)
    q_ref[...] = _dot(hn, wq_ref[...]) * (LOG2E * HEAD_DIM ** -0.5)
    nrow = rows_ref.shape[0]
    rows_ref[...] = _dot_nt(wkvt_ref[:nrow], hn)
    win_ref[...] = _dot_nt(wkvt_ref[nrow:], hn)
    gates_ref[...] = jax.nn.sigmoid(_dot(hn, wgt_ref[...]))
    bg_ref[...] = _dot(hn, wc_ref[:, :CONV_CH])
    ucv_ref[...] = _dot(hn, wc_ref[:, CONV_CH:2 * CONV_CH]) * _dot(hn, wc_ref[:, 2 * CONV_CH:])


def _odd_in(x, g, w_in, seq):
    n, d = x.shape
    tm = 512 if seq % 512 == 0 else seq
    nt = seq // tm
    kvw = 6 * KV_HEADS * HEAD_DIM
    ngt = 3 * NSA_HEADS
    wq = w_in[:, :NSA_WIDTH].astype(BF16)
    wkvt = w_in[:, NSA_WIDTH:NSA_WIDTH + kvw].T.astype(BF16)
    wgt = w_in[:, NSA_WIDTH + kvw:NSA_WIDTH + kvw + ngt].astype(BF16)
    wc = w_in[:, NSA_WIDTH + kvw + ngt:].astype(BF16)
    nrow = 4 * KV_HEADS * HEAD_DIM
    row = lambda w: pl.BlockSpec((tm, w), lambda i: (i, 0))
    slab = lambda r: pl.BlockSpec((None, r, tm), lambda i: (i // nt, 0, i % nt))
    sh = lambda w: jax.ShapeDtypeStruct((n, w), F32)
    return pl.pallas_call(
        _odd_in_body, grid=(n // tm,),
        in_specs=[row(d), _const_spec((1, d)), _const_spec(wq.shape), _const_spec(wkvt.shape),
                  _const_spec(wgt.shape), _const_spec(wc.shape)],
        out_specs=[row(NSA_WIDTH),
                   slab(nrow), slab(kvw - nrow), row(ngt), row(CONV_CH), row(CONV_CH)],
        out_shape=[sh(NSA_WIDTH),
                   jax.ShapeDtypeStruct((n // seq, nrow, seq), F32),
                   jax.ShapeDtypeStruct((n // seq, kvw - nrow, seq), F32),
                   sh(ngt), sh(CONV_CH), sh(CONV_CH)],
        compiler_params=_params(("parallel",)), name="odd_in")(x, g.reshape(1, d), wq, wkvt, wgt, wc)


def _conv_body(u_ref, bg_ref, pre_ref, w_ref, y_ref, carry, full, *, tb):
    @pl.when(pl.program_id(1) == 0)
    def _():
        carry[...] = pre_ref[...]

    full[0:CONV_HALO] = carry[...]
    full[CONV_HALO:] = u_ref[...]
    f = full[...]
    w = w_ref[...]
    conv = f * w[2:3] + pltpu.roll(f, 1, 0) * w[1:2] + pltpu.roll(f, 2, 0) * w[0:1]
    y_ref[...] = (bg_ref[...] * conv[CONV_HALO:]).astype(y_ref.dtype)
    carry[...] = full[tb:tb + CONV_HALO]


def _conv(u, bg, prefix, conv_w):
    b, t, c = u.shape
    tb = 512 if t % 512 == 0 else t
    pre = jnp.pad(prefix, ((0, 0), (CONV_HALO - prefix.shape[1], 0), (0, 0)))
    blk = pl.BlockSpec((None, tb, c), lambda i, j: (i, j, 0))
    return pl.pallas_call(
        functools.partial(_conv_body, tb=tb), grid=(b, t // tb),
        in_specs=[blk, blk, pl.BlockSpec((None, CONV_HALO, c), lambda i, j: (i, 0, 0)), _const_spec((CONV_K, c))],
        out_specs=blk, out_shape=jax.ShapeDtypeStruct((b, t, c), BF16),
        scratch_shapes=[pltpu.VMEM((CONV_HALO, c), F32), pltpu.VMEM((tb + CONV_HALO, c), F32)],
        compiler_params=_params(("parallel", "arbitrary")), name="conv")(u, bg, pre, conv_w)


def _odd_out_body(x_ref, o_ref, yc_ref, w0_ref, w1_ref, out_ref):
    out_ref[...] = (x_ref[...] + _dot(o_ref[...].astype(BF16), w0_ref[...])
                    + _dot(yc_ref[...].astype(BF16), w1_ref[...]))


def _odd_out(x, o, yconv, w_out):
    n, d = x.shape
    tm = 512 if n % 512 == 0 else n
    w0 = w_out[:NSA_WIDTH].astype(BF16)
    w1 = w_out[NSA_WIDTH:].astype(BF16)
    row = lambda w: pl.BlockSpec((tm, w), lambda i: (i, 0))
    return pl.pallas_call(
        _odd_out_body, grid=(n // tm,),
        in_specs=[row(d), row(NSA_WIDTH), row(CONV_CH), _const_spec(w0.shape), _const_spec(w1.shape)],
        out_specs=row(d), out_shape=jax.ShapeDtypeStruct((n, d), F32),
        compiler_params=_params(("parallel",)), name="odd_out")(x, o, yconv, w0, w1)


CMP_TILE = 2048
CMP_ROWS = 4 * HEAD_DIM


def _cmp_weights(cmp_pos_w, cmp_w, cmp_b, tile):
    m = tile // CMP_STRIDE
    cols = -(-(m + 1) // 128) * 128
    pos = jnp.arange(tile)
    chunk = (pos // CMP_STRIDE)[:, None]
    col = jnp.arange(cols)[None, :]
    mats = []
    for kv in range(2):
        wa = cmp_pos_w[kv, :CMP_STRIDE][pos % CMP_STRIDE][:, None]
        wb = cmp_pos_w[kv, CMP_STRIDE:][pos % CMP_STRIDE][:, None]
        mats.append(jnp.where(chunk == col, wb, 0.0) + jnp.where(chunk == col - 1, wa, 0.0))
    w2 = jnp.stack(mats).astype(BF16)
    wbdt = jax.scipy.linalg.block_diag(cmp_w[0], cmp_w[0], cmp_w[1], cmp_w[1]).T.astype(BF16)
    biast = jnp.concatenate([cmp_b[0], cmp_b[0], cmp_b[1], cmp_b[1]]).reshape(-1, 1)
    return w2, wbdt, biast


def _compress_body(*refs, n_pages, n_prefetch):
    refs = refs[n_prefetch:]
    pages = refs[:n_pages]
    w2_ref, wbdt_ref, biast_ref, out_ref, carry = refs[n_pages:]
    m_out = out_ref.shape[1]

    @pl.when(pl.program_id(1) == 0)
    def _():
        carry[...] = jnp.zeros(carry.shape, F32)

    x = jnp.concatenate([pg[...] for pg in pages], axis=1).astype(BF16)
    half = CMP_ROWS // 2
    res = jnp.concatenate([_dot(x[:half], w2_ref[0]), _dot(x[half:], w2_ref[1])], axis=0)
    first = lax.broadcasted_iota(jnp.int32, (CMP_ROWS, m_out), 1) == 0
    pre = res[:, :m_out] + jnp.where(first, carry[...], 0.0)
    carry[...] = jnp.broadcast_to(res[:, m_out:m_out + 1], carry.shape)
    out_ref[...] = _dot(wbdt_ref[...], pre.astype(BF16)) + biast_ref[...]


def _compress_prompt(rows_t, cmp_pos_w, cmp_w, cmp_b):
    b, _, t = rows_t.shape
    tile = min(CMP_TILE, t)
    assert t % tile == 0
    w2, wbdt, biast = _cmp_weights(cmp_pos_w, cmp_w, cmp_b, tile)
    m_out = tile // CMP_STRIDE
    return pl.pallas_call(
        functools.partial(_compress_body, n_pages=1, n_prefetch=0), grid=(b, t // tile),
        in_specs=[pl.BlockSpec((None, CMP_ROWS, tile), lambda i, j: (i, 0, j)),
                  _const_spec(w2.shape), _const_spec(wbdt.shape), _const_spec(biast.shape)],
        out_specs=pl.BlockSpec((None, CMP_ROWS, m_out), lambda i, j: (i, 0, j)),
        out_shape=jax.ShapeDtypeStruct((b, CMP_ROWS, t // CMP_STRIDE), F32),
        scratch_shapes=[pltpu.VMEM((CMP_ROWS, m_out), F32)],
        compiler_params=_params(("parallel", "arbitrary")), name="compress_prompt")(rows_t, w2, wbdt, biast)


def _compress_paged(cache_t, page_table, cmp_pos_w, cmp_w, cmp_b):
    b, npg = page_table.shape
    pg = min(CMP_TILE // PAGE_SIZE, npg)
    assert npg % pg == 0
    tile = pg * PAGE_SIZE
    w2, wbdt, biast = _cmp_weights(cmp_pos_w, cmp_w, cmp_b, tile)
    m_out = tile // CMP_STRIDE
    specs = [pl.BlockSpec((None, CMP_ROWS, PAGE_SIZE),
                          functools.partial(lambda i, j, pt, p: (pt[i, j * pg + p], 0, 0), p=p))
             for p in range(pg)]
    return pl.pallas_call(
        functools.partial(_compress_body, n_pages=pg, n_prefetch=1),
        grid_spec=pltpu.PrefetchScalarGridSpec(
            num_scalar_prefetch=1, grid=(b, npg // pg),
            in_specs=specs + [_const_spec(w2.shape), _const_spec(wbdt.shape), _const_spec(biast.shape)],
            out_specs=pl.BlockSpec((None, CMP_ROWS, m_out), lambda i, j, pt: (i, 0, j)),
            scratch_shapes=[pltpu.VMEM((CMP_ROWS, m_out), F32)]),
        out_shape=jax.ShapeDtypeStruct((b, CMP_ROWS, npg * PAGE_SIZE // CMP_STRIDE), F32),
        compiler_params=_params(("parallel", "arbitrary")), name="compress_paged")(
            page_table, *([cache_t] * pg), w2, wbdt, biast)


def _alibi_slopes(rows_per_head):
    sl = (LOG2E * 2.0 ** (-8.0 * np.arange(1, NSA_HEADS + 1) / NSA_HEADS)).astype(np.float32).reshape(KV_HEADS, GQ)
    return jnp.asarray(np.repeat(sl, rows_per_head, axis=1)[:, :, None])


def _score_matrix(n_cmp_rows, n_blocks, lanes):
    m = np.arange(n_cmp_rows)[:, None]
    j = np.arange(lanes)[None, :]
    return jnp.asarray(((m >= 4 * j) & (m <= 4 * j + 4) & (m >= 1) & (j < n_blocks)).astype(np.float32))


def _group_queries(q_ref, g):
    heads = [q_ref[:, (g * GQ + i) * HEAD_DIM:(g * GQ + i + 1) * HEAD_DIM] for i in range(GQ)]
    return jnp.concatenate(heads, axis=0).astype(BF16)


def _softmax_rows(s, mask):
    s = jnp.where(mask, s, -jnp.inf)
    m = jnp.max(s, axis=-1, keepdims=True)
    m = jnp.where(m > -jnp.inf, m, 0.0)
    e = jnp.exp2(s - m)
    return e * (1.0 / jnp.maximum(jnp.sum(e, axis=-1, keepdims=True), 1e-30))


def _select_blocks(score, cur, n_blocks):
    r, lanes = score.shape
    blk = lax.broadcasted_iota(jnp.int32, (r, lanes), 1)
    forced = (blk == 0) | (blk == cur) | (blk == cur - 1)
    val = jnp.where(forced, jnp.inf, jnp.where(blk <= cur, score, -jnp.inf))
    rank = jnp.zeros((r, lanes), jnp.int32)
    for i in range(n_blocks):
        ci = val[:, i:i + 1]
        ahead = (ci > val) | ((ci == val) & (blk > i))
        rank = rank + ahead.astype(jnp.int32)
    return ((rank < min(SEL_TOPN, n_blocks)) & (blk < n_blocks)).astype(F32)


POS_ROWS = 16


def _slope_pieces(rows_per_head):
    sl = _alibi_slopes(rows_per_head)
    hi = sl.astype(BF16)
    mid = (sl - hi.astype(F32)).astype(BF16)
    lo = (sl - hi.astype(F32) - mid.astype(F32)).astype(BF16)
    pad = jnp.zeros(sl.shape[:2] + (POS_ROWS - 6,), BF16)
    return jnp.concatenate([hi, mid, lo, hi, mid, lo, pad], axis=2)


def _pos_rows(t_len, stride=1, offset=0):
    k = stride * np.arange(t_len) + offset
    hi = (SEL_BLOCK * (k // SEL_BLOCK)).astype(np.float32)
    lo = (k % SEL_BLOCK).astype(np.float32)
    return np.stack([hi, hi, hi, lo, lo, lo] + [np.zeros(t_len, np.float32)] * (POS_ROWS - 6))


def _block_rows(n_rows, n_keys):
    return (np.arange(n_rows)[:, None] == (np.arange(n_keys) // SEL_BLOCK)[None, :]).astype(np.float32)


def _key_constants(t_len):
    both = np.concatenate([_pos_rows(t_len), _block_rows(t_len // SEL_BLOCK, t_len)], axis=0)
    return jnp.asarray(both).astype(BF16)


def _select_bias_t(score_t, cur):
    n_blocks, r = score_t.shape
    blk = lax.broadcasted_iota(jnp.int32, (n_blocks, r), 0)
    forced = (blk == 0) | (blk == cur) | (blk == cur - 1)
    val = jnp.where(forced, jnp.inf, jnp.where(blk <= cur, score_t, -jnp.inf))

    rank = jnp.zeros((n_blocks, r), F32)
    for i in range(n_blocks):
        ci = val[i:i + 1, :]
        ahead = (ci > val) | ((ci == val) & (blk > i))
        rank = rank + jnp.where(ahead, 1.0, 0.0)
    return jnp.where((rank < min(SEL_TOPN, n_blocks)) & (blk <= cur), 0.0, NEG)


def _nsa_prompt_body(q_ref, gates_ref, kvc_ref, sel_ref, win_ref, saug_ref, kconst_ref, cconst_ref,
                     smat_ref, o_ref, need_sc, *, t_len, kc, wb):
    bi = pl.program_id(1)
    p0 = bi * Q_BLOCK
    nrow = GQ * Q_BLOCK
    n_cmp = kvc_ref.shape[1]
    n_blocks = t_len // SEL_BLOCK
    qpos = lax.broadcasted_iota(jnp.int32, (Q_BLOCK, 1), 0) + p0
    tpos = jnp.concatenate([qpos] * GQ, axis=0)
    gates = gates_ref[...]

    qgs = [_group_queries(q_ref, g) for g in range(KV_HEADS)]
    q_pos = [jnp.concatenate([qgs[g], saug_ref[g]], axis=1) for g in range(KV_HEADS)]

    cpos = CMP_STRIDE * lax.broadcasted_iota(jnp.int32, (1, n_cmp), 1) + (CMP_STRIDE - 1)
    dist = tpos - cpos
    cmask = (dist >= 0) & (cpos >= 2 * CMP_STRIDE - 1)

    def compressed_scores(g):
        ks = g * HEAD_DIM
        katc = jnp.concatenate([kvc_ref[ks:ks + HEAD_DIM, :].astype(BF16), cconst_ref[...]], axis=0)
        return _dot(q_pos[g], katc)

    wstart = pl.multiple_of(jnp.clip((p0 - WINDOW) // 128 * 128, 0, t_len - wb), 128)
    wpos = wstart + lax.broadcasted_iota(jnp.int32, (1, wb), 1)
    wmask = (wpos <= tpos) & (wpos > tpos - WINDOW)

    def window_scores(g):
        katw = jnp.concatenate([win_ref[g * HEAD_DIM:(g + 1) * HEAD_DIM, pl.ds(wstart, wb)].astype(BF16),
                                kconst_ref[0:POS_ROWS, pl.ds(wstart, wb)]], axis=0)
        return _dot(q_pos[g], katw)

    def window_output(g, e_win):
        vs = (KV_HEADS + g) * HEAD_DIM
        vwt = jnp.concatenate([win_ref[vs:vs + HEAD_DIM, pl.ds(wstart, wb)].astype(BF16),
                               jnp.ones((POS_ROWS, wb), BF16)], axis=0)
        acc_w = _dot_nt(e_win, vwt)
        return acc_w[:, :HEAD_DIM] / acc_w[:, HEAD_DIM:HEAD_DIM + 1]

    s_cmp = [compressed_scores(g) for g in range(KV_HEADS)]
    s_win = [window_scores(g) for g in range(KV_HEADS)]
    p_cmps = [_softmax_rows(s_cmp[g], cmask) for g in range(KV_HEADS)]
    e_wins = []
    for g in range(KV_HEADS):
        sw = jnp.where(wmask, s_win[g], -jnp.inf)
        e_wins.append(jnp.exp2(sw - jnp.max(sw, axis=1, keepdims=True)).astype(BF16))
    o_cmps = [_dot_nt(p_cmps[g].astype(BF16), kvc_ref[(KV_HEADS + g) * HEAD_DIM:(KV_HEADS + g + 1) * HEAD_DIM, :]
                      .astype(BF16)) for g in range(KV_HEADS)]
    imps = [jnp.sum(p_cmps[g].reshape(GQ, Q_BLOCK, n_cmp), axis=0) for g in range(KV_HEADS)]
    o_wins = [window_output(g, e_wins[g]) for g in range(KV_HEADS)]
    score_t = _dot_nt(smat_ref[...], _split3(jnp.concatenate(imps, axis=0), 1))
    bias_f = _select_bias_t(score_t, bi)
    bias_qb = bias_f.T
    biases = [bias_qb[g * Q_BLOCK:(g + 1) * Q_BLOCK] for g in range(KV_HEADS)]
    q_aug = [jnp.concatenate([q_pos[g], jnp.concatenate([biases[g]] * GQ, axis=0).astype(BF16)], axis=1)
             for g in range(KV_HEADS)]
    for hh in range(NSA_HEADS):
        g, r = hh // GQ, slice((hh % GQ) * Q_BLOCK, (hh % GQ + 1) * Q_BLOCK)
        o_ref[:, hh * HEAD_DIM:(hh + 1) * HEAD_DIM] = (
            gates[:, 3 * hh:3 * hh + 1] * o_cmps[g][r] + gates[:, 3 * hh + 2:3 * hh + 3] * o_wins[g][r])

    ones_rows = jnp.ones((POS_ROWS, kc), BF16)

    def chunk_scores(g, c):
        k0 = pl.multiple_of(c * kc, kc)
        kat = jnp.concatenate([sel_ref[g * HEAD_DIM:(g + 1) * HEAD_DIM, pl.ds(k0, kc)].astype(BF16),
                               kconst_ref[:, pl.ds(k0, kc)]], axis=0)
        return _dot(q_aug[g], kat), k0

    def probabilities(s, m_i):
        m_new = jnp.maximum(m_i, jnp.max(s, axis=1, keepdims=True))
        return m_new, jnp.exp2(s - m_new).astype(BF16), jnp.exp2(m_i - m_new)

    def weighted_values(g, pr, k0):
        vs = (KV_HEADS + g) * HEAD_DIM
        vt = jnp.concatenate([sel_ref[vs:vs + HEAD_DIM, pl.ds(k0, kc)].astype(BF16), ones_rows], axis=0)
        return _dot_nt(pr, vt)

    def chunk_update(c, carry, mask=None):
        scores = [chunk_scores(g, c) for g in range(KV_HEADS)]
        if mask is not None:
            scores = [(jnp.where(mask(k0), s, NEG), k0) for s, k0 in scores]
        probs = [probabilities(scores[g][0], carry[g][0]) for g in range(KV_HEADS)]
        out = []
        for g in range(KV_HEADS):
            m_new, pr, alpha = probs[g]
            out.append((m_new, alpha * carry[g][1] + weighted_values(g, pr, scores[g][1])))
        return tuple(out)

    last = (p0 + Q_BLOCK - 1) // kc
    blocks_per_chunk = kc // SEL_BLOCK
    n_need = jnp.int32(0)
    for c in range(t_len // kc - 1):
        picked = bias_f[c * blocks_per_chunk:(c + 1) * blocks_per_chunk, :] == 0.0
        need = jnp.where(c < last, jnp.max(jnp.where(picked, 1, 0)), 0)
        need_sc[n_need] = c
        n_need = n_need + need

    one = (jnp.full((nrow, 1), NEG, F32), jnp.zeros((nrow, HEAD_DIM + POS_ROWS), F32))
    carry = lax.fori_loop(0, n_need, lambda i, carry: chunk_update(need_sc[i], carry), (one,) * KV_HEADS)
    causal = lambda k0: k0 + lax.broadcasted_iota(jnp.int32, (1, kc), 1) <= tpos
    carry = chunk_update(last, carry, mask=causal)
    for g in range(KV_HEADS):
        acc_s = carry[g][1]
        o_sel = acc_s[:, :HEAD_DIM] / jnp.maximum(acc_s[:, HEAD_DIM:HEAD_DIM + 1], 1e-30)
        for i in range(GQ):
            hh = g * GQ + i
            r = slice(i * Q_BLOCK, (i + 1) * Q_BLOCK)
            o_ref[:, hh * HEAD_DIM:(hh + 1) * HEAD_DIM] += gates[:, 3 * hh + 1:3 * hh + 2] * o_sel[r]


def _nsa_prompt(q, gates, kvc_t, rows_t, win_t):
    b, t, _ = q.shape
    assert t % 128 == 0
    kc = 512 if t % 512 == 0 else t
    wb = min(WINDOW + 128, t)
    n_cmp = kvc_t.shape[2]
    n_blocks = t // SEL_BLOCK
    ncol = 4 * HEAD_DIM
    saug = _slope_pieces(Q_BLOCK)
    kconst = _key_constants(t)
    cconst = jnp.asarray(_pos_rows(n_cmp, CMP_STRIDE, CMP_STRIDE - 1)).astype(BF16)
    smat = jnp.tile(_score_matrix(n_cmp, n_blocks, n_blocks).T, (1, 3)).astype(BF16)
    return pl.pallas_call(
        functools.partial(_nsa_prompt_body, t_len=t, kc=kc, wb=wb), grid=(b, t // Q_BLOCK),
        in_specs=[pl.BlockSpec((None, Q_BLOCK, NSA_WIDTH), lambda i, j: (i, j, 0)),
                  pl.BlockSpec((None, Q_BLOCK, 3 * NSA_HEADS), lambda i, j: (i, j, 0)),
                  pl.BlockSpec((None, ncol, n_cmp), lambda i, j: (i, 0, 0)),
                  pl.BlockSpec((None, ncol, t), lambda i, j: (i, 1, 0)),
                  pl.BlockSpec((None, ncol, t), lambda i, j: (i, 0, 0)),
                  _const_spec(saug.shape), _const_spec(kconst.shape), _const_spec(cconst.shape),
                  _const_spec(smat.shape)],
        out_specs=pl.BlockSpec((None, Q_BLOCK, NSA_WIDTH), lambda i, j: (i, j, 0)),
        out_shape=jax.ShapeDtypeStruct((b, t, NSA_WIDTH), F32),
        scratch_shapes=[pltpu.SMEM((t // kc,), jnp.int32)],
        compiler_params=_params(("parallel", "arbitrary")), name="nsa_prompt")(
            q, gates, kvc_t, rows_t, win_t, saug, kconst, cconst, smat)


def _nsa_dec_select_body(q_ref, kvc_ref, slope_ref, smat_ref, ocmp_ref, sel_ref, *, past_len, tq, n_blocks,
                         step_blocks):
    n_cmp = kvc_ref.shape[1]
    n_steps = sel_ref.shape[1] - 1
    lanes_out = sel_ref.shape[3]
    nrow = GQ * tq
    qpos = lax.broadcasted_iota(jnp.int32, (tq, 1), 0) + past_len
    tpos = jnp.concatenate([qpos] * GQ, axis=0)
    cpos = CMP_STRIDE * lax.broadcasted_iota(jnp.int32, (1, n_cmp), 1) + (CMP_STRIDE - 1)
    dist = tpos - cpos
    for g in range(KV_HEADS):
        qg = _group_queries(q_ref, g)
        ks, vs = g * HEAD_DIM, (KV_HEADS + g) * HEAD_DIM
        s = _dot(qg, kvc_ref[ks:ks + HEAD_DIM, :].astype(BF16)) - slope_ref[g] * dist.astype(F32)
        p_cmp = _softmax_rows(s, (dist >= 0) & (cpos >= 2 * CMP_STRIDE - 1))
        ocmp_ref[g] = _dot_nt(p_cmp.astype(BF16), kvc_ref[vs:vs + HEAD_DIM, :].astype(BF16))
        imp = jnp.sum(p_cmp.reshape(GQ, tq, n_cmp), axis=0)
        score = _dot(_split3(imp, 1), smat_ref[...])
        bias = jnp.where(_select_blocks(score, qpos // SEL_BLOCK, n_blocks) > 0.5, 0.0, NEG)
        pad = jnp.zeros((tq, lanes_out - step_blocks), F32)
        for st in range(n_steps):
            sel_ref[g, st] = jnp.concatenate([bias[:, st * step_blocks:(st + 1) * step_blocks], pad], axis=1)
        last_blk = n_steps * step_blocks
        sel_ref[g, n_steps] = jnp.concatenate([bias[:, last_blk:last_blk + 1],
                                               jnp.zeros((tq, lanes_out - 1), F32)], axis=1)


def _nsa_dec_select(q, kvc, past_len, n_blocks, step_blocks):
    b, tq, _ = q.shape
    n_cmp = kvc.shape[2]
    lanes = -(-n_blocks // 128) * 128
    n_steps = (n_blocks - 1) // step_blocks
    assert n_steps * step_blocks == n_blocks - 1 and step_blocks <= 128
    slopes = _alibi_slopes(tq)
    smat = jnp.tile(_score_matrix(n_cmp, n_blocks, lanes), (3, 1)).astype(BF16)
    return pl.pallas_call(
        functools.partial(_nsa_dec_select_body, past_len=past_len, tq=tq, n_blocks=n_blocks,
                          step_blocks=step_blocks), grid=(b,),
        in_specs=[pl.BlockSpec((None, tq, NSA_WIDTH), lambda i: (i, 0, 0)),
                  pl.BlockSpec((None, 4 * HEAD_DIM, n_cmp), lambda i: (i, 0, 0)),
                  _const_spec(slopes.shape), _const_spec(smat.shape)],
        out_specs=[pl.BlockSpec((None, KV_HEADS, GQ * tq, HEAD_DIM), lambda i: (i, 0, 0, 0)),
                   pl.BlockSpec((None, KV_HEADS, n_steps + 1, tq, 128), lambda i: (i, 0, 0, 0, 0))],
        out_shape=[jax.ShapeDtypeStruct((b, KV_HEADS, GQ * tq, HEAD_DIM), F32),
                   jax.ShapeDtypeStruct((b, KV_HEADS, n_steps + 1, tq, 128), F32)],
        compiler_params=_params(("parallel",)), name="nsa_dec_select")(q, kvc, slopes, smat)


def _nsa_dec_attend_body(pt_ref, q_ref, selb_ref, ocmp_ref, gates_ref, newsel_ref, winbuf_ref, newwin_ref,
                         slope_ref, saug_ref, pos_ref, blk_ref, *rest, past_len, tq, t_valid, n_pages):
    pages = rest[:n_pages]
    o_ref, m_sc, acc_sc = rest[n_pages:]
    step = pl.program_id(1)
    nrow = GQ * tq
    qpos = lax.broadcasted_iota(jnp.int32, (tq, 1), 0) + past_len
    tpos = jnp.concatenate([qpos] * GQ, axis=0)

    @pl.when(step == 0)
    def _():
        m_sc[...] = jnp.full(m_sc.shape, NEG, F32)
        acc_sc[...] = jnp.zeros(acc_sc.shape, F32)

    def update(g, sc, weighted_values):
        m_i = m_sc[g]
        m_new = jnp.maximum(m_i, jnp.max(sc, axis=1, keepdims=True))
        pr = jnp.exp2(sc - m_new)
        acc_sc[g] = jnp.exp2(m_i - m_new) * acc_sc[g] + weighted_values(pr.astype(BF16))
        m_sc[g] = m_new

    nk = n_pages * PAGE_SIZE
    ones_rows = jnp.ones((POS_ROWS, nk), BF16)
    scores = []
    for g in range(KV_HEADS):
        qg = _group_queries(q_ref, g)
        ks = g * HEAD_DIM
        bias = jnp.concatenate([selb_ref[g, step]] * GQ, axis=0).astype(BF16)
        q_aug = jnp.concatenate([qg, saug_ref[g], bias], axis=1)
        kat = jnp.concatenate([jnp.concatenate([pg[ks:ks + HEAD_DIM, :] for pg in pages], axis=1).astype(BF16),
                               pos_ref[...], blk_ref[...]], axis=0)
        scores.append(_dot(q_aug, kat))
    for g in range(KV_HEADS):
        vs = (KV_HEADS + g) * HEAD_DIM
        vt = jnp.concatenate([jnp.concatenate([pg[vs:vs + HEAD_DIM, :] for pg in pages], axis=1).astype(BF16),
                              ones_rows], axis=0)
        update(g, scores[g], lambda p, vt=vt: _dot_nt(p, vt))

    @pl.when(step == pl.num_programs(1) - 1)
    def _():
        gates = gates_ref[...]
        tn = newsel_ref.shape[0]
        npos = past_len + lax.broadcasted_iota(jnp.int32, (1, tn), 1)
        nd = tpos - npos
        valid_new = (nd >= 0) & (npos < past_len + t_valid)
        wlen = winbuf_ref.shape[1]
        wpos = past_len - wlen + lax.broadcasted_iota(jnp.int32, (1, wlen), 1)
        wd = tpos - wpos
        for g in range(KV_HEADS):
            qg = _group_queries(q_ref, g)
            ks, vs = g * HEAD_DIM, (KV_HEADS + g) * HEAD_DIM
            slope = slope_ref[g]
            block_bias = jnp.concatenate([selb_ref[g, selb_ref.shape[1] - 1][:, 0:1]] * GQ, axis=0)
            sc = (_dot_nt(qg, newsel_ref[:, ks:ks + HEAD_DIM].astype(BF16)) + slope * npos.astype(F32)
                  + block_bias)
            new_v = jnp.concatenate([newsel_ref[:, vs:vs + HEAD_DIM].astype(BF16),
                                     jnp.ones((tn, POS_ROWS), BF16)], axis=1)
            update(g, jnp.where(valid_new, sc, NEG), lambda p, new_v=new_v: _dot(p, new_v))
            acc = acc_sc[g]
            o_sel = acc[:, :HEAD_DIM] / jnp.maximum(acc[:, HEAD_DIM:HEAD_DIM + 1], 1e-30)
            s1 = _dot(qg, winbuf_ref[ks:ks + HEAD_DIM, :].astype(BF16)) - slope * wd.astype(F32)
            s2 = _dot_nt(qg, newwin_ref[:, ks:ks + HEAD_DIM].astype(BF16)) - slope * nd.astype(F32)
            ok1 = (wd >= 0) & (wd < WINDOW) & (wpos >= 0)
            ok2 = valid_new & (nd < WINDOW)
            s1 = jnp.where(ok1, s1, -jnp.inf)
            s2 = jnp.where(ok2, s2, -jnp.inf)
            mx = jnp.maximum(jnp.max(s1, axis=1, keepdims=True), jnp.max(s2, axis=1, keepdims=True))
            mx = jnp.where(mx > -jnp.inf, mx, 0.0)
            e1 = jnp.exp2(s1 - mx)
            e2 = jnp.exp2(s2 - mx)
            den = jnp.maximum(jnp.sum(e1, axis=1, keepdims=True) + jnp.sum(e2, axis=1, keepdims=True), 1e-30)
            o_win = (_dot_nt(e1.astype(BF16), winbuf_ref[vs:vs + HEAD_DIM, :].astype(BF16))
                     + _dot(e2.astype(BF16), newwin_ref[:, vs:vs + HEAD_DIM].astype(BF16))) / den
            o_cmp = ocmp_ref[g]
            for i in range(GQ):
                hh = g * GQ + i
                r = slice(i * tq, (i + 1) * tq)
                o_ref[:, hh * HEAD_DIM:(hh + 1) * HEAD_DIM] = (
                    gates[:, 3 * hh:3 * hh + 1] * o_cmp[r] + gates[:, 3 * hh + 1:3 * hh + 2] * o_sel[r]
                    + gates[:, 3 * hh + 2:3 * hh + 3] * o_win[r])


DEC_PAGES = 32


def _nsa_dec_attend(q, selbias, o_cmp, gates, newsel, winbuf, newwin, cache, page_table, past_len, t_valid):
    b, tq, _ = q.shape
    npg = page_table.shape[1]
    pg = min(DEC_PAGES, npg)
    assert npg % pg == 0 and selbias.shape[2] == npg // pg + 1
    nk = pg * PAGE_SIZE
    ncol = 4 * HEAD_DIM
    slopes = _alibi_slopes(tq)
    saug = _slope_pieces(tq)
    pos_rows = jnp.asarray(_pos_rows(past_len)).astype(BF16)
    blk_rows = jnp.asarray(_block_rows(selbias.shape[-1], nk)).astype(BF16)
    wlen = winbuf.shape[2]
    bspec = lambda shape: pl.BlockSpec((None,) + shape, lambda i, j, pt: (i,) + (0,) * len(shape))
    page_specs = [pl.BlockSpec((None, ncol, PAGE_SIZE),
                               functools.partial(lambda i, j, pt, p: (pt[i, j * pg + p], 1, 0), p=p))
                  for p in range(pg)]
    return pl.pallas_call(
        functools.partial(_nsa_dec_attend_body, past_len=past_len, tq=tq, t_valid=t_valid, n_pages=pg),
        grid_spec=pltpu.PrefetchScalarGridSpec(
            num_scalar_prefetch=1, grid=(b, npg // pg),
            in_specs=[pl.BlockSpec((None, tq, NSA_WIDTH), lambda i, j, pt: (i, 0, 0)),
                      bspec(selbias.shape[1:]), bspec((KV_HEADS, GQ * tq, HEAD_DIM)),
                      bspec((tq, 3 * NSA_HEADS)), bspec((tq, ncol)), bspec((ncol, wlen)), bspec((tq, ncol)),
                      _const_spec(slopes.shape), _const_spec(saug.shape),
                      pl.BlockSpec((POS_ROWS, nk), lambda i, j, pt: (0, j)), _const_spec(blk_rows.shape)]
            + page_specs,
            out_specs=bspec((tq, NSA_WIDTH)),
            scratch_shapes=[pltpu.VMEM((KV_HEADS, GQ * tq, 1), F32),
                            pltpu.VMEM((KV_HEADS, GQ * tq, HEAD_DIM + POS_ROWS), F32)]),
        out_shape=jax.ShapeDtypeStruct((b, tq, NSA_WIDTH), F32),
        compiler_params=_params(("parallel", "arbitrary")), name="nsa_dec_attend")(
            page_table, q, selbias, o_cmp, gates, newsel, winbuf, newwin, slopes, saug, pos_rows, blk_rows,
            *([cache] * pg))


def _pad_t(a, axis, to):
    pad = [(0, 0)] * a.ndim
    pad[axis] = (0, to - a.shape[axis])
    return jnp.pad(a, pad)


def _even_layer(x, b, t, pos0, pool_buf, c0, n0, m0, g, w_in, b_gate, w_pool, pool_scale, w_out):
    n = b * t
    u, q, k, v, og, vt, gc, gr = _even_in(x, g, w_in, b_gate)
    u3 = u.reshape(b, t, POOL_WIDTH)
    new_pool = jnp.concatenate([pool_buf, u3], axis=1)[:, -pool_buf.shape[1]:]
    h4 = ML_HEADS
    if t % 256 == 0:
        tp, L = t, 256
        vt_in, gr_in = vt, gr
    else:
        tp = L = -(-t // 8) * 8
        vt_in = _pad_t(vt.reshape(h4, ML_DH, b, t), 3, tp).transpose(0, 2, 1, 3)
        gr_in = _pad_t(gr.reshape(2 * h4, b, t), 2, tp).transpose(1, 0, 2)
    seq = lambda a: _pad_t(a.reshape(h4, b, t, ML_DH), 2, tp)
    ypool = _pool(_pad_t(u3, 1, tp), pool_buf, w_pool, pool_scale, pos0)[:, :t].reshape(n, POOL_WIDTH)
    hm, c, nn, m = _mlstm(seq(q), seq(k), seq(v), vt_in, _pad_t(gc.reshape(b, t, 2 * h4), 1, tp), gr_in,
                          c0, n0, m0, L=L, t_valid=t)
    hm = hm[:, :, :t].reshape(h4, n, ML_DH)
    x = _even_out(x, ypool, og, hm, w_out)
    return x, new_pool, c, nn.reshape(b, h4, ML_DH), m.reshape(b, h4)


def _odd_layer(x, b, t, pos0, cache, page_table, win_buf, conv_buf, g, w_in, cmp_pos_w, cmp_w, cmp_b,
               conv_w, w_out):
    n = b * t
    ncol = 4 * HEAD_DIM
    prompt = cache is None
    q, rows_t, win_t, gates, ucv, bg = _odd_in(x, g, w_in, t if prompt else n)
    u3 = ucv.reshape(b, t, CONV_CH)
    new_conv = jnp.concatenate([conv_buf, u3], axis=1)[:, -(CONV_K - 1):]
    if prompt:
        kvc_t = _compress_prompt(rows_t, cmp_pos_w, cmp_w, cmp_b)
        o = _nsa_prompt(q.reshape(b, t, NSA_WIDTH), gates.reshape(b, t, 3 * NSA_HEADS), kvc_t, rows_t, win_t)
        new_win_t = win_t[:, :, t - min(WINDOW, t):]
        new_rows = rows_t.reshape(b, 4, KV_HEADS, HEAD_DIM, t).transpose(0, 4, 1, 2, 3)
        tp = t
    else:
        assert pos0 % SEL_BLOCK + t <= SEL_BLOCK
        tp = -(-t // 8) * 8
        rows_bt = rows_t[0].T.reshape(b, t, 2 * ncol)
        win_bt = win_t[0].T.reshape(b, t, ncol)
        n_pool = cache.shape[0]
        cache_t = cache.transpose(0, 2, 3, 4, 1).reshape(n_pool, 2 * ncol, PAGE_SIZE)
        kvc_t = _compress_paged(cache_t, page_table, cmp_pos_w, cmp_w, cmp_b)
        qp = _pad_t(q.reshape(b, t, NSA_WIDTH), 1, tp)
        n_blocks = -(-(pos0 + t) // SEL_BLOCK)
        step_blocks = min(DEC_PAGES, page_table.shape[1]) * PAGE_SIZE // SEL_BLOCK
        o_cmp, selmask = _nsa_dec_select(qp, kvc_t, pos0, n_blocks, step_blocks)
        wlen = win_buf.shape[1]
        winb_t = win_buf.transpose(0, 2, 3, 4, 1).reshape(b, ncol, wlen)
        o = _nsa_dec_attend(qp, selmask, o_cmp, _pad_t(gates.reshape(b, t, -1), 1, tp),
                            _pad_t(rows_bt[:, :, ncol:], 1, tp), winb_t, _pad_t(win_bt, 1, tp),
                            cache_t, page_table, pos0, t)[:, :t]
        new_win_t = jnp.concatenate([winb_t, win_bt.transpose(0, 2, 1)], axis=2)[:, :, -wlen:]
        new_rows = rows_bt.reshape(b, t, 4, KV_HEADS, HEAD_DIM)
    yconv = _conv(_pad_t(u3, 1, tp), _pad_t(bg.reshape(b, t, CONV_CH), 1, tp), conv_buf, conv_w)[:, :t]
    x = _odd_out(x, o.reshape(n, NSA_WIDTH), yconv.reshape(n, CONV_CH), w_out)
    new_win = new_win_t.reshape(b, 2, KV_HEADS, HEAD_DIM, new_win_t.shape[2]).transpose(0, 4, 1, 2, 3)
    return x, new_rows, new_win, new_conv


def _trunk(x3, pos0, pool_buf, ml_c, ml_n, ml_m, kv_cache, page_table, win_buf, conv_buf, p):
    b, t, d = x3.shape
    depth = p["norm_g"].shape[0]
    x = x3.reshape(b * t, d)
    pools, cs, ns, ms, rows, wins, convs = [], [], [], [], [], [], []
    for l in range(depth):
        j = l // 2
        x = _ffn(x, p["norm_g"][l, 0], p["w_ffn_in"][l][0], p["w_ffn_out"][l][0])
        if l % 2 == 0:
            x, pb, c, n, m = _even_layer(x, b, t, pos0, pool_buf[j], ml_c[j], ml_n[j], ml_m[j], p["norm_g"][l, 1],
                                         p["w_in_even"][j], p["b_gate_even"][j], p["w_pool"][j],
                                         p["pool_scale"][j], p["w_out_even"][j])
            pools.append(pb)
            cs.append(c)
            ns.append(n)
            ms.append(m)
        else:
            cache = None if kv_cache is None else kv_cache[j]
            wb = None if win_buf is None else win_buf[j]
            x, r, wn, cn = _odd_layer(x, b, t, pos0, cache, page_table, wb, conv_buf[j], p["norm_g"][l, 1],
                                      p["w_in_odd"][j], p["cmp_pos_w"][j], p["cmp_w"][j], p["cmp_b"][j],
                                      p["conv_w"][j], p["w_out_odd"][j])
            rows.append(r)
            wins.append(wn)
            convs.append(cn)
        x = _ffn(x, p["norm_g"][l, 2], p["w_ffn_in"][l][1], p["w_ffn_out"][l][1],
                 final_g=p["final_g"] if l == depth - 1 else None)
    states = (jnp.stack(pools), jnp.stack(cs), jnp.stack(ns), jnp.stack(ms),
              jnp.stack(rows), jnp.stack(wins), jnp.stack(convs))
    return x.reshape(b, t, d), states


def kernel(x_prompt, x_sample, state_pool, state_mlstm_c, state_mlstm_n, state_mlstm_m, cache_nsa_kv, state_win_kv, state_conv, page_table, norm_g, final_g, w_ffn_in, w_ffn_out, w_in_even, b_gate_even, w_pool, pool_scale, w_out_even, w_in_odd, cmp_pos_w, cmp_w, cmp_b, conv_w, w_out_odd):
    bp = x_prompt.shape[0]
    n_even, n_odd = state_pool.shape[0], state_conv.shape[0]
    past_len = page_table.shape[1] * PAGE_SIZE
    p = dict(norm_g=norm_g, final_g=final_g, w_ffn_in=w_ffn_in.astype(BF16), w_ffn_out=w_ffn_out.astype(BF16),
             w_in_even=w_in_even, b_gate_even=b_gate_even, w_pool=w_pool, pool_scale=pool_scale,
             w_out_even=w_out_even, w_in_odd=w_in_odd, cmp_pos_w=cmp_pos_w, cmp_w=cmp_w, cmp_b=cmp_b,
             conv_w=conv_w, w_out_odd=w_out_odd)
    pool0 = jnp.zeros((n_even, bp) + state_pool.shape[2:], F32)
    c0 = jnp.zeros((n_even, bp) + state_mlstm_c.shape[2:], F32)
    n0 = jnp.zeros((n_even, bp) + state_mlstm_n.shape[2:], F32)
    m0 = jnp.zeros((n_even, bp) + state_mlstm_m.shape[2:], F32)
    conv0 = jnp.zeros((n_odd, bp) + state_conv.shape[2:], F32)
    y_p, (pool_p, c_p, n_p, m_p, kv_p, win_p, conv_p) = _trunk(
        x_prompt, 0, pool0, c0, n0, m0, None, None, None, conv0, p)
    y_s, (pool_s, c_s, n_s, m_s, kv_s, win_s, conv_s) = _trunk(
        x_sample, past_len, state_pool, state_mlstm_c, state_mlstm_n, state_mlstm_m,
        cache_nsa_kv, page_table, state_win_kv, state_conv, p)
    return (y_p, y_s, pool_p, pool_s, c_p, c_s, n_p, n_s, m_p, m_s,
            kv_p, kv_s, win_p, win_s, conv_p, conv_s)
```
